```python
import math
import jax, jax.numpy as jnp
from jax import lax
import numpy as np

D_MODEL = 1024
BATCH = 8
SEQ = 4096
DEPTH = 1

POOL_WIDTH = D_MODEL
POOL_GROUPS = 4
POOL_GROUP_WIDTH = POOL_WIDTH // POOL_GROUPS
POOL_WINDOWS = (2, 4, 8, 16)
SSD_INNER = 2 * D_MODEL
SSD_HEAD_DIM = 64
SSD_HEADS = SSD_INNER // SSD_HEAD_DIM
SSD_GROUPS = 4
SSD_STATE = 128
SSD_CONV = 4
SSD_CHUNK = 128
SSD_CONV_DIM = SSD_INNER + 2 * SSD_GROUPS * SSD_STATE
N_BRANCH = 2
IN_COLS = POOL_WIDTH + SSD_INNER + SSD_CONV_DIM + SSD_HEADS + N_BRANCH * D_MODEL
MOE_GROUPS = 4
EXPERTS_PER_GROUP = 8
N_EXPERTS = MOE_GROUPS * EXPERTS_PER_GROUP
MOE_TOPK = 2
EXPERT_FF = 512
MOE_BLOCK = 256
EPS = 1e-6

kernel_name = 'hybrid_pool_ssd_hiermoe_adaln'


def rmsnorm(x, g):
    x32 = x.astype(jnp.float32)
    y = x32 * lax.rsqrt(jnp.mean(x32 * x32, axis=-1, keepdims=True) + EPS)
    return (y * g.astype(jnp.float32)).astype(x.dtype)


def multiscale_pool(xp, pool_w, pool_scale):
    b, s, _ = xp.shape
    xg = xp.reshape(b, s, POOL_GROUPS, POOL_GROUP_WIDTH).astype(jnp.float32)
    cs = jnp.cumsum(xg, axis=1)
    pos = jnp.arange(s)
    outs = []
    for gi, w in enumerate(POOL_WINDOWS):
        csg = cs[:, :, gi]
        lagged = jnp.pad(csg, ((0, 0), (w, 0), (0, 0)))[:, :s]
        cnt = jnp.minimum(pos + 1, w).astype(jnp.float32)[None, :, None]
        outs.append((csg - lagged) / cnt - xg[:, :, gi])
    d = jnp.stack(outs, axis=2).astype(xp.dtype)
    y = jnp.einsum('bsgc,gcd->bsgd', d, pool_w)
    return y.reshape(b, s, POOL_WIDTH) * pool_scale


def causal_dwconv(x, w, bias):
    out = lax.conv_general_dilated(x, w[:, None, :].astype(x.dtype), window_strides=(1,),
                                   padding=[(SSD_CONV - 1, 0)],
                                   dimension_numbers=('NWC', 'WIO', 'NWC'),
                                   feature_group_count=x.shape[-1])
    return out + bias


def ssd_chunked(X, Adt, Bm, Cm):
    b, s, h, p = X.shape
    g, n = Bm.shape[2], Bm.shape[3]
    r = h // g
    nc = s // SSD_CHUNK
    Q = SSD_CHUNK
    X = X.reshape(b, nc, Q, g, r, p)
    A = Adt.reshape(b, nc, Q, g, r).transpose(0, 3, 4, 1, 2)
    Bc = Bm.reshape(b, nc, Q, g, n)
    Cc = Cm.reshape(b, nc, Q, g, n)
    A_cs = jnp.cumsum(A, axis=-1)
    seg = A_cs[..., :, None] - A_cs[..., None, :]
    mask = jnp.tril(jnp.ones((Q, Q), dtype=bool))
    L = jnp.exp(jnp.where(mask, seg, -jnp.inf))
    CB = jnp.einsum('bclgn,bcsgn->bgcls', Cc, Bc)
    Y_diag = jnp.einsum('bgrcls,bcsgrp->bclgrp', L * CB[:, :, None], X)
    decay_states = jnp.exp(A_cs[..., -1:] - A_cs)
    states = jnp.einsum('bclgn,bgrcl,bclgrp->bcgrpn', Bc, decay_states, X)
    chunk_decay = jnp.exp(A_cs[..., -1])

    def step(carry, inp):
        st, dec = inp
        return carry * dec[..., None, None] + st, carry

    states_c = states.transpose(1, 0, 2, 3, 4, 5)
    dec_c = chunk_decay.transpose(3, 0, 1, 2)
    _, prev = lax.scan(step, jnp.zeros(states_c.shape[1:], states_c.dtype), (states_c, dec_c))
    prev = prev.transpose(1, 0, 2, 3, 4, 5)
    Y_off = jnp.einsum('bclgn,bcgrpn,bgrcl->bclgrp', Cc, prev, jnp.exp(A_cs))
    return (Y_diag + Y_off).reshape(b, s, h, p)


def ssd_branch(z, xbc, dt_raw, conv_w, conv_b, dt_bias, a_log, d_skip, ssd_norm_g):
    b, s, _ = xbc.shape
    xbc = jax.nn.silu(causal_dwconv(xbc, conv_w, conv_b))
    xs, Bm, Cm = jnp.split(xbc, [SSD_INNER, SSD_INNER + SSD_GROUPS * SSD_STATE], axis=-1)
    xs = xs.reshape(b, s, SSD_HEADS, SSD_HEAD_DIM).astype(jnp.float32)
    Bm = Bm.reshape(b, s, SSD_GROUPS, SSD_STATE).astype(jnp.float32)
    Cm = Cm.reshape(b, s, SSD_GROUPS, SSD_STATE).astype(jnp.float32)
    dt = jax.nn.softplus(dt_raw.astype(jnp.float32) + dt_bias.astype(jnp.float32))
    A = -jnp.exp(a_log.astype(jnp.float32))
    y = ssd_chunked(xs * dt[..., None], dt * A, Bm, Cm) + d_skip.astype(jnp.float32)[:, None] * xs
    y = y.reshape(b, s, SSD_INNER) * jax.nn.silu(z.astype(jnp.float32))
    return rmsnorm(y, ssd_norm_g).astype(z.dtype)


def mixer(h, w_in, pool_w, pool_scale, conv_w, conv_b, dt_bias, a_log, d_skip, ssd_norm_g,
          w_branch_pool, w_branch_ssd, w_out):
    proj = h @ w_in
    i0 = POOL_WIDTH
    i1 = i0 + SSD_INNER
    i2 = i1 + SSD_CONV_DIM
    i3 = i2 + SSD_HEADS
    xp, z, xbc, dt_raw, gates = jnp.split(proj, [i0, i1, i2, i3], axis=-1)
    y_pool = multiscale_pool(xp, pool_w, pool_scale)
    y_ssd = ssd_branch(z, xbc, dt_raw, conv_w, conv_b, dt_bias, a_log, d_skip, ssd_norm_g)
    g_pool, g_ssd = jnp.split(jax.nn.sigmoid(gates), 2, axis=-1)
    u = g_pool * (y_pool @ w_branch_pool) + g_ssd * (y_ssd @ w_branch_ssd)
    return u @ w_out


def hier_moe(h, wg, bg, we, be, w13, w2):
    b, s, d = h.shape
    T = b * s
    ht = h.reshape(T, d)
    p_group = jax.nn.softmax((ht @ wg + bg).astype(jnp.float32), axis=-1)
    gw, gsel = lax.top_k(p_group, 1)
    e_logits = (ht @ we + be).astype(jnp.float32).reshape(T, MOE_GROUPS, EXPERTS_PER_GROUP)
    e_in = jnp.take_along_axis(e_logits, gsel[:, :, None], axis=1)[:, 0]
    ew, ei = lax.top_k(jax.nn.softmax(e_in, axis=-1), MOE_TOPK)
    ew = ew / jnp.sum(ew, axis=-1, keepdims=True)
    weights = (gw * ew).reshape(-1)
    experts = (gsel * EXPERTS_PER_GROUP + ei).reshape(-1).astype(jnp.int32)
    A = T * MOE_TOPK
    P = A + N_EXPERTS * MOE_BLOCK
    n_blocks = P // MOE_BLOCK
    order = jnp.argsort(experts)
    e_sorted = experts[order]
    tok_sorted = (order // MOE_TOPK).astype(jnp.int32)
    w_sorted = weights[order]
    counts = jnp.bincount(experts, length=N_EXPERTS).astype(jnp.int32)
    starts = jnp.cumsum(counts) - counts
    padded = ((counts + MOE_BLOCK - 1) // MOE_BLOCK) * MOE_BLOCK
    pad_ends = jnp.cumsum(padded)
    pad_starts = pad_ends - padded
    dest = pad_starts[e_sorted] + jnp.arange(A, dtype=jnp.int32) - starts[e_sorted]
    x_buf = jnp.zeros((P, d), h.dtype).at[dest].set(ht[tok_sorted])
    w_buf = jnp.zeros((P,), jnp.float32).at[dest].set(w_sorted)
    t_buf = jnp.zeros((P,), jnp.int32).at[dest].set(tok_sorted)
    block_expert = jnp.minimum(
        jnp.searchsorted(pad_ends, jnp.arange(n_blocks, dtype=jnp.int32) * MOE_BLOCK, side='right'),
        N_EXPERTS - 1)

    def expert_block(args):
        xb, e = args
        a, g = jnp.split(xb @ w13[e], 2, axis=-1)
        return (jax.nn.silu(a) * g) @ w2[e]

    y = lax.map(expert_block, (x_buf.reshape(n_blocks, MOE_BLOCK, d), block_expert)).reshape(P, d)
    out = jnp.zeros((T, d), y.dtype).at[t_buf].add(y * w_buf[:, None].astype(y.dtype))
    return out.reshape(b, s, d)


def setup_inputs(seed: int = 0) -> dict:
    key = jax.random.key(seed)
    ks = jax.random.split(key, 32)
    D = D_MODEL
    L = DEPTH
    f32 = jnp.float32
    nrm = lambda k, shape, fan: jax.random.normal(k, shape, f32) * (fan ** -0.5)
    dt0 = jnp.exp(jax.random.uniform(ks[10], (L, SSD_HEADS), f32, math.log(1e-3), math.log(1e-1)))
    return {
        'x': jax.random.normal(ks[0], (BATCH, SEQ, D), f32),
        'c': jax.random.normal(ks[1], (BATCH, D), f32),
        'ada_w': nrm(ks[2], (L, D, 6 * D), D),
        'ada_b': 0.02 * jax.random.normal(ks[3], (L, 6 * D), f32),
        'norm1_g': 1.0 + 0.05 * jax.random.normal(ks[4], (L, D), f32),
        'w_in': nrm(ks[5], (L, D, IN_COLS), D),
        'pool_w': nrm(ks[6], (L, POOL_GROUPS, POOL_GROUP_WIDTH, POOL_GROUP_WIDTH), POOL_GROUP_WIDTH),
        'pool_scale': 1.0 + 0.1 * jax.random.normal(ks[7], (L, POOL_WIDTH), f32),
        'conv_w': nrm(ks[8], (L, SSD_CONV, SSD_CONV_DIM), SSD_CONV),
        'conv_b': 0.02 * jax.random.normal(ks[9], (L, SSD_CONV_DIM), f32),
        'dt_bias': dt0 + jnp.log(-jnp.expm1(-dt0)),
        'a_log': jnp.log(jax.random.uniform(ks[11], (L, SSD_HEADS), f32, 1.0, 16.0)),
        'd_skip': 1.0 + 0.1 * jax.random.normal(ks[12], (L, SSD_HEADS), f32),
        'ssd_norm_g': 1.0 + 0.05 * jax.random.normal(ks[13], (L, SSD_INNER), f32),
        'w_branch_pool': nrm(ks[14], (L, POOL_WIDTH, D), POOL_WIDTH),
        'w_branch_ssd': nrm(ks[15], (L, SSD_INNER, D), SSD_INNER),
        'w_out': nrm(ks[16], (L, D, D), D),
        'norm2_g': 1.0 + 0.05 * jax.random.normal(ks[17], (L, D), f32),
        'router_group_w': nrm(ks[18], (L, D, MOE_GROUPS), D),
        'router_group_b': 0.01 * jax.random.normal(ks[19], (L, MOE_GROUPS), f32),
        'router_expert_w': nrm(ks[20], (L, D, N_EXPERTS), D),
        'router_expert_b': 0.01 * jax.random.normal(ks[21], (L, N_EXPERTS), f32),
        'w13': nrm(ks[22], (L, N_EXPERTS, D, 2 * EXPERT_FF), D),
        'w2': nrm(ks[23], (L, N_EXPERTS, EXPERT_FF, D), EXPERT_FF),
        'final_norm_g': 1.0 + 0.05 * jax.random.normal(ks[24], (D,), f32),
    }


def reference(x, c, ada_w, ada_b, norm1_g, w_in, pool_w, pool_scale, conv_w, conv_b, dt_bias, a_log,
              d_skip, ssd_norm_g, w_branch_pool, w_branch_ssd, w_out, norm2_g, router_group_w,
              router_group_b, router_expert_w, router_expert_b, w13, w2, final_norm_g):
    for l in range(DEPTH):
        mod = jax.nn.silu(c) @ ada_w[l] + ada_b[l]
        sh1, sc1, g1, sh2, sc2, g2 = jnp.split(mod[:, None, :], 6, axis=-1)
        h = rmsnorm(x, norm1_g[l]) * (1.0 + sc1) + sh1
        x = x + g1 * mixer(h, w_in[l], pool_w[l], pool_scale[l], conv_w[l], conv_b[l], dt_bias[l],
                           a_log[l], d_skip[l], ssd_norm_g[l], w_branch_pool[l], w_branch_ssd[l], w_out[l])
        h = rmsnorm(x, norm2_g[l]) * (1.0 + sc2) + sh2
        x = x + g2 * hier_moe(h, router_group_w[l], router_group_b[l], router_expert_w[l],
                              router_expert_b[l], w13[l], w2[l])
    return rmsnorm(x, final_norm_g)
```

```python
import functools

import jax
import jax.numpy as jnp
from jax import lax
from jax.experimental import pallas as pl
from jax.experimental.pallas import tpu as pltpu

F32 = jnp.float32
BF16 = jnp.bfloat16
I32 = jnp.int32

D_MODEL = 1024
POOL_WIDTH = 1024
POOL_GROUPS = 4
POOL_GROUP_WIDTH = 256
POOL_WINDOWS = (2, 4, 8, 16)
SSD_INNER = 2048
SSD_HEAD_DIM = 64
SSD_HEADS = 32
SSD_GROUPS = 4
SSD_STATE = 128
SSD_CONV = 4
SSD_CHUNK = 128
SSD_CONV_DIM = SSD_INNER + 2 * SSD_GROUPS * SSD_STATE
N_EXPERTS = 32
EXPERTS_PER_GROUP = 8
MOE_GROUPS = 4
EXPERT_FF = 512
EPS = 1e-6

LANES = 128
MAIN_COLS = POOL_WIDTH + SSD_INNER + SSD_CONV_DIM + 2 * D_MODEL
COL_Z = POOL_WIDTH
COL_XBC = COL_Z + SSD_INNER
COL_GATE = COL_XBC + SSD_CONV_DIM

TM_IN = 512
TM_MIX = 256
POOL_HALO = 16
CONV_HALO = 8
TS_SCATTER = 512
BM_EXPERT = 512
TC_COMBINE = 256
VMEM_LIMIT = 56 * 1024 * 1024


def _dot(a, b):
    return jnp.dot(a, b, preferred_element_type=F32)


def _hi_lo(x):
    hi = x.astype(BF16)
    lo = (x - hi.astype(F32)).astype(BF16)
    return hi, lo


def _split3(x):
    hi = x.astype(BF16)
    r = x - hi.astype(F32)
    mid = r.astype(BF16)
    lo = (r - mid.astype(F32)).astype(BF16)
    return hi, mid, lo


def _dot3(a, b):
    ah, al = _hi_lo(a)
    bh, bl = _hi_lo(b)
    return _dot(ah, bh) + _dot(ah, bl) + _dot(al, bh)


def _sigmoid(x):
    return 1.0 / (1.0 + jnp.exp(-x))


def _silu(x):
    return x * _sigmoid(x)


def _softplus(x):
    return jnp.maximum(x, 0.0) + jnp.log1p(jnp.exp(-jnp.abs(x)))


def _rmsnorm(x, g):
    return x * lax.rsqrt(jnp.mean(x * x, axis=-1, keepdims=True) + EPS) * g


def _ada_kernel(c_ref, w_ref, b_ref, o_ref):
    o_ref[...] = _dot3(_silu(c_ref[...]), w_ref[...]) + b_ref[...]


def _ada(c, w, b):
    bsz = c.shape[0]
    n = w.shape[1]
    tn = 1024
    return pl.pallas_call(
        _ada_kernel,
        out_shape=jax.ShapeDtypeStruct((bsz, n), F32),
        grid=(n // tn,),
        in_specs=[
            pl.BlockSpec((bsz, D_MODEL), lambda j: (0, 0)),
            pl.BlockSpec((D_MODEL, tn), lambda j: (0, j)),
            pl.BlockSpec((1, tn), lambda j: (0, j)),
        ],
        out_specs=pl.BlockSpec((bsz, tn), lambda j: (0, j)),
        compiler_params=pltpu.CompilerParams(dimension_semantics=("arbitrary",)),
        name="ada",
    )(c, w, b.reshape(1, n))


def _inproj_kernel(x_ref, mod_ref, g_ref, w_ref, wdh_ref, wdl_ref, proj_ref, dt_ref):
    h = _rmsnorm(x_ref[...], g_ref[...]) * (1.0 + mod_ref[0, 1:2, :]) + mod_ref[0, 0:1, :]
    hb, hl = _hi_lo(h)
    cw = 1024
    for j in range(MAIN_COLS // cw):
        proj_ref[:, j * cw:(j + 1) * cw] = _dot(hb, w_ref[:, j * cw:(j + 1) * cw]).astype(BF16)
    dt_ref[...] = _dot(hb, wdh_ref[...]) + _dot(hb, wdl_ref[...]) + _dot(hl, wdh_ref[...])


def _inproj(x2, mod3, g1, w_main, wd_hi, wd_lo, seq):
    t = x2.shape[0]
    tiles_per_seq = seq // TM_IN
    const = lambda i: (0, 0)
    return pl.pallas_call(
        _inproj_kernel,
        out_shape=(jax.ShapeDtypeStruct((t, MAIN_COLS), BF16),
                   jax.ShapeDtypeStruct((t, LANES), F32)),
        grid=(t // TM_IN,),
        in_specs=[
            pl.BlockSpec((TM_IN, D_MODEL), lambda i: (i, 0)),
            pl.BlockSpec((1, 6, D_MODEL), lambda i: (i // tiles_per_seq, 0, 0)),
            pl.BlockSpec((1, D_MODEL), const),
            pl.BlockSpec((D_MODEL, MAIN_COLS), const, pipeline_mode=pl.Buffered(1)),
            pl.BlockSpec((D_MODEL, LANES), const),
            pl.BlockSpec((D_MODEL, LANES), const),
        ],
        out_specs=(pl.BlockSpec((TM_IN, MAIN_COLS), lambda i: (i, 0)),
                   pl.BlockSpec((TM_IN, LANES), lambda i: (i, 0))),
        compiler_params=pltpu.CompilerParams(dimension_semantics=("arbitrary",),
                                             vmem_limit_bytes=VMEM_LIMIT),
        name="inproj",
    )(x2, mod3, g1, w_main, wd_hi, wd_lo)


def _mix_kernel(proj_ref, dt_ref, x_ref, mod_ref, poolw_ref, pools_ref, convw_ref, convb_ref,
                dtb_ref, alog_ref, dskip_ref, ng_ref, wbp_ref, wbs_ref, wout_ref, n2g_ref,
                wrh_ref, wrl_ref, br_ref,
                xnew_ref, h2_ref, ri_ref, rf_ref, cnt_ref,
                xp_ext, xbc_ext, act, state, yssd, cnt_acc):
    b = pl.program_id(0)
    j = pl.program_id(1)
    tm = TM_MIX

    @pl.when(j == 0)
    def _():
        xp_ext[0:POOL_HALO, :] = jnp.zeros((POOL_HALO, POOL_WIDTH), F32)
        xbc_ext[0:CONV_HALO, :] = jnp.zeros((CONV_HALO, SSD_CONV_DIM), F32)
        state[...] = jnp.zeros_like(state)

    @pl.when((j == 0) & (b == 0))
    def _():
        cnt_acc[...] = jnp.zeros_like(cnt_acc)

    xp_ext[POOL_HALO:POOL_HALO + tm, :] = proj_ref[:, 0:POOL_WIDTH].astype(F32)
    xbc_ext[CONV_HALO:CONV_HALO + tm, :] = proj_ref[:, COL_XBC:COL_XBC + SSD_CONV_DIM].astype(F32)

    gw = POOL_GROUP_WIDTH
    pos = j * tm + lax.broadcasted_iota(I32, (tm, gw), 0)
    yp = []
    for gi, w in enumerate(POOL_WINDOWS):
        cols = slice(gi * gw, (gi + 1) * gw)
        cur = xp_ext[POOL_HALO:POOL_HALO + tm, cols]
        acc = cur
        for k in range(1, w):
            acc = acc + xp_ext[POOL_HALO - k:POOL_HALO - k + tm, cols]
        cnt = jnp.minimum(pos + 1, w).astype(F32)
        d = acc / cnt - cur
        yp.append(_dot(d.astype(BF16), poolw_ref[gi]) * pools_ref[:, cols])
    y_pool = jnp.concatenate(yp, axis=1)
    bp = _dot(y_pool.astype(BF16), wbp_ref[...])

    cb = 512
    for c0 in range(0, SSD_CONV_DIM, cb):
        cols = slice(c0, c0 + cb)
        acc = convb_ref[:, cols] + convw_ref[0:1, cols] * xbc_ext[CONV_HALO - 3:CONV_HALO - 3 + tm, cols]
        for k in range(1, SSD_CONV):
            off = CONV_HALO - (SSD_CONV - 1) + k
            acc = acc + convw_ref[k:k + 1, cols] * xbc_ext[off:off + tm, cols]
        act[:, cols] = _silu(acc)

    xp_ext[0:POOL_HALO, :] = xp_ext[tm:tm + POOL_HALO, :]
    xbc_ext[0:CONV_HALO, :] = xbc_ext[tm:tm + CONV_HALO, :]

    q = SSD_CHUNK
    lane = lax.broadcasted_iota(I32, (q, q), 1)
    row = lax.broadcasted_iota(I32, (q, q), 0)
    causal = row >= lane
    triu = jnp.where(row <= lane, 1.0, 0.0).astype(BF16)
    lo_half = lane < SSD_HEAD_DIM
    a_rows = -jnp.exp(alog_ref[...])
    col_b = SSD_INNER
    col_c = SSD_INNER + SSD_GROUPS * SSD_STATE
    heads_per_group = SSD_HEADS // SSD_GROUPS
    for c in range(tm // q):
        r0 = c * q
        dtv = _softplus(dt_ref[r0:r0 + q, :] + dtb_ref[...])
        dt_t = dtv.T
        p1, p2, p3 = _split3(dt_t * a_rows)
        acs_t = _dot(p1, triu) + _dot(p2, triu) + _dot(p3, triu)
        acs = acs_t.T
        a_last = acs_t[:, q - 1:q]
        w_t = dt_t * jnp.exp(a_last - acs_t)
        cd_rows = jnp.exp(jnp.broadcast_to(a_last, (q, q)))
        for g in range(SSD_GROUPS):
            bg = act[r0:r0 + q, col_b + g * SSD_STATE:col_b + (g + 1) * SSD_STATE]
            cg = act[r0:r0 + q, col_c + g * SSD_STATE:col_c + (g + 1) * SSD_STATE]
            cbm = lax.dot_general(cg.astype(BF16), bg.astype(BF16),
                                  (((1,), (1,)), ((), ())), preferred_element_type=F32)
            bg_t = bg.T
            for pq in range(heads_per_group // 2):
                pair = g * (heads_per_group // 2) + pq
                pcols = slice(pair * LANES, (pair + 1) * LANES)
                lhs = []
                bts = []
                for hh in range(2):
                    h = 2 * pair + hh
                    a_l = jnp.broadcast_to(acs[:, h:h + 1], (q, q))
                    a_s = jnp.broadcast_to(acs_t[h:h + 1, :], (q, q))
                    lmat = jnp.where(causal, jnp.exp(a_l - a_s), 0.0)
                    lhs.append((lmat * cbm * dt_t[h:h + 1, :]).astype(BF16))
                    lhs.append((cg * jnp.exp(a_l)).astype(BF16))
                    bts.append((bg_t * w_t[h:h + 1, :]).astype(BF16))
                xs_pair = act[r0:r0 + q, pcols]
                x0 = jnp.where(lo_half, xs_pair, 0.0).astype(BF16)
                x1 = jnp.where(lo_half, 0.0, xs_pair).astype(BF16)
                s_pair = state[:, pcols]
                s0 = jnp.where(lo_half, s_pair, 0.0).astype(BF16)
                s1 = jnp.where(lo_half, 0.0, s_pair).astype(BF16)
                y = _dot(jnp.concatenate(lhs, axis=1), jnp.concatenate([x0, s0, x1, s1], axis=0))
                yssd[r0:r0 + q, pcols] = y
                ds = _dot(jnp.concatenate(bts, axis=1), jnp.concatenate([x0, x1], axis=0))
                cd = jnp.where(lo_half,
                               jnp.broadcast_to(cd_rows[2 * pair:2 * pair + 1, :], (q, q)),
                               jnp.broadcast_to(cd_rows[2 * pair + 1:2 * pair + 2, :], (q, q)))
                state[:, pcols] = s_pair * cd + ds

    y = yssd[...] + dskip_ref[...] * act[:, 0:SSD_INNER]
    y = y * _silu(proj_ref[:, COL_Z:COL_Z + SSD_INNER].astype(F32))
    y = _rmsnorm(y, ng_ref[...])
    bs = _dot(y.astype(BF16), wbs_ref[...])
    g_pool = _sigmoid(proj_ref[:, COL_GATE:COL_GATE + D_MODEL].astype(F32))
    g_ssd = _sigmoid(proj_ref[:, COL_GATE + D_MODEL:COL_GATE + 2 * D_MODEL].astype(F32))
    u = g_pool * bp + g_ssd * bs
    x_new = x_ref[...] + mod_ref[0, 2:3, :] * _dot(u.astype(BF16), wout_ref[...])
    xnew_ref[...] = x_new

    h2 = _rmsnorm(x_new, n2g_ref[...]) * (1.0 + mod_ref[0, 4:5, :]) + mod_ref[0, 3:4, :]
    h2_ref[...] = h2
    hh, hl = _hi_lo(h2)
    logits = _dot(hh, wrh_ref[...]) + _dot(hh, wrl_ref[...]) + _dot(hl, wrh_ref[...]) + br_ref[...]
    ln = lax.broadcasted_iota(I32, (tm, LANES), 1)
    neg = -jnp.inf
    lg = jnp.where(ln < MOE_GROUPS, logits[:, 0:LANES], neg)
    mg = jnp.max(lg, axis=-1, keepdims=True)
    gsel = jnp.min(jnp.where(lg == mg, ln, LANES), axis=-1, keepdims=True)
    gwt = 1.0 / jnp.sum(jnp.exp(lg - mg), axis=-1, keepdims=True)
    in_group = (lax.shift_right_logical(ln, 3) == gsel) & (ln < N_EXPERTS)
    le = jnp.where(in_group, logits[:, LANES:2 * LANES], neg)
    m1 = jnp.max(le, axis=-1, keepdims=True)
    e0 = jnp.min(jnp.where(le == m1, ln, LANES), axis=-1, keepdims=True)
    le2 = jnp.where(ln == e0, neg, le)
    m2 = jnp.max(le2, axis=-1, keepdims=True)
    e1 = jnp.min(jnp.where(le2 == m2, ln, LANES), axis=-1, keepdims=True)
    t21 = jnp.exp(m2 - m1)
    w0 = gwt / (1.0 + t21)
    w1 = gwt * t21 / (1.0 + t21)

    oh0 = jnp.where(ln == e0, 1.0, 0.0)
    oh1 = jnp.where(ln == e1, 1.0, 0.0)
    oh = oh0 + oh1
    tr = lax.broadcasted_iota(I32, (tm, tm), 0)
    tc = lax.broadcasted_iota(I32, (tm, tm), 1)
    before = jnp.where(tc < tr, 1.0, 0.0).astype(BF16)
    pre = _dot(before, oh.astype(BF16)) + cnt_acc[0:1, :]
    rank0 = jnp.sum(pre * oh0, axis=-1, keepdims=True).astype(I32)
    rank1 = jnp.sum(pre * oh1, axis=-1, keepdims=True).astype(I32)
    cnt_acc[...] = cnt_acc[...] + jnp.sum(oh, axis=0, keepdims=True)
    cnt_ref[...] = cnt_acc[...].astype(I32)
    ri_ref[...] = jnp.where(ln == 0, e0, jnp.where(ln == 1, e1,
                            jnp.where(ln == 2, rank0, jnp.where(ln == 3, rank1, 0))))
    rf_ref[...] = jnp.where(ln == 0, w0, jnp.where(ln == 1, w1, 0.0))


def _mix(proj, dt_raw, x2, mod3, consts, bsz, seq):
    t = x2.shape[0]
    tiles = seq // TM_MIX
    tok = lambda b, j: (b * tiles + j, 0)
    const2 = lambda b, j: (0, 0)
    const3 = lambda b, j: (0, 0, 0)

    def cspec(a):
        return pl.BlockSpec(a.shape, const3 if a.ndim == 3 else const2, pipeline_mode=pl.Buffered(1))

    return pl.pallas_call(
        _mix_kernel,
        out_shape=(jax.ShapeDtypeStruct((t, D_MODEL), F32),
                   jax.ShapeDtypeStruct((t, D_MODEL), F32),
                   jax.ShapeDtypeStruct((t, LANES), I32),
                   jax.ShapeDtypeStruct((t, LANES), F32),
                   jax.ShapeDtypeStruct((8, LANES), I32)),
        grid=(bsz, tiles),
        in_specs=[
            pl.BlockSpec((TM_MIX, MAIN_COLS), tok),
            pl.BlockSpec((TM_MIX, LANES), tok),
            pl.BlockSpec((TM_MIX, D_MODEL), tok),
            pl.BlockSpec((1, 6, D_MODEL), lambda b, j: (b, 0, 0)),
        ] + [cspec(a) for a in consts],
        out_specs=(pl.BlockSpec((TM_MIX, D_MODEL), tok),
                   pl.BlockSpec((TM_MIX, D_MODEL), tok),
                   pl.BlockSpec((TM_MIX, LANES), tok),
                   pl.BlockSpec((TM_MIX, LANES), tok),
                   pl.BlockSpec((8, LANES), const2)),
        scratch_shapes=[
            pltpu.VMEM((POOL_HALO + TM_MIX, POOL_WIDTH), F32),
            pltpu.VMEM((CONV_HALO + TM_MIX, SSD_CONV_DIM), F32),
            pltpu.VMEM((TM_MIX, SSD_CONV_DIM), F32),
            pltpu.VMEM((SSD_STATE, SSD_INNER), F32),
            pltpu.VMEM((TM_MIX, SSD_INNER), F32),
            pltpu.VMEM((8, LANES), F32),
        ],
        compiler_params=pltpu.CompilerParams(dimension_semantics=("arbitrary", "arbitrary"),
                                             vmem_limit_bytes=VMEM_LIMIT),
        name="mix",
    )(proj, dt_raw, x2, mod3, *consts)


def _scatter_kernel(ps_ref, idx_ref, h_ref, xb_in_ref, xb_ref, sem):
    del xb_in_ref

    def issue(r, carry):
        for k in range(2):
            dst = ps_ref[idx_ref[0, k, r]] + idx_ref[0, 2 + k, r]
            pltpu.make_async_copy(h_ref.at[pl.ds(r, 1)], xb_ref.at[pl.ds(dst, 1)], sem).start()
        return carry

    lax.fori_loop(0, TS_SCATTER, issue, 0, unroll=8)
    for _ in range(2):
        pltpu.make_async_copy(h_ref, xb_ref.at[pl.ds(0, TS_SCATTER)], sem).wait()


def _scatter(pad_starts, idx, h2, xbuf):
    t = h2.shape[0]
    return pl.pallas_call(
        _scatter_kernel,
        out_shape=jax.ShapeDtypeStruct(xbuf.shape, xbuf.dtype),
        grid_spec=pltpu.PrefetchScalarGridSpec(
            num_scalar_prefetch=1,
            grid=(t // TS_SCATTER,),
            in_specs=[
                pl.BlockSpec((1, 4, TS_SCATTER), lambda i, ps: (i, 0, 0), memory_space=pltpu.SMEM),
                pl.BlockSpec((TS_SCATTER, D_MODEL), lambda i, ps: (i, 0)),
                pl.BlockSpec(memory_space=pl.ANY),
            ],
            out_specs=pl.BlockSpec(memory_space=pl.ANY),
            scratch_shapes=[pltpu.SemaphoreType.DMA(())],
        ),
        input_output_aliases={3: 0},
        compiler_params=pltpu.CompilerParams(dimension_semantics=("arbitrary",)),
        name="moe_scatter",
    )(pad_starts, idx, h2, xbuf)


def _expert_kernel(be_ref, nu_ref, x_ref, w13_ref, w2_ref, y_ref, w13b, w2b):
    i = pl.program_id(0)
    e = be_ref[i]
    prev = be_ref[jnp.maximum(i - 1, 0)]

    @pl.when((i == 0) | (e != prev))
    def _():
        w13b[...] = w13_ref[0].astype(BF16)
        w2b[...] = w2_ref[0].astype(BF16)

    @pl.when(i < nu_ref[0])
    def _():
        x = x_ref[...].astype(BF16)
        a = _dot(x, w13b[:, 0:EXPERT_FF])
        g = _dot(x, w13b[:, EXPERT_FF:2 * EXPERT_FF])
        y_ref[...] = _dot((_silu(a) * g).astype(BF16), w2b[...])

    @pl.when(i >= nu_ref[0])
    def _():
        y_ref[...] = jnp.zeros_like(y_ref)


def _experts(block_expert, n_used, xbuf, w13, w2):
    p = xbuf.shape[0]
    nb = p // BM_EXPERT
    return pl.pallas_call(
        _expert_kernel,
        out_shape=jax.ShapeDtypeStruct((p, D_MODEL), F32),
        grid_spec=pltpu.PrefetchScalarGridSpec(
            num_scalar_prefetch=2,
            grid=(nb,),
            in_specs=[
                pl.BlockSpec((BM_EXPERT, D_MODEL), lambda i, be, nu: (i, 0)),
                pl.BlockSpec((1, D_MODEL, 2 * EXPERT_FF), lambda i, be, nu: (be[i], 0, 0)),
                pl.BlockSpec((1, EXPERT_FF, D_MODEL), lambda i, be, nu: (be[i], 0, 0)),
            ],
            out_specs=pl.BlockSpec((BM_EXPERT, D_MODEL), lambda i, be, nu: (i, 0)),
            scratch_shapes=[pltpu.VMEM((D_MODEL, 2 * EXPERT_FF), BF16),
                            pltpu.VMEM((EXPERT_FF, D_MODEL), BF16)],
        ),
        compiler_params=pltpu.CompilerParams(dimension_semantics=("arbitrary",),
                                             vmem_limit_bytes=VMEM_LIMIT),
        name="moe_experts",
    )(block_expert, n_used, xbuf, w13, w2)


def _combine_kernel(ps_ref, idx_ref, x_ref, rf_ref, mod_ref, fg_ref, yb_ref, out_ref, gbuf, sem):
    def issue(r, carry):
        for k in range(2):
            src = ps_ref[idx_ref[0, k, r]] + idx_ref[0, 2 + k, r]
            pltpu.make_async_copy(yb_ref.at[pl.ds(src, 1)], gbuf.at[k, pl.ds(r, 1)], sem).start()
        return carry

    lax.fori_loop(0, TC_COMBINE, issue, 0, unroll=8)
    for k in range(2):
        pltpu.make_async_copy(yb_ref.at[pl.ds(0, TC_COMBINE)], gbuf.at[k], sem).wait()
    moe = rf_ref[:, 0:1] * gbuf[0] + rf_ref[:, 1:2] * gbuf[1]
    out_ref[...] = _rmsnorm(x_ref[...] + mod_ref[0, 5:6, :] * moe, fg_ref[...])


def _combine(pad_starts, idx, x_new, rf, mod3, fg, ybuf, seq):
    t = x_new.shape[0]
    tiles_per_seq = seq // TC_COMBINE
    return pl.pallas_call(
        _combine_kernel,
        out_shape=jax.ShapeDtypeStruct((t, D_MODEL), F32),
        grid_spec=pltpu.PrefetchScalarGridSpec(
            num_scalar_prefetch=1,
            grid=(t // TC_COMBINE,),
            in_specs=[
                pl.BlockSpec((1, 4, TC_COMBINE), lambda i, ps: (i, 0, 0), memory_space=pltpu.SMEM),
                pl.BlockSpec((TC_COMBINE, D_MODEL), lambda i, ps: (i, 0)),
                pl.BlockSpec((TC_COMBINE, LANES), lambda i, ps: (i, 0)),
                pl.BlockSpec((1, 6, D_MODEL), lambda i, ps: (i // tiles_per_seq, 0, 0)),
                pl.BlockSpec((1, D_MODEL), lambda i, ps: (0, 0)),
                pl.BlockSpec(memory_space=pl.ANY),
            ],
            out_specs=pl.BlockSpec((TC_COMBINE, D_MODEL), lambda i, ps: (i, 0)),
            scratch_shapes=[pltpu.VMEM((2, TC_COMBINE, D_MODEL), F32),
                            pltpu.SemaphoreType.DMA(())],
        ),
        compiler_params=pltpu.CompilerParams(dimension_semantics=("arbitrary",)),
        name="moe_combine",
    )(pad_starts, idx, x_new, rf, mod3, fg, ybuf)


def _pad_lanes(a, n=LANES):
    return jnp.pad(a, [(0, 0)] * (a.ndim - 1) + [(0, n - a.shape[-1])])


def _layer(x2, c, bsz, seq, ada_w, ada_b, norm1_g, w_in, pool_w, pool_scale, conv_w, conv_b, dt_bias,
           a_log, d_skip, ssd_norm_g, w_branch_pool, w_branch_ssd, w_out, norm2_g, router_group_w,
           router_group_b, router_expert_w, router_expert_b, w13, w2):
    t = bsz * seq
    mod3 = _ada(c, ada_w, ada_b).reshape(bsz, 6, D_MODEL)

    i2 = POOL_WIDTH + SSD_INNER + SSD_CONV_DIM
    i3 = i2 + SSD_HEADS
    w_main = jnp.concatenate([w_in[:, :i2], w_in[:, i3:]], axis=1).astype(BF16)
    wd = _pad_lanes(w_in[:, i2:i3])
    wd_hi = wd.astype(BF16)
    wd_lo = (wd - wd_hi.astype(F32)).astype(BF16)
    proj, dt_raw = _inproj(x2, mod3, norm1_g.reshape(1, D_MODEL), w_main, wd_hi, wd_lo, seq)

    wr = jnp.concatenate([_pad_lanes(router_group_w), _pad_lanes(router_expert_w)], axis=1)
    wr_hi = wr.astype(BF16)
    wr_lo = (wr - wr_hi.astype(F32)).astype(BF16)
    br = jnp.concatenate([_pad_lanes(router_group_b.reshape(1, -1)),
                          _pad_lanes(router_expert_b.reshape(1, -1))], axis=1)
    consts = [
        pool_w.astype(BF16),
        pool_scale.reshape(1, POOL_WIDTH),
        conv_w,
        conv_b.reshape(1, SSD_CONV_DIM),
        _pad_lanes(dt_bias.reshape(1, SSD_HEADS)),
        jnp.broadcast_to(jnp.pad(a_log, (0, LANES - SSD_HEADS))[:, None], (LANES, LANES)),
        jnp.repeat(d_skip, SSD_HEAD_DIM).reshape(1, SSD_INNER),
        ssd_norm_g.reshape(1, SSD_INNER),
        w_branch_pool.astype(BF16),
        w_branch_ssd.astype(BF16),
        w_out.astype(BF16),
        norm2_g.reshape(1, D_MODEL),
        wr_hi, wr_lo, br,
    ]
    x_new, h2, ri, rf, cnt = _mix(proj, dt_raw, x2, mod3, consts, bsz, seq)

    counts = cnt[0, :N_EXPERTS]
    padded = ((counts + BM_EXPERT - 1) // BM_EXPERT) * BM_EXPERT
    pad_ends = jnp.cumsum(padded)
    pad_starts = (pad_ends - padded).astype(I32)
    n_rows = 2 * t + N_EXPERTS * BM_EXPERT
    nb = n_rows // BM_EXPERT
    block_expert = jnp.minimum(
        jnp.searchsorted(pad_ends, jnp.arange(nb, dtype=I32) * BM_EXPERT, side='right'),
        N_EXPERTS - 1).astype(I32)
    n_used = (pad_ends[-1:] // BM_EXPERT).astype(I32)

    idx4 = ri[:, 0:4]
    idx_s = idx4.reshape(t // TS_SCATTER, TS_SCATTER, 4).transpose(0, 2, 1)
    idx_c = idx4.reshape(t // TC_COMBINE, TC_COMBINE, 4).transpose(0, 2, 1)
    xbuf = _scatter(pad_starts, idx_s, h2, jnp.zeros((n_rows, D_MODEL), F32))
    ybuf = _experts(block_expert, n_used, xbuf, w13, w2)
    return pad_starts, idx_c, x_new, rf, mod3, ybuf


def kernel(x, c, ada_w, ada_b, norm1_g, w_in, pool_w, pool_scale, conv_w, conv_b, dt_bias, a_log, d_skip, ssd_norm_g, w_branch_pool, w_branch_ssd, w_out, norm2_g, router_group_w, router_group_b, router_expert_w, router_expert_b, w13, w2, final_norm_g):
    bsz, seq, d = x.shape
    depth = ada_w.shape[0]
    assert depth == 1 and d == D_MODEL and seq % TM_IN == 0
    x2 = x.reshape(bsz * seq, d)
    l = 0
    pad_starts, idx_c, x_new, rf, mod3, ybuf = _layer(
        x2, c, bsz, seq, ada_w[l], ada_b[l], norm1_g[l], w_in[l], pool_w[l], pool_scale[l], conv_w[l],
        conv_b[l], dt_bias[l], a_log[l], d_skip[l], ssd_norm_g[l], w_branch_pool[l], w_branch_ssd[l],
        w_out[l], norm2_g[l], router_group_w[l], router_group_b[l], router_expert_w[l],
        router_expert_b[l], w13[l], w2[l])
    out = _combine(pad_starts, idx_c, x_new, rf, mod3, final_norm_g.reshape(1, d), ybuf, seq)
    return out.reshape(bsz, seq, d)
```

```python
import functools

import jax
import jax.numpy as jnp
from jax import lax
from jax.experimental import pallas as pl
from jax.experimental.pallas import tpu as pltpu

F32 = jnp.float32
BF16 = jnp.bfloat16
I32 = jnp.int32

D_MODEL = 1024
POOL_WIDTH = 1024
POOL_GROUPS = 4
POOL_GROUP_WIDTH = 256
POOL_WINDOWS = (2, 4, 8, 16)
SSD_INNER = 2048
SSD_HEAD_DIM = 64
SSD_HEADS = 32
SSD_GROUPS = 4
SSD_STATE = 128
SSD_CONV = 4
SSD_CHUNK = 128
SSD_CONV_DIM = SSD_INNER + 2 * SSD_GROUPS * SSD_STATE
N_EXPERTS = 32
EXPERTS_PER_GROUP = 8
MOE_GROUPS = 4
EXPERT_FF = 512
EPS = 1e-6

LANES = 128
MAIN_COLS = POOL_WIDTH + SSD_INNER + SSD_CONV_DIM + 2 * D_MODEL
COL_Z = POOL_WIDTH
COL_XBC = COL_Z + SSD_INNER
COL_GATE = COL_XBC + SSD_CONV_DIM

TM_IN = 512
TM_MIX = 256
POOL_HALO = 16
CONV_HALO = 8
TP_PLAN = 2048
BM_EXPERT = 512
TC_COMBINE = 512
VMEM_LIMIT = 56 * 1024 * 1024


def _dot(a, b):
    return jnp.dot(a, b, preferred_element_type=F32)


def _hi_lo(x):
    hi = x.astype(BF16)
    lo = (x - hi.astype(F32)).astype(BF16)
    return hi, lo


def _split3(x):
    hi = x.astype(BF16)
    r = x - hi.astype(F32)
    mid = r.astype(BF16)
    lo = (r - mid.astype(F32)).astype(BF16)
    return hi, mid, lo


def _dot3(a, b):
    ah, al = _hi_lo(a)
    bh, bl = _hi_lo(b)
    return _dot(ah, bh) + _dot(ah, bl) + _dot(al, bh)


def _sigmoid(x):
    return 1.0 / (1.0 + jnp.exp(-x))


def _silu(x):
    return x * _sigmoid(x)


def _softplus(x):
    return jnp.maximum(x, 0.0) + jnp.log1p(jnp.exp(-jnp.abs(x)))


def _rmsnorm(x, g):
    return x * lax.rsqrt(jnp.mean(x * x, axis=-1, keepdims=True) + EPS) * g


def _ada_kernel(c_ref, w_ref, b_ref, o_ref):
    o_ref[...] = _dot3(_silu(c_ref[...]), w_ref[...]) + b_ref[...]


def _ada(c, w, b):
    bsz = c.shape[0]
    n = w.shape[1]
    tn = 1024
    return pl.pallas_call(
        _ada_kernel,
        out_shape=jax.ShapeDtypeStruct((bsz, n), F32),
        grid=(n // tn,),
        in_specs=[
            pl.BlockSpec((bsz, D_MODEL), lambda j: (0, 0)),
            pl.BlockSpec((D_MODEL, tn), lambda j: (0, j)),
            pl.BlockSpec((1, tn), lambda j: (0, j)),
        ],
        out_specs=pl.BlockSpec((bsz, tn), lambda j: (0, j)),
        compiler_params=pltpu.CompilerParams(dimension_semantics=("arbitrary",)),
        name="ada",
    )(c, w, b.reshape(1, n))


def _inproj_kernel(x_ref, mod_ref, g_ref, w_ref, wdh_ref, wdl_ref, proj_ref, dt_ref):
    h = _rmsnorm(x_ref[...], g_ref[...]) * (1.0 + mod_ref[0, 1:2, :]) + mod_ref[0, 0:1, :]
    hb, hl = _hi_lo(h)
    cw = 1024
    for j in range(MAIN_COLS // cw):
        proj_ref[:, j * cw:(j + 1) * cw] = _dot(hb, w_ref[:, j * cw:(j + 1) * cw]).astype(BF16)
    dt_ref[...] = _dot(hb, wdh_ref[...]) + _dot(hb, wdl_ref[...]) + _dot(hl, wdh_ref[...])


def _inproj(x2, mod3, g1, w_main, wd_hi, wd_lo, seq):
    t = x2.shape[0]
    tiles_per_seq = seq // TM_IN
    const = lambda i: (0, 0)
    return pl.pallas_call(
        _inproj_kernel,
        out_shape=(jax.ShapeDtypeStruct((t, MAIN_COLS), BF16),
                   jax.ShapeDtypeStruct((t, LANES), F32)),
        grid=(t // TM_IN,),
        in_specs=[
            pl.BlockSpec((TM_IN, D_MODEL), lambda i: (i, 0)),
            pl.BlockSpec((1, 6, D_MODEL), lambda i: (i // tiles_per_seq, 0, 0)),
            pl.BlockSpec((1, D_MODEL), const),
            pl.BlockSpec((D_MODEL, MAIN_COLS), const, pipeline_mode=pl.Buffered(1)),
            pl.BlockSpec((D_MODEL, LANES), const),
            pl.BlockSpec((D_MODEL, LANES), const),
        ],
        out_specs=(pl.BlockSpec((TM_IN, MAIN_COLS), lambda i: (i, 0)),
                   pl.BlockSpec((TM_IN, LANES), lambda i: (i, 0))),
        compiler_params=pltpu.CompilerParams(dimension_semantics=("arbitrary",),
                                             vmem_limit_bytes=VMEM_LIMIT),
        name="inproj",
    )(x2, mod3, g1, w_main, wd_hi, wd_lo)


def _mix_kernel(proj_ref, dt_ref, x_ref, mod_ref, poolw_ref, pools_ref, convw_ref, convb_ref,
                dtb_ref, alog_ref, dskip_ref, ng_ref, wbp_ref, wbs_ref, wout_ref, n2g_ref,
                wrh_ref, wrl_ref, br_ref,
                xnew_ref, h2_ref, ri_ref, rf_ref, cnt_ref,
                xp_ext, xbc_ext, act, state, yssd, cnt_acc):
    b = pl.program_id(0)
    j = pl.program_id(1)
    tm = TM_MIX

    @pl.when(j == 0)
    def _():
        xp_ext[0:POOL_HALO, :] = jnp.zeros((POOL_HALO, POOL_WIDTH), F32)
        xbc_ext[0:CONV_HALO, :] = jnp.zeros((CONV_HALO, SSD_CONV_DIM), F32)
        state[...] = jnp.zeros_like(state)

    @pl.when((j == 0) & (b == 0))
    def _():
        cnt_acc[...] = jnp.zeros_like(cnt_acc)

    xp_ext[POOL_HALO:POOL_HALO + tm, :] = proj_ref[:, 0:POOL_WIDTH].astype(F32)
    xbc_ext[CONV_HALO:CONV_HALO + tm, :] = proj_ref[:, COL_XBC:COL_XBC + SSD_CONV_DIM].astype(F32)

    gw = POOL_GROUP_WIDTH
    pos = j * tm + lax.broadcasted_iota(I32, (tm, gw), 0)
    yp = []
    for gi, w in enumerate(POOL_WINDOWS):
        cols = slice(gi * gw, (gi + 1) * gw)
        cur = xp_ext[POOL_HALO:POOL_HALO + tm, cols]
        acc = cur
        for k in range(1, w):
            acc = acc + xp_ext[POOL_HALO - k:POOL_HALO - k + tm, cols]
        cnt = jnp.minimum(pos + 1, w).astype(F32)
        d = acc / cnt - cur
        yp.append(_dot(d.astype(BF16), poolw_ref[gi]) * pools_ref[:, cols])
    y_pool = jnp.concatenate(yp, axis=1)
    bp = _dot(y_pool.astype(BF16), wbp_ref[...])

    cb = 512
    for c0 in range(0, SSD_CONV_DIM, cb):
        cols = slice(c0, c0 + cb)
        acc = convb_ref[:, cols] + convw_ref[0:1, cols] * xbc_ext[CONV_HALO - 3:CONV_HALO - 3 + tm, cols]
        for k in range(1, SSD_CONV):
            off = CONV_HALO - (SSD_CONV - 1) + k
            acc = acc + convw_ref[k:k + 1, cols] * xbc_ext[off:off + tm, cols]
        act[:, cols] = _silu(acc)

    xp_ext[0:POOL_HALO, :] = xp_ext[tm:tm + POOL_HALO, :]
    xbc_ext[0:CONV_HALO, :] = xbc_ext[tm:tm + CONV_HALO, :]

    q = SSD_CHUNK
    lane = lax.broadcasted_iota(I32, (q, q), 1)
    row = lax.broadcasted_iota(I32, (q, q), 0)
    causal = row >= lane
    triu = jnp.where(row <= lane, 1.0, 0.0).astype(BF16)
    lo_half = lane < SSD_HEAD_DIM
    a_rows = -jnp.exp(alog_ref[...])
    col_b = SSD_INNER
    col_c = SSD_INNER + SSD_GROUPS * SSD_STATE
    heads_per_group = SSD_HEADS // SSD_GROUPS
    for c in range(tm // q):
        r0 = c * q
        dtv = _softplus(dt_ref[r0:r0 + q, :] + dtb_ref[...])
        dt_t = dtv.T
        p1, p2, p3 = _split3(dt_t * a_rows)
        acs_t = _dot(p1, triu) + _dot(p2, triu) + _dot(p3, triu)
        acs = acs_t.T
        a_last = acs_t[:, q - 1:q]
        w_t = dt_t * jnp.exp(a_last - acs_t)
        cd_rows = jnp.exp(jnp.broadcast_to(a_last, (q, q)))
        for g in range(SSD_GROUPS):
            bg = act[r0:r0 + q, col_b + g * SSD_STATE:col_b + (g + 1) * SSD_STATE]
            cg = act[r0:r0 + q, col_c + g * SSD_STATE:col_c + (g + 1) * SSD_STATE]
            cbm = lax.dot_general(cg.astype(BF16), bg.astype(BF16),
                                  (((1,), (1,)), ((), ())), preferred_element_type=F32)
            bg_t = bg.T
            for pq in range(heads_per_group // 2):
                pair = g * (heads_per_group // 2) + pq
                pcols = slice(pair * LANES, (pair + 1) * LANES)
                lhs = []
                bts = []
                for hh in range(2):
                    h = 2 * pair + hh
                    a_l = jnp.broadcast_to(acs[:, h:h + 1], (q, q))
                    a_s = jnp.broadcast_to(acs_t[h:h + 1, :], (q, q))
                    lmat = jnp.where(causal, jnp.exp(a_l - a_s), 0.0)
                    lhs.append((lmat * cbm * dt_t[h:h + 1, :]).astype(BF16))
                    lhs.append((cg * jnp.exp(a_l)).astype(BF16))
                    bts.append((bg_t * w_t[h:h + 1, :]).astype(BF16))
                xs_pair = act[r0:r0 + q, pcols]
                x0 = jnp.where(lo_half, xs_pair, 0.0).astype(BF16)
                x1 = jnp.where(lo_half, 0.0, xs_pair).astype(BF16)
                s_pair = state[:, pcols]
                s0 = jnp.where(lo_half, s_pair, 0.0).astype(BF16)
                s1 = jnp.where(lo_half, 0.0, s_pair).astype(BF16)
                y = _dot(jnp.concatenate(lhs, axis=1), jnp.concatenate([x0, s0, x1, s1], axis=0))
                yssd[r0:r0 + q, pcols] = y
                ds = _dot(jnp.concatenate(bts, axis=1), jnp.concatenate([x0, x1], axis=0))
                cd = jnp.where(lo_half,
                               jnp.broadcast_to(cd_rows[2 * pair:2 * pair + 1, :], (q, q)),
                               jnp.broadcast_to(cd_rows[2 * pair + 1:2 * pair + 2, :], (q, q)))
                state[:, pcols] = s_pair * cd + ds

    y = yssd[...] + dskip_ref[...] * act[:, 0:SSD_INNER]
    y = y * _silu(proj_ref[:, COL_Z:COL_Z + SSD_INNER].astype(F32))
    y = _rmsnorm(y, ng_ref[...])
    bs = _dot(y.astype(BF16), wbs_ref[...])
    g_pool = _sigmoid(proj_ref[:, COL_GATE:COL_GATE + D_MODEL].astype(F32))
    g_ssd = _sigmoid(proj_ref[:, COL_GATE + D_MODEL:COL_GATE + 2 * D_MODEL].astype(F32))
    u = g_pool * bp + g_ssd * bs
    x_new = x_ref[...] + mod_ref[0, 2:3, :] * _dot(u.astype(BF16), wout_ref[...])
    xnew_ref[...] = x_new

    h2 = _rmsnorm(x_new, n2g_ref[...]) * (1.0 + mod_ref[0, 4:5, :]) + mod_ref[0, 3:4, :]
    h2_ref[...] = h2
    hh, hl = _hi_lo(h2)
    logits = _dot(hh, wrh_ref[...]) + _dot(hh, wrl_ref[...]) + _dot(hl, wrh_ref[...]) + br_ref[...]
    ln = lax.broadcasted_iota(I32, (tm, LANES), 1)
    neg = -jnp.inf
    lg = jnp.where(ln < MOE_GROUPS, logits[:, 0:LANES], neg)
    mg = jnp.max(lg, axis=-1, keepdims=True)
    gsel = jnp.min(jnp.where(lg == mg, ln, LANES), axis=-1, keepdims=True)
    gwt = 1.0 / jnp.sum(jnp.exp(lg - mg), axis=-1, keepdims=True)
    in_group = (lax.shift_right_logical(ln, 3) == gsel) & (ln < N_EXPERTS)
    le = jnp.where(in_group, logits[:, LANES:2 * LANES], neg)
    m1 = jnp.max(le, axis=-1, keepdims=True)
    e0 = jnp.min(jnp.where(le == m1, ln, LANES), axis=-1, keepdims=True)
    le2 = jnp.where(ln == e0, neg, le)
    m2 = jnp.max(le2, axis=-1, keepdims=True)
    e1 = jnp.min(jnp.where(le2 == m2, ln, LANES), axis=-1, keepdims=True)
    t21 = jnp.exp(m2 - m1)
    w0 = gwt / (1.0 + t21)
    w1 = gwt * t21 / (1.0 + t21)

    oh0 = jnp.where(ln == e0, 1.0, 0.0)
    oh1 = jnp.where(ln == e1, 1.0, 0.0)
    oh = oh0 + oh1
    tr = lax.broadcasted_iota(I32, (tm, tm), 0)
    tc = lax.broadcasted_iota(I32, (tm, tm), 1)
    before = jnp.where(tc < tr, 1.0, 0.0).astype(BF16)
    pre = _dot(before, oh.astype(BF16)) + cnt_acc[0:1, :]
    rank0 = jnp.sum(pre * oh0, axis=-1, keepdims=True).astype(I32)
    rank1 = jnp.sum(pre * oh1, axis=-1, keepdims=True).astype(I32)
    cnt_acc[...] = cnt_acc[...] + jnp.sum(oh, axis=0, keepdims=True)
    cnt_ref[...] = cnt_acc[...].astype(I32)
    ri_ref[...] = jnp.where(ln == 0, e0, jnp.where(ln == 1, e1,
                            jnp.where(ln == 2, rank0, jnp.where(ln == 3, rank1, 0))))
    rf_ref[...] = jnp.where(ln == 0, w0, jnp.where(ln == 1, w1, 0.0))


def _mix(proj, dt_raw, x2, mod3, consts, bsz, seq):
    t = x2.shape[0]
    tiles = seq // TM_MIX
    tok = lambda b, j: (b * tiles + j, 0)
    const2 = lambda b, j: (0, 0)
    const3 = lambda b, j: (0, 0, 0)

    def cspec(a):
        return pl.BlockSpec(a.shape, const3 if a.ndim == 3 else const2, pipeline_mode=pl.Buffered(1))

    return pl.pallas_call(
        _mix_kernel,
        out_shape=(jax.ShapeDtypeStruct((t, D_MODEL), F32),
                   jax.ShapeDtypeStruct((t, D_MODEL), F32),
                   jax.ShapeDtypeStruct((t, LANES), I32),
                   jax.ShapeDtypeStruct((t, LANES), F32),
                   jax.ShapeDtypeStruct((8, LANES), I32)),
        grid=(bsz, tiles),
        in_specs=[
            pl.BlockSpec((TM_MIX, MAIN_COLS), tok),
            pl.BlockSpec((TM_MIX, LANES), tok),
            pl.BlockSpec((TM_MIX, D_MODEL), tok),
            pl.BlockSpec((1, 6, D_MODEL), lambda b, j: (b, 0, 0)),
        ] + [cspec(a) for a in consts],
        out_specs=(pl.BlockSpec((TM_MIX, D_MODEL), tok),
                   pl.BlockSpec((TM_MIX, D_MODEL), tok),
                   pl.BlockSpec((TM_MIX, LANES), tok),
                   pl.BlockSpec((TM_MIX, LANES), tok),
                   pl.BlockSpec((8, LANES), const2)),
        scratch_shapes=[
            pltpu.VMEM((POOL_HALO + TM_MIX, POOL_WIDTH), F32),
            pltpu.VMEM((CONV_HALO + TM_MIX, SSD_CONV_DIM), F32),
            pltpu.VMEM((TM_MIX, SSD_CONV_DIM), F32),
            pltpu.VMEM((SSD_STATE, SSD_INNER), F32),
            pltpu.VMEM((TM_MIX, SSD_INNER), F32),
            pltpu.VMEM((8, LANES), F32),
        ],
        compiler_params=pltpu.CompilerParams(dimension_semantics=("arbitrary", "arbitrary"),
                                             vmem_limit_bytes=VMEM_LIMIT),
        name="mix",
    )(proj, dt_raw, x2, mod3, *consts)


def _plan_kernel(ps_ref, idx_ref, src_ref):
    i = pl.program_id(0)

    @pl.when(i == 0)
    def _():
        def clear(r, carry):
            src_ref[r] = 0
            return carry
        lax.fori_loop(0, src_ref.shape[0], clear, 0, unroll=8)

    def place(r, carry):
        for k in range(2):
            dst = ps_ref[idx_ref[0, k, r]] + idx_ref[0, 2 + k, r]
            src_ref[dst] = 2 * (i * TP_PLAN + r) + k
        return carry

    lax.fori_loop(0, TP_PLAN, place, 0, unroll=8)


def _plan(pad_starts, idx, n_rows):
    nsteps = idx.shape[0]
    return pl.pallas_call(
        _plan_kernel,
        out_shape=jax.ShapeDtypeStruct((n_rows,), I32),
        grid_spec=pltpu.PrefetchScalarGridSpec(
            num_scalar_prefetch=1,
            grid=(nsteps,),
            in_specs=[pl.BlockSpec((1, 4, TP_PLAN), lambda i, ps: (i, 0, 0), memory_space=pltpu.SMEM)],
            out_specs=pl.BlockSpec((n_rows,), lambda i, ps: (0,), memory_space=pltpu.SMEM),
        ),
        compiler_params=pltpu.CompilerParams(dimension_semantics=("arbitrary",)),
        name="moe_plan",
    )(pad_starts, idx)


def _expert_kernel(be_ref, bv_ref, nu_ref, src_ref, srcn_ref, h_ref, w13_ref, w2_ref, out_ref,
                   gbuf, ybuf, w13b, w2b, gsem, ssem):
    i = pl.program_id(0)
    nu = nu_ref[0]
    slot = lax.rem(i, 2)
    bm = BM_EXPERT

    def gather_copy(tok, r, s):
        return pltpu.make_async_copy(h_ref.at[pl.ds(tok, 1)], gbuf.at[s, pl.ds(r, 1)], gsem.at[s])

    def scatter_copy(row, r, s):
        return pltpu.make_async_copy(ybuf.at[s, pl.ds(r, 1)], out_ref.at[pl.ds(row, 1)], ssem.at[s])

    def start_gather(block_src, nvalid, s):
        @pl.when(nvalid < bm)
        def _():
            gbuf[s] = jnp.zeros((bm, D_MODEL), F32)

        def body(r, carry):
            gather_copy(lax.shift_right_logical(block_src[0, 0, r], 1), r, s).start()
            return carry
        lax.fori_loop(0, nvalid, body, 0)

    def wait_rows(nvalid, full_copy, row_copy):
        @pl.when(nvalid == bm)
        def _():
            full_copy.wait()

        @pl.when(nvalid < bm)
        def _():
            def body(r, carry):
                row_copy.wait()
                return carry
            lax.fori_loop(0, nvalid, body, 0)

    def wait_gather(nvalid, s):
        wait_rows(nvalid,
                  pltpu.make_async_copy(h_ref.at[pl.ds(0, bm)], gbuf.at[s], gsem.at[s]),
                  gather_copy(0, 0, s))

    def wait_scatter(nvalid, s):
        wait_rows(nvalid,
                  pltpu.make_async_copy(ybuf.at[s], out_ref.at[pl.ds(0, bm)], ssem.at[s]),
                  scatter_copy(0, 0, s))

    @pl.when(i == 0)
    def _():
        start_gather(src_ref, bv_ref[0], 0)

    @pl.when(i < nu)
    def _():
        @pl.when(i + 1 < nu)
        def _():
            start_gather(srcn_ref, bv_ref[i + 1], 1 - slot)

        @pl.when((i == 0) | (be_ref[i] != be_ref[jnp.maximum(i - 1, 0)]))
        def _():
            w13b[...] = w13_ref[0].astype(BF16)
            w2b[...] = w2_ref[0].astype(BF16)

        nvalid = bv_ref[i]
        wait_gather(nvalid, slot)

        @pl.when(i >= 2)
        def _():
            wait_scatter(bv_ref[jnp.maximum(i - 2, 0)], slot)

        x = gbuf[slot].astype(BF16)
        a = _dot(x, w13b[:, 0:EXPERT_FF])
        g = _dot(x, w13b[:, EXPERT_FF:2 * EXPERT_FF])
        ybuf[slot] = _dot((_silu(a) * g).astype(BF16), w2b[...])

        def body(r, carry):
            scatter_copy(src_ref[0, 0, r], r, slot).start()
            return carry
        lax.fori_loop(0, nvalid, body, 0)

        @pl.when(i == nu - 1)
        def _():
            wait_scatter(nvalid, slot)

            @pl.when(i >= 1)
            def _():
                wait_scatter(bv_ref[jnp.maximum(i - 1, 0)], 1 - slot)


def _experts(block_expert, block_valid, n_used, src3, h2, w13, w2):
    nb = src3.shape[0]
    t = h2.shape[0]
    return pl.pallas_call(
        _expert_kernel,
        out_shape=jax.ShapeDtypeStruct((2 * t, D_MODEL), F32),
        grid_spec=pltpu.PrefetchScalarGridSpec(
            num_scalar_prefetch=3,
            grid=(nb,),
            in_specs=[
                pl.BlockSpec((1, 1, BM_EXPERT), lambda i, be, bv, nu: (i, 0, 0), memory_space=pltpu.SMEM),
                pl.BlockSpec((1, 1, BM_EXPERT), lambda i, be, bv, nu: (jnp.minimum(i + 1, nb - 1), 0, 0),
                             memory_space=pltpu.SMEM),
                pl.BlockSpec(memory_space=pl.ANY),
                pl.BlockSpec((1, D_MODEL, 2 * EXPERT_FF), lambda i, be, bv, nu: (be[i], 0, 0)),
                pl.BlockSpec((1, EXPERT_FF, D_MODEL), lambda i, be, bv, nu: (be[i], 0, 0)),
            ],
            out_specs=pl.BlockSpec(memory_space=pl.ANY),
            scratch_shapes=[pltpu.VMEM((2, BM_EXPERT, D_MODEL), F32),
                            pltpu.VMEM((2, BM_EXPERT, D_MODEL), F32),
                            pltpu.VMEM((D_MODEL, 2 * EXPERT_FF), BF16),
                            pltpu.VMEM((EXPERT_FF, D_MODEL), BF16),
                            pltpu.SemaphoreType.DMA((2,)),
                            pltpu.SemaphoreType.DMA((2,))],
        ),
        compiler_params=pltpu.CompilerParams(dimension_semantics=("arbitrary",),
                                             vmem_limit_bytes=VMEM_LIMIT),
        name="moe_experts",
    )(block_expert, block_valid, n_used, src3, src3, h2, w13, w2)


def _combine_kernel(x_ref, y_ref, rf_ref, mod_ref, fg_ref, out_ref):
    moe = rf_ref[:, 0:1] * y_ref[:, 0:D_MODEL] + rf_ref[:, 1:2] * y_ref[:, D_MODEL:2 * D_MODEL]
    out_ref[...] = _rmsnorm(x_ref[...] + mod_ref[0, 5:6, :] * moe, fg_ref[...])


def _combine(x_new, y2, rf, mod3, fg, seq):
    t = x_new.shape[0]
    tiles_per_seq = seq // TC_COMBINE
    return pl.pallas_call(
        _combine_kernel,
        out_shape=jax.ShapeDtypeStruct((t, D_MODEL), F32),
        grid=(t // TC_COMBINE,),
        in_specs=[
            pl.BlockSpec((TC_COMBINE, D_MODEL), lambda i: (i, 0)),
            pl.BlockSpec((TC_COMBINE, 2 * D_MODEL), lambda i: (i, 0)),
            pl.BlockSpec((TC_COMBINE, LANES), lambda i: (i, 0)),
            pl.BlockSpec((1, 6, D_MODEL), lambda i: (i // tiles_per_seq, 0, 0)),
            pl.BlockSpec((1, D_MODEL), lambda i: (0, 0)),
        ],
        out_specs=pl.BlockSpec((TC_COMBINE, D_MODEL), lambda i: (i, 0)),
        compiler_params=pltpu.CompilerParams(dimension_semantics=("arbitrary",)),
        name="moe_combine",
    )(x_new, y2, rf, mod3, fg)


def _pad_lanes(a, n=LANES):
    return jnp.pad(a, [(0, 0)] * (a.ndim - 1) + [(0, n - a.shape[-1])])


def _layer(x2, c, bsz, seq, ada_w, ada_b, norm1_g, w_in, pool_w, pool_scale, conv_w, conv_b, dt_bias,
           a_log, d_skip, ssd_norm_g, w_branch_pool, w_branch_ssd, w_out, norm2_g, router_group_w,
           router_group_b, router_expert_w, router_expert_b, w13, w2):
    t = bsz * seq
    mod3 = _ada(c, ada_w, ada_b).reshape(bsz, 6, D_MODEL)

    i2 = POOL_WIDTH + SSD_INNER + SSD_CONV_DIM
    i3 = i2 + SSD_HEADS
    w_main = jnp.concatenate([w_in[:, :i2], w_in[:, i3:]], axis=1).astype(BF16)
    wd = _pad_lanes(w_in[:, i2:i3])
    wd_hi = wd.astype(BF16)
    wd_lo = (wd - wd_hi.astype(F32)).astype(BF16)
    proj, dt_raw = _inproj(x2, mod3, norm1_g.reshape(1, D_MODEL), w_main, wd_hi, wd_lo, seq)

    wr = jnp.concatenate([_pad_lanes(router_group_w), _pad_lanes(router_expert_w)], axis=1)
    wr_hi = wr.astype(BF16)
    wr_lo = (wr - wr_hi.astype(F32)).astype(BF16)
    br = jnp.concatenate([_pad_lanes(router_group_b.reshape(1, -1)),
                          _pad_lanes(router_expert_b.reshape(1, -1))], axis=1)
    consts = [
        pool_w.astype(BF16),
        pool_scale.reshape(1, POOL_WIDTH),
        conv_w,
        conv_b.reshape(1, SSD_CONV_DIM),
        _pad_lanes(dt_bias.reshape(1, SSD_HEADS)),
        jnp.broadcast_to(jnp.pad(a_log, (0, LANES - SSD_HEADS))[:, None], (LANES, LANES)),
        jnp.repeat(d_skip, SSD_HEAD_DIM).reshape(1, SSD_INNER),
        ssd_norm_g.reshape(1, SSD_INNER),
        w_branch_pool.astype(BF16),
        w_branch_ssd.astype(BF16),
        w_out.astype(BF16),
        norm2_g.reshape(1, D_MODEL),
        wr_hi, wr_lo, br,
    ]
    x_new, h2, ri, rf, cnt = _mix(proj, dt_raw, x2, mod3, consts, bsz, seq)

    counts = cnt[0, :N_EXPERTS]
    padded = ((counts + BM_EXPERT - 1) // BM_EXPERT) * BM_EXPERT
    pad_ends = jnp.cumsum(padded)
    pad_starts = (pad_ends - padded).astype(I32)
    n_rows = 2 * t + N_EXPERTS * BM_EXPERT
    nb = n_rows // BM_EXPERT
    block_row0 = jnp.arange(nb, dtype=I32) * BM_EXPERT
    block_expert = jnp.minimum(jnp.sum((pad_ends[None, :] <= block_row0[:, None]).astype(I32), axis=1),
                               N_EXPERTS - 1)
    block_valid = jnp.clip((pad_starts + counts)[block_expert] - block_row0, 0, BM_EXPERT).astype(I32)
    n_used = (pad_ends[-1:] // BM_EXPERT).astype(I32)

    idx_p = ri[:, 0:4].reshape(t // TP_PLAN, TP_PLAN, 4).transpose(0, 2, 1)
    src = _plan(pad_starts, idx_p, n_rows)
    y2 = _experts(block_expert, block_valid, n_used, src.reshape(nb, 1, BM_EXPERT), h2, w13, w2)
    return x_new, y2.reshape(t, 2 * D_MODEL), rf, mod3


def kernel(x, c, ada_w, ada_b, norm1_g, w_in, pool_w, pool_scale, conv_w, conv_b, dt_bias, a_log, d_skip, ssd_norm_g, w_branch_pool, w_branch_ssd, w_out, norm2_g, router_group_w, router_group_b, router_expert_w, router_expert_b, w13, w2, final_norm_g):
    bsz, seq, d = x.shape
    depth = ada_w.shape[0]
    assert depth == 1 and d == D_MODEL and seq % TM_IN == 0
    x2 = x.reshape(bsz * seq, d)
    l = 0
    x_new, y2, rf, mod3 = _layer(
        x2, c, bsz, seq, ada_w[l], ada_b[l], norm1_g[l], w_in[l], pool_w[l], pool_scale[l], conv_w[l],
        conv_b[l], dt_bias[l], a_log[l], d_skip[l], ssd_norm_g[l], w_branch_pool[l], w_branch_ssd[l],
        w_out[l], norm2_g[l], router_group_w[l], router_group_b[l], router_expert_w[l],
        router_expert_b[l], w13[l], w2[l])
    out = _combine(x_new, y2, rf, mod3, final_norm_g.reshape(1, d), seq)
    return out.reshape(bsz, seq, d)
```

```python
import functools

import jax
import jax.numpy as jnp
from jax import lax
from jax.experimental import pallas as pl
from jax.experimental.pallas import tpu as pltpu

F32 = jnp.float32
BF16 = jnp.bfloat16
I32 = jnp.int32

D_MODEL = 1024
POOL_WIDTH = 1024
POOL_GROUPS = 4
POOL_GROUP_WIDTH = 256
POOL_WINDOWS = (2, 4, 8, 16)
SSD_INNER = 2048
SSD_HEAD_DIM = 64
SSD_HEADS = 32
SSD_GROUPS = 4
SSD_STATE = 128
SSD_CONV = 4
SSD_CHUNK = 128
SSD_CONV_DIM = SSD_INNER + 2 * SSD_GROUPS * SSD_STATE
N_EXPERTS = 32
EXPERTS_PER_GROUP = 8
MOE_GROUPS = 4
EXPERT_FF = 512
EPS = 1e-6

LANES = 128
TOK_SUB = D_MODEL // LANES
MAIN_COLS = POOL_WIDTH + SSD_INNER + SSD_CONV_DIM + 2 * D_MODEL
COL_Z = POOL_WIDTH
COL_XBC = COL_Z + SSD_INNER
COL_GATE = COL_XBC + SSD_CONV_DIM

TM_IN = 512
TM_MIX = 256
POOL_HALO = 16
CONV_HALO = 8
TS_SCATTER = 1024
ISSUE_UNROLL = 8
BM_EXPERT = 512
TC_COMBINE = 256
VMEM_LIMIT = 56 * 1024 * 1024


def _dot(a, b):
    return jnp.dot(a, b, preferred_element_type=F32)


def _hi_lo(x):
    hi = x.astype(BF16)
    lo = (x - hi.astype(F32)).astype(BF16)
    return hi, lo


def _split3(x):
    hi = x.astype(BF16)
    r = x - hi.astype(F32)
    mid = r.astype(BF16)
    lo = (r - mid.astype(F32)).astype(BF16)
    return hi, mid, lo


def _dot3(a, b):
    ah, al = _hi_lo(a)
    bh, bl = _hi_lo(b)
    return _dot(ah, bh) + _dot(ah, bl) + _dot(al, bh)


def _sigmoid(x):
    return 1.0 / (1.0 + jnp.exp(-x))


def _silu(x):
    return x * _sigmoid(x)


def _softplus(x):
    return jnp.maximum(x, 0.0) + jnp.log1p(jnp.exp(-jnp.abs(x)))


def _rmsnorm(x, g):
    return x * lax.rsqrt(jnp.mean(x * x, axis=-1, keepdims=True) + EPS) * g


def _store_token_tiles(ref, v):
    n = v.shape[0]
    for j in range(TOK_SUB):
        ref[pl.ds(j, n, stride=TOK_SUB), :] = v[:, j * LANES:(j + 1) * LANES]


def _load_token_tiles(ref):
    n = ref.shape[0] // TOK_SUB
    return jnp.concatenate([ref[pl.ds(j, n, stride=TOK_SUB), :] for j in range(TOK_SUB)], axis=1)


def _token_tile(ref, row):
    return ref.at[pl.ds(pl.multiple_of(row * TOK_SUB, TOK_SUB), TOK_SUB)]


def _ada_kernel(c_ref, w_ref, b_ref, o_ref):
    o_ref[...] = _dot3(_silu(c_ref[...]), w_ref[...]) + b_ref[...]


def _ada(c, w, b):
    bsz = c.shape[0]
    n = w.shape[1]
    tn = 1024
    return pl.pallas_call(
        _ada_kernel,
        out_shape=jax.ShapeDtypeStruct((bsz, n), F32),
        grid=(n // tn,),
        in_specs=[
            pl.BlockSpec((bsz, D_MODEL), lambda j: (0, 0)),
            pl.BlockSpec((D_MODEL, tn), lambda j: (0, j)),
            pl.BlockSpec((1, tn), lambda j: (0, j)),
        ],
        out_specs=pl.BlockSpec((bsz, tn), lambda j: (0, j)),
        compiler_params=pltpu.CompilerParams(dimension_semantics=("arbitrary",)),
        name="ada",
    )(c, w, b.reshape(1, n))


def _inproj_kernel(x_ref, mod_ref, g_ref, w_ref, wdh_ref, wdl_ref, proj_ref, dt_ref):
    h = _rmsnorm(x_ref[...], g_ref[...]) * (1.0 + mod_ref[0, 1:2, :]) + mod_ref[0, 0:1, :]
    hb, hl = _hi_lo(h)
    cw = 1024
    for j in range(MAIN_COLS // cw):
        proj_ref[:, j * cw:(j + 1) * cw] = _dot(hb, w_ref[:, j * cw:(j + 1) * cw]).astype(BF16)
    dt_ref[...] = _dot(hb, wdh_ref[...]) + _dot(hb, wdl_ref[...]) + _dot(hl, wdh_ref[...])


def _inproj(x2, mod3, g1, w_main, wd_hi, wd_lo, seq):
    t = x2.shape[0]
    tiles_per_seq = seq // TM_IN
    const = lambda i: (0, 0)
    return pl.pallas_call(
        _inproj_kernel,
        out_shape=(jax.ShapeDtypeStruct((t, MAIN_COLS), BF16),
                   jax.ShapeDtypeStruct((t, LANES), F32)),
        grid=(t // TM_IN,),
        in_specs=[
            pl.BlockSpec((TM_IN, D_MODEL), lambda i: (i, 0)),
            pl.BlockSpec((1, 6, D_MODEL), lambda i: (i // tiles_per_seq, 0, 0)),
            pl.BlockSpec((1, D_MODEL), const),
            pl.BlockSpec((D_MODEL, MAIN_COLS), const, pipeline_mode=pl.Buffered(1)),
            pl.BlockSpec((D_MODEL, LANES), const),
            pl.BlockSpec((D_MODEL, LANES), const),
        ],
        out_specs=(pl.BlockSpec((TM_IN, MAIN_COLS), lambda i: (i, 0)),
                   pl.BlockSpec((TM_IN, LANES), lambda i: (i, 0))),
        compiler_params=pltpu.CompilerParams(dimension_semantics=("arbitrary",),
                                             vmem_limit_bytes=VMEM_LIMIT),
        name="inproj",
    )(x2, mod3, g1, w_main, wd_hi, wd_lo)


def _mix_kernel(proj_ref, dt_ref, x_ref, mod_ref, poolw_ref, pools_ref, convw_ref, convb_ref,
                dtb_ref, alog_ref, dskip_ref, ng_ref, wbp_ref, wbs_ref, wout_ref, n2g_ref,
                wrh_ref, wrl_ref, br_ref,
                xnew_ref, h2_ref, ri_ref, rf_ref, cnt_ref,
                xp_ext, xbc_ext, act, state, yssd, cnt_acc):
    b = pl.program_id(0)
    j = pl.program_id(1)
    tm = TM_MIX

    @pl.when(j == 0)
    def _():
        xp_ext[0:POOL_HALO, :] = jnp.zeros((POOL_HALO, POOL_WIDTH), F32)
        xbc_ext[0:CONV_HALO, :] = jnp.zeros((CONV_HALO, SSD_CONV_DIM), F32)
        state[...] = jnp.zeros_like(state)

    @pl.when((j == 0) & (b == 0))
    def _():
        cnt_acc[...] = jnp.zeros_like(cnt_acc)

    xp_ext[POOL_HALO:POOL_HALO + tm, :] = proj_ref[:, 0:POOL_WIDTH].astype(F32)
    xbc_ext[CONV_HALO:CONV_HALO + tm, :] = proj_ref[:, COL_XBC:COL_XBC + SSD_CONV_DIM].astype(F32)

    gw = POOL_GROUP_WIDTH
    pos = j * tm + lax.broadcasted_iota(I32, (tm, gw), 0)
    yp = []
    for gi, w in enumerate(POOL_WINDOWS):
        cols = slice(gi * gw, (gi + 1) * gw)
        cur = xp_ext[POOL_HALO:POOL_HALO + tm, cols]
        acc = cur
        for k in range(1, w):
            acc = acc + xp_ext[POOL_HALO - k:POOL_HALO - k + tm, cols]
        cnt = jnp.minimum(pos + 1, w).astype(F32)
        d = acc / cnt - cur
        yp.append(_dot(d.astype(BF16), poolw_ref[gi]) * pools_ref[:, cols])
    y_pool = jnp.concatenate(yp, axis=1)
    bp = _dot(y_pool.astype(BF16), wbp_ref[...])

    cb = 512
    for c0 in range(0, SSD_CONV_DIM, cb):
        cols = slice(c0, c0 + cb)
        acc = convb_ref[:, cols] + convw_ref[0:1, cols] * xbc_ext[CONV_HALO - 3:CONV_HALO - 3 + tm, cols]
        for k in range(1, SSD_CONV):
            off = CONV_HALO - (SSD_CONV - 1) + k
            acc = acc + convw_ref[k:k + 1, cols] * xbc_ext[off:off + tm, cols]
        act[:, cols] = _silu(acc)

    xp_ext[0:POOL_HALO, :] = xp_ext[tm:tm + POOL_HALO, :]
    xbc_ext[0:CONV_HALO, :] = xbc_ext[tm:tm + CONV_HALO, :]

    q = SSD_CHUNK
    lane = lax.broadcasted_iota(I32, (q, q), 1)
    row = lax.broadcasted_iota(I32, (q, q), 0)
    causal = row >= lane
    triu = jnp.where(row <= lane, 1.0, 0.0).astype(BF16)
    lo_half = lane < SSD_HEAD_DIM
    a_rows = -jnp.exp(alog_ref[...])
    col_b = SSD_INNER
    col_c = SSD_INNER + SSD_GROUPS * SSD_STATE
    heads_per_group = SSD_HEADS // SSD_GROUPS
    for c in range(tm // q):
        r0 = c * q
        dtv = _softplus(dt_ref[r0:r0 + q, :] + dtb_ref[...])
        dt_t = dtv.T
        p1, p2, p3 = _split3(dt_t * a_rows)
        acs_t = _dot(p1, triu) + _dot(p2, triu) + _dot(p3, triu)
        acs = acs_t.T
        a_last = acs_t[:, q - 1:q]
        w_t = dt_t * jnp.exp(a_last - acs_t)
        cd_rows = jnp.exp(jnp.broadcast_to(a_last, (q, q)))
        for g in range(SSD_GROUPS):
            bg = act[r0:r0 + q, col_b + g * SSD_STATE:col_b + (g + 1) * SSD_STATE]
            cg = act[r0:r0 + q, col_c + g * SSD_STATE:col_c + (g + 1) * SSD_STATE]
            cbm = lax.dot_general(cg.astype(BF16), bg.astype(BF16),
                                  (((1,), (1,)), ((), ())), preferred_element_type=F32)
            bg_t = bg.T
            for pq in range(heads_per_group // 2):
                pair = g * (heads_per_group // 2) + pq
                pcols = slice(pair * LANES, (pair + 1) * LANES)
                lhs = []
                bts = []
                for hh in range(2):
                    h = 2 * pair + hh
                    a_l = jnp.broadcast_to(acs[:, h:h + 1], (q, q))
                    a_s = jnp.broadcast_to(acs_t[h:h + 1, :], (q, q))
                    lmat = jnp.where(causal, jnp.exp(a_l - a_s), 0.0)
                    lhs.append((lmat * cbm * dt_t[h:h + 1, :]).astype(BF16))
                    lhs.append((cg * jnp.exp(a_l)).astype(BF16))
                    bts.append((bg_t * w_t[h:h + 1, :]).astype(BF16))
                xs_pair = act[r0:r0 + q, pcols]
                x0 = jnp.where(lo_half, xs_pair, 0.0).astype(BF16)
                x1 = jnp.where(lo_half, 0.0, xs_pair).astype(BF16)
                s_pair = state[:, pcols]
                s0 = jnp.where(lo_half, s_pair, 0.0).astype(BF16)
                s1 = jnp.where(lo_half, 0.0, s_pair).astype(BF16)
                y = _dot(jnp.concatenate(lhs, axis=1), jnp.concatenate([x0, s0, x1, s1], axis=0))
                yssd[r0:r0 + q, pcols] = y
                ds = _dot(jnp.concatenate(bts, axis=1), jnp.concatenate([x0, x1], axis=0))
                cd = jnp.where(lo_half,
                               jnp.broadcast_to(cd_rows[2 * pair:2 * pair + 1, :], (q, q)),
                               jnp.broadcast_to(cd_rows[2 * pair + 1:2 * pair + 2, :], (q, q)))
                state[:, pcols] = s_pair * cd + ds

    y = yssd[...] + dskip_ref[...] * act[:, 0:SSD_INNER]
    y = y * _silu(proj_ref[:, COL_Z:COL_Z + SSD_INNER].astype(F32))
    y = _rmsnorm(y, ng_ref[...])
    bs = _dot(y.astype(BF16), wbs_ref[...])
    g_pool = _sigmoid(proj_ref[:, COL_GATE:COL_GATE + D_MODEL].astype(F32))
    g_ssd = _sigmoid(proj_ref[:, COL_GATE + D_MODEL:COL_GATE + 2 * D_MODEL].astype(F32))
    u = g_pool * bp + g_ssd * bs
    x_new = x_ref[...] + mod_ref[0, 2:3, :] * _dot(u.astype(BF16), wout_ref[...])
    xnew_ref[...] = x_new

    h2 = _rmsnorm(x_new, n2g_ref[...]) * (1.0 + mod_ref[0, 4:5, :]) + mod_ref[0, 3:4, :]
    _store_token_tiles(h2_ref, h2)
    hh, hl = _hi_lo(h2)
    logits = _dot(hh, wrh_ref[...]) + _dot(hh, wrl_ref[...]) + _dot(hl, wrh_ref[...]) + br_ref[...]
    ln = lax.broadcasted_iota(I32, (tm, LANES), 1)
    neg = -jnp.inf
    lg = jnp.where(ln < MOE_GROUPS, logits[:, 0:LANES], neg)
    mg = jnp.max(lg, axis=-1, keepdims=True)
    gsel = jnp.min(jnp.where(lg == mg, ln, LANES), axis=-1, keepdims=True)
    gwt = 1.0 / jnp.sum(jnp.exp(lg - mg), axis=-1, keepdims=True)
    in_group = (lax.shift_right_logical(ln, 3) == gsel) & (ln < N_EXPERTS)
    le = jnp.where(in_group, logits[:, LANES:2 * LANES], neg)
    m1 = jnp.max(le, axis=-1, keepdims=True)
    e0 = jnp.min(jnp.where(le == m1, ln, LANES), axis=-1, keepdims=True)
    le2 = jnp.where(ln == e0, neg, le)
    m2 = jnp.max(le2, axis=-1, keepdims=True)
    e1 = jnp.min(jnp.where(le2 == m2, ln, LANES), axis=-1, keepdims=True)
    t21 = jnp.exp(m2 - m1)
    w0 = gwt / (1.0 + t21)
    w1 = gwt * t21 / (1.0 + t21)

    oh0 = jnp.where(ln == e0, 1.0, 0.0)
    oh1 = jnp.where(ln == e1, 1.0, 0.0)
    oh = oh0 + oh1
    tr = lax.broadcasted_iota(I32, (tm, tm), 0)
    tc = lax.broadcasted_iota(I32, (tm, tm), 1)
    before = jnp.where(tc < tr, 1.0, 0.0).astype(BF16)
    pre = _dot(before, oh.astype(BF16)) + cnt_acc[0:1, :]
    rank0 = jnp.sum(pre * oh0, axis=-1, keepdims=True).astype(I32)
    rank1 = jnp.sum(pre * oh1, axis=-1, keepdims=True).astype(I32)
    cnt_acc[...] = cnt_acc[...] + jnp.sum(oh, axis=0, keepdims=True)
    cnt_ref[...] = cnt_acc[...].astype(I32)
    ri_ref[...] = jnp.where(ln == 0, e0, jnp.where(ln == 1, e1,
                            jnp.where(ln == 2, rank0, jnp.where(ln == 3, rank1, 0))))
    rf_ref[...] = jnp.where(ln == 0, w0, jnp.where(ln == 1, w1, 0.0))


def _mix(proj, dt_raw, x2, mod3, consts, bsz, seq):
    t = x2.shape[0]
    tiles = seq // TM_MIX
    tok = lambda b, j: (b * tiles + j, 0)
    const2 = lambda b, j: (0, 0)
    const3 = lambda b, j: (0, 0, 0)

    def cspec(a):
        return pl.BlockSpec(a.shape, const3 if a.ndim == 3 else const2, pipeline_mode=pl.Buffered(1))

    return pl.pallas_call(
        _mix_kernel,
        out_shape=(jax.ShapeDtypeStruct((t, D_MODEL), F32),
                   jax.ShapeDtypeStruct((t * TOK_SUB, LANES), F32),
                   jax.ShapeDtypeStruct((t, LANES), I32),
                   jax.ShapeDtypeStruct((t, LANES), F32),
                   jax.ShapeDtypeStruct((8, LANES), I32)),
        grid=(bsz, tiles),
        in_specs=[
            pl.BlockSpec((TM_MIX, MAIN_COLS), tok),
            pl.BlockSpec((TM_MIX, LANES), tok),
            pl.BlockSpec((TM_MIX, D_MODEL), tok),
            pl.BlockSpec((1, 6, D_MODEL), lambda b, j: (b, 0, 0)),
        ] + [cspec(a) for a in consts],
        out_specs=(pl.BlockSpec((TM_MIX, D_MODEL), tok),
                   pl.BlockSpec((TM_MIX * TOK_SUB, LANES), tok),
                   pl.BlockSpec((TM_MIX, LANES), tok),
                   pl.BlockSpec((TM_MIX, LANES), tok),
                   pl.BlockSpec((8, LANES), const2)),
        scratch_shapes=[
            pltpu.VMEM((POOL_HALO + TM_MIX, POOL_WIDTH), F32),
            pltpu.VMEM((CONV_HALO + TM_MIX, SSD_CONV_DIM), F32),
            pltpu.VMEM((TM_MIX, SSD_CONV_DIM), F32),
            pltpu.VMEM((SSD_STATE, SSD_INNER), F32),
            pltpu.VMEM((TM_MIX, SSD_INNER), F32),
            pltpu.VMEM((8, LANES), F32),
        ],
        compiler_params=pltpu.CompilerParams(dimension_semantics=("arbitrary", "arbitrary"),
                                             vmem_limit_bytes=VMEM_LIMIT),
        name="mix",
    )(proj, dt_raw, x2, mod3, *consts)


def _scatter_kernel(ps_ref, pv_ref, pe_ref, nu_ref, idx_ref, h_ref, xb_ref, zrow, zblk, sem, zsem):
    i = pl.program_id(0)
    ts = TS_SCATTER
    bm = BM_EXPERT
    nb = xb_ref.shape[0] // (bm * TOK_SUB)

    @pl.when(i == 0)
    def _():
        zrow[...] = jnp.zeros_like(zrow)
        zblk[...] = jnp.zeros_like(zblk)

        def row_copy(row):
            return pltpu.make_async_copy(zrow, _token_tile(xb_ref, row), zsem)

        def blk_copy(blk):
            start = pl.multiple_of(blk * (bm * TOK_SUB), bm * TOK_SUB)
            return pltpu.make_async_copy(zblk, xb_ref.at[pl.ds(start, bm * TOK_SUB)], zsem)

        def for_pad_rows(fn):
            def per_expert(e, carry):
                lax.fori_loop(pv_ref[e], pe_ref[e], lambda row, c: fn(row_copy(row), c), 0)
                return carry
            lax.fori_loop(0, N_EXPERTS, per_expert, 0)
            lax.fori_loop(nu_ref[0], nb, lambda blk, c: fn(blk_copy(blk), c), 0)

        def start(cp, c):
            cp.start()
            return c

        def wait(cp, c):
            cp.wait()
            return c

        for_pad_rows(start)
        for_pad_rows(wait)

    def issue(j, carry):
        for u in range(ISSUE_UNROLL):
            r = j * ISSUE_UNROLL + u
            for k in range(2):
                dst = ps_ref[idx_ref[0, k, r]] + idx_ref[0, 2 + k, r]
                pltpu.make_async_copy(_token_tile(h_ref, r), _token_tile(xb_ref, dst), sem).start(priority=k)
        return carry

    lax.fori_loop(0, ts // ISSUE_UNROLL, issue, 0)
    for _ in range(2):
        pltpu.make_async_copy(h_ref, xb_ref.at[pl.ds(0, ts * TOK_SUB)], sem).wait()


def _scatter(pad_starts, pad_valid_end, pad_ends, n_used, idx, h2t, n_rows):
    t = h2t.shape[0] // TOK_SUB
    return pl.pallas_call(
        _scatter_kernel,
        out_shape=jax.ShapeDtypeStruct((n_rows * TOK_SUB, LANES), F32),
        grid_spec=pltpu.PrefetchScalarGridSpec(
            num_scalar_prefetch=4,
            grid=(t // TS_SCATTER,),
            in_specs=[
                pl.BlockSpec((1, 4, TS_SCATTER), lambda i, *_: (i, 0, 0), memory_space=pltpu.SMEM),
                pl.BlockSpec((TS_SCATTER * TOK_SUB, LANES), lambda i, *_: (i, 0)),
            ],
            out_specs=pl.BlockSpec(memory_space=pl.ANY),
            scratch_shapes=[pltpu.VMEM((TOK_SUB, LANES), F32),
                            pltpu.VMEM((BM_EXPERT * TOK_SUB, LANES), F32),
                            pltpu.SemaphoreType.DMA(()),
                            pltpu.SemaphoreType.DMA(())],
        ),
        compiler_params=pltpu.CompilerParams(dimension_semantics=("arbitrary",),
                                             disable_bounds_checks=True),
        name="moe_scatter",
    )(pad_starts, pad_valid_end, pad_ends, n_used, idx, h2t)


def _expert_kernel(be_ref, nu_ref, x_ref, w13_ref, w2_ref, y_ref, w13b, w2b):
    i = pl.program_id(0)

    @pl.when((i == 0) | (be_ref[i] != be_ref[jnp.maximum(i - 1, 0)]))
    def _():
        w13b[...] = w13_ref[0].astype(BF16)
        w2b[...] = w2_ref[0].astype(BF16)

    @pl.when(i < nu_ref[0])
    def _():
        x = _load_token_tiles(x_ref).astype(BF16)
        a = _dot(x, w13b[:, 0:EXPERT_FF])
        g = _dot(x, w13b[:, EXPERT_FF:2 * EXPERT_FF])
        _store_token_tiles(y_ref, _dot((_silu(a) * g).astype(BF16), w2b[...]))

    @pl.when(i >= nu_ref[0])
    def _():
        y_ref[...] = jnp.zeros_like(y_ref)


def _experts(block_expert, n_used, xbuf, w13, w2):
    nb = xbuf.shape[0] // (BM_EXPERT * TOK_SUB)
    used = lambda i, nu: jnp.minimum(i, nu[0] - 1)
    return pl.pallas_call(
        _expert_kernel,
        out_shape=jax.ShapeDtypeStruct(xbuf.shape, F32),
        grid_spec=pltpu.PrefetchScalarGridSpec(
            num_scalar_prefetch=2,
            grid=(nb,),
            in_specs=[
                pl.BlockSpec((BM_EXPERT * TOK_SUB, LANES), lambda i, be, nu: (used(i, nu), 0)),
                pl.BlockSpec((1, D_MODEL, 2 * EXPERT_FF), lambda i, be, nu: (be[i], 0, 0)),
                pl.BlockSpec((1, EXPERT_FF, D_MODEL), lambda i, be, nu: (be[i], 0, 0)),
            ],
            out_specs=pl.BlockSpec((BM_EXPERT * TOK_SUB, LANES), lambda i, be, nu: (i, 0)),
            scratch_shapes=[pltpu.VMEM((D_MODEL, 2 * EXPERT_FF), BF16),
                            pltpu.VMEM((EXPERT_FF, D_MODEL), BF16)],
        ),
        compiler_params=pltpu.CompilerParams(dimension_semantics=("arbitrary",),
                                             vmem_limit_bytes=VMEM_LIMIT),
        name="moe_experts",
    )(block_expert, n_used, xbuf, w13, w2)


def _combine_kernel(ps_ref, idx_ref, idxn_ref, x_ref, rf_ref, mod_ref, fg_ref, yb_ref, out_ref, gbuf, sem):
    i = pl.program_id(0)
    n = pl.num_programs(0)
    tc = TC_COMBINE
    slot = lax.rem(i, 2)

    def start_gather(idx, s):
        def issue(j, carry):
            for u in range(ISSUE_UNROLL):
                r = j * ISSUE_UNROLL + u
                for k in range(2):
                    src = ps_ref[idx[0, k, r]] + idx[0, 2 + k, r]
                    pltpu.make_async_copy(_token_tile(yb_ref, src), _token_tile(gbuf.at[s, k], r),
                                          sem.at[s]).start(priority=k)
            return carry
        lax.fori_loop(0, tc // ISSUE_UNROLL, issue, 0)

    @pl.when(i == 0)
    def _():
        start_gather(idx_ref, 0)

    @pl.when(i + 1 < n)
    def _():
        start_gather(idxn_ref, 1 - slot)

    for k in range(2):
        pltpu.make_async_copy(yb_ref.at[pl.ds(0, tc * TOK_SUB)], gbuf.at[slot, k], sem.at[slot]).wait()
    moe = (rf_ref[:, 0:1] * _load_token_tiles(gbuf.at[slot, 0])
           + rf_ref[:, 1:2] * _load_token_tiles(gbuf.at[slot, 1]))
    out_ref[...] = _rmsnorm(x_ref[...] + mod_ref[0, 5:6, :] * moe, fg_ref[...])


def _combine(pad_starts, idx, x_new, rf, mod3, fg, ybuf, seq):
    t = x_new.shape[0]
    n = t // TC_COMBINE
    tiles_per_seq = seq // TC_COMBINE
    return pl.pallas_call(
        _combine_kernel,
        out_shape=jax.ShapeDtypeStruct((t, D_MODEL), F32),
        grid_spec=pltpu.PrefetchScalarGridSpec(
            num_scalar_prefetch=1,
            grid=(n,),
            in_specs=[
                pl.BlockSpec((1, 4, TC_COMBINE), lambda i, ps: (i, 0, 0), memory_space=pltpu.SMEM),
                pl.BlockSpec((1, 4, TC_COMBINE), lambda i, ps: (jnp.minimum(i + 1, n - 1), 0, 0),
                             memory_space=pltpu.SMEM),
                pl.BlockSpec((TC_COMBINE, D_MODEL), lambda i, ps: (i, 0)),
                pl.BlockSpec((TC_COMBINE, LANES), lambda i, ps: (i, 0)),
                pl.BlockSpec((1, 6, D_MODEL), lambda i, ps: (i // tiles_per_seq, 0, 0)),
                pl.BlockSpec((1, D_MODEL), lambda i, ps: (0, 0)),
                pl.BlockSpec(memory_space=pl.ANY),
            ],
            out_specs=pl.BlockSpec((TC_COMBINE, D_MODEL), lambda i, ps: (i, 0)),
            scratch_shapes=[pltpu.VMEM((2, 2, TC_COMBINE * TOK_SUB, LANES), F32),
                            pltpu.SemaphoreType.DMA((2,))],
        ),
        compiler_params=pltpu.CompilerParams(dimension_semantics=("arbitrary",),
                                             disable_bounds_checks=True),
        name="moe_combine",
    )(pad_starts, idx, idx, x_new, rf, mod3, fg, ybuf)


def _pad_lanes(a, n=LANES):
    return jnp.pad(a, [(0, 0)] * (a.ndim - 1) + [(0, n - a.shape[-1])])


def _layer(x2, c, bsz, seq, ada_w, ada_b, norm1_g, w_in, pool_w, pool_scale, conv_w, conv_b, dt_bias,
           a_log, d_skip, ssd_norm_g, w_branch_pool, w_branch_ssd, w_out, norm2_g, router_group_w,
           router_group_b, router_expert_w, router_expert_b, w13, w2):
    t = bsz * seq
    mod3 = _ada(c, ada_w, ada_b).reshape(bsz, 6, D_MODEL)

    i2 = POOL_WIDTH + SSD_INNER + SSD_CONV_DIM
    i3 = i2 + SSD_HEADS
    w_main = jnp.concatenate([w_in[:, :i2], w_in[:, i3:]], axis=1).astype(BF16)
    wd = _pad_lanes(w_in[:, i2:i3])
    wd_hi = wd.astype(BF16)
    wd_lo = (wd - wd_hi.astype(F32)).astype(BF16)
    proj, dt_raw = _inproj(x2, mod3, norm1_g.reshape(1, D_MODEL), w_main, wd_hi, wd_lo, seq)

    wr = jnp.concatenate([_pad_lanes(router_group_w), _pad_lanes(router_expert_w)], axis=1)
    wr_hi = wr.astype(BF16)
    wr_lo = (wr - wr_hi.astype(F32)).astype(BF16)
    br = jnp.concatenate([_pad_lanes(router_group_b.reshape(1, -1)),
                          _pad_lanes(router_expert_b.reshape(1, -1))], axis=1)
    consts = [
        pool_w.astype(BF16),
        pool_scale.reshape(1, POOL_WIDTH),
        conv_w,
        conv_b.reshape(1, SSD_CONV_DIM),
        _pad_lanes(dt_bias.reshape(1, SSD_HEADS)),
        jnp.broadcast_to(jnp.pad(a_log, (0, LANES - SSD_HEADS))[:, None], (LANES, LANES)),
        jnp.repeat(d_skip, SSD_HEAD_DIM).reshape(1, SSD_INNER),
        ssd_norm_g.reshape(1, SSD_INNER),
        w_branch_pool.astype(BF16),
        w_branch_ssd.astype(BF16),
        w_out.astype(BF16),
        norm2_g.reshape(1, D_MODEL),
        wr_hi, wr_lo, br,
    ]
    x_new, h2, ri, rf, cnt = _mix(proj, dt_raw, x2, mod3, consts, bsz, seq)

    counts = cnt[0, :N_EXPERTS]
    padded = ((counts + BM_EXPERT - 1) // BM_EXPERT) * BM_EXPERT
    pad_ends = jnp.cumsum(padded)
    pad_starts = (pad_ends - padded).astype(I32)
    n_rows = 2 * t + N_EXPERTS * BM_EXPERT
    nb = n_rows // BM_EXPERT
    block_row0 = jnp.arange(nb, dtype=I32) * BM_EXPERT
    block_expert = jnp.minimum(jnp.sum((pad_ends[None, :] <= block_row0[:, None]).astype(I32), axis=1),
                               N_EXPERTS - 1)
    n_used = (pad_ends[-1:] // BM_EXPERT).astype(I32)

    idx4 = ri[:, 0:4]
    idx_s = idx4.reshape(t // TS_SCATTER, TS_SCATTER, 4).transpose(0, 2, 1)
    idx_c = idx4.reshape(t // TC_COMBINE, TC_COMBINE, 4).transpose(0, 2, 1)
    xbuf = _scatter(pad_starts, pad_starts + counts, pad_ends.astype(I32), n_used, idx_s, h2, n_rows)
    ybuf = _experts(block_expert, n_used, xbuf, w13, w2)
    return pad_starts, idx_c, x_new, rf, mod3, ybuf


def kernel(x, c, ada_w, ada_b, norm1_g, w_in, pool_w, pool_scale, conv_w, conv_b, dt_bias, a_log, d_skip, ssd_norm_g, w_branch_pool, w_branch_ssd, w_out, norm2_g, router_group_w, router_group_b, router_expert_w, router_expert_b, w13, w2, final_norm_g):
    bsz, seq, d = x.shape
    depth = ada_w.shape[0]
    assert depth == 1 and d == D_MODEL and seq % TM_IN == 0
    x2 = x.reshape(bsz * seq, d)
    l = 0
    pad_starts, idx_c, x_new, rf, mod3, ybuf = _layer(
        x2, c, bsz, seq, ada_w[l], ada_b[l], norm1_g[l], w_in[l], pool_w[l], pool_scale[l], conv_w[l],
        conv_b[l], dt_bias[l], a_log[l], d_skip[l], ssd_norm_g[l], w_branch_pool[l], w_branch_ssd[l],
        w_out[l], norm2_g[l], router_group_w[l], router_group_b[l], router_expert_w[l],
        router_expert_b[l], w13[l], w2[l])
    out = _combine(pad_starts, idx_c, x_new, rf, mod3, final_norm_g.reshape(1, d), ybuf, seq)
    return out.reshape(bsz, seq, d)
```

```python
import functools

import jax
import jax.numpy as jnp
from jax import lax
from jax.experimental import pallas as pl
from jax.experimental.pallas import tpu as pltpu

F32 = jnp.float32
BF16 = jnp.bfloat16
I32 = jnp.int32

D_MODEL = 1024
POOL_WIDTH = 1024
POOL_GROUPS = 4
POOL_GROUP_WIDTH = 256
POOL_WINDOWS = (2, 4, 8, 16)
SSD_INNER = 2048
SSD_HEAD_DIM = 64
SSD_HEADS = 32
SSD_GROUPS = 4
SSD_STATE = 128
SSD_CONV = 4
SSD_CHUNK = 128
SSD_CONV_DIM = SSD_INNER + 2 * SSD_GROUPS * SSD_STATE
N_EXPERTS = 32
EXPERTS_PER_GROUP = 8
MOE_GROUPS = 4
EXPERT_FF = 512
EPS = 1e-6

LANES = 128
TOK_SUB = D_MODEL // LANES
MAIN_COLS = POOL_WIDTH + SSD_INNER + SSD_CONV_DIM + 2 * D_MODEL
COL_Z = POOL_WIDTH
COL_XBC = COL_Z + SSD_INNER
COL_GATE = COL_XBC + SSD_CONV_DIM

TM_IN = 512
TM_MIX = 256
SEQ_HALO = SSD_CHUNK
LOG2E = 1.4426950408889634
TR_ROUTE = 2048
ROUTE_CHUNK = 256
TS_SCATTER = 1024
ISSUE_UNROLL = 8
BM_EXPERT = 512
TC_COMBINE = 256
VMEM_LIMIT = 56 * 1024 * 1024


def _dot(a, b):
    return jnp.dot(a, b, preferred_element_type=F32)


def _hi_lo(x):
    hi = x.astype(BF16)
    lo = (x - hi.astype(F32)).astype(BF16)
    return hi, lo


def _split3(x):
    hi = x.astype(BF16)
    r = x - hi.astype(F32)
    mid = r.astype(BF16)
    lo = (r - mid.astype(F32)).astype(BF16)
    return hi, mid, lo


def _dot3(a, b):
    ah, al = _hi_lo(a)
    bh, bl = _hi_lo(b)
    return _dot(ah, bh) + _dot(ah, bl) + _dot(al, bh)


def _sigmoid(x):
    return 1.0 / (1.0 + jnp.exp(-x))


def _silu(x):
    return x * _sigmoid(x)


def _softplus(x):
    return jnp.maximum(x, 0.0) + jnp.log1p(jnp.exp(-jnp.abs(x)))


def _rmsnorm(x, g):
    return x * lax.rsqrt(jnp.mean(x * x, axis=-1, keepdims=True) + EPS) * g


def _store_token_tiles(ref, v):
    n = v.shape[0]
    for j in range(TOK_SUB):
        ref[pl.ds(j, n, stride=TOK_SUB), :] = v[:, j * LANES:(j + 1) * LANES]


def _load_token_tiles(ref):
    n = ref.shape[0] // TOK_SUB
    return jnp.concatenate([ref[pl.ds(j, n, stride=TOK_SUB), :] for j in range(TOK_SUB)], axis=1)


def _token_tile(ref, row):
    return ref.at[pl.ds(pl.multiple_of(row * TOK_SUB, TOK_SUB), TOK_SUB)]


def _ada_kernel(c_ref, w_ref, b_ref, o_ref):
    o_ref[...] = _dot3(_silu(c_ref[...]), w_ref[...]) + b_ref[...]


def _ada(c, w, b):
    bsz = c.shape[0]
    n = w.shape[1]
    tn = 1024
    return pl.pallas_call(
        _ada_kernel,
        out_shape=jax.ShapeDtypeStruct((bsz, n), F32),
        grid=(n // tn,),
        in_specs=[
            pl.BlockSpec((bsz, D_MODEL), lambda j: (0, 0)),
            pl.BlockSpec((D_MODEL, tn), lambda j: (0, j)),
            pl.BlockSpec((1, tn), lambda j: (0, j)),
        ],
        out_specs=pl.BlockSpec((bsz, tn), lambda j: (0, j)),
        compiler_params=pltpu.CompilerParams(dimension_semantics=("arbitrary",)),
        name="ada",
    )(c, w, b.reshape(1, n))


def _inproj_kernel(x_ref, mod_ref, g_ref, w_ref, wdh_ref, wdl_ref, proj_ref, dt_ref):
    h = _rmsnorm(x_ref[...], g_ref[...]) * (1.0 + mod_ref[0, 1:2, :]) + mod_ref[0, 0:1, :]
    hb, hl = _hi_lo(h)
    cw = 1024
    for j in range(MAIN_COLS // cw):
        proj_ref[:, j * cw:(j + 1) * cw] = _dot(hb, w_ref[:, j * cw:(j + 1) * cw]).astype(BF16)
    dt_ref[...] = _dot(hb, wdh_ref[...]) + _dot(hb, wdl_ref[...]) + _dot(hl, wdh_ref[...])


def _inproj(x2, mod3, g1, w_main, wd_hi, wd_lo, seq):
    t = x2.shape[0]
    tiles_per_seq = seq // TM_IN
    const = lambda i: (0, 0)
    return pl.pallas_call(
        _inproj_kernel,
        out_shape=(jax.ShapeDtypeStruct((t, MAIN_COLS), BF16),
                   jax.ShapeDtypeStruct((t, LANES), F32)),
        grid=(t // TM_IN,),
        in_specs=[
            pl.BlockSpec((TM_IN, D_MODEL), lambda i: (i, 0)),
            pl.BlockSpec((1, 6, D_MODEL), lambda i: (i // tiles_per_seq, 0, 0)),
            pl.BlockSpec((1, D_MODEL), const),
            pl.BlockSpec((D_MODEL, MAIN_COLS), const, pipeline_mode=pl.Buffered(1)),
            pl.BlockSpec((D_MODEL, LANES), const),
            pl.BlockSpec((D_MODEL, LANES), const),
        ],
        out_specs=(pl.BlockSpec((TM_IN, MAIN_COLS), lambda i: (i, 0)),
                   pl.BlockSpec((TM_IN, LANES), lambda i: (i, 0))),
        compiler_params=pltpu.CompilerParams(dimension_semantics=("arbitrary",),
                                             vmem_limit_bytes=VMEM_LIMIT),
        name="inproj",
    )(x2, mod3, g1, w_main, wd_hi, wd_lo)


def _mix_kernel(proj_ref, dt_ref, x_ref, mod_ref, poolw_ref, pools_ref, convw_ref, convb_ref,
                dtb_ref, alog_ref, dskip_ref, ng_ref, wbp_ref, wbs_ref, wout_ref, n2g_ref,
                wrh_ref, wrl_ref, br_ref, band_ref, shift_ref, expand_ref,
                xnew_ref, h2_ref, lt_ref,
                xp_ext, xbc_ext, act, state, yssd):
    j = pl.program_id(1)
    tm = TM_MIX
    q = SSD_CHUNK
    halo = SEQ_HALO

    @pl.when(j == 0)
    def _():
        xp_ext[0:halo, :] = jnp.zeros((halo, POOL_WIDTH), BF16)
        xbc_ext[0:halo, :] = jnp.zeros((halo, SSD_CONV_DIM), BF16)
        state[...] = jnp.zeros_like(state)

    xp_ext[halo:halo + tm, :] = proj_ref[:, 0:POOL_WIDTH]
    xbc_ext[halo:halo + tm, :] = proj_ref[:, COL_XBC:COL_XBC + SSD_CONV_DIM]

    gw = POOL_GROUP_WIDTH
    pos = j * tm + lax.broadcasted_iota(I32, (tm, gw), 0)
    yp = []
    for gi, w in enumerate(POOL_WINDOWS):
        cols = slice(gi * gw, (gi + 1) * gw)
        acc = jnp.concatenate([_dot(band_ref[gi], xp_ext[r0:r0 + halo + q, cols])
                               for r0 in range(0, tm, q)], axis=0)
        cur = xp_ext[halo:halo + tm, cols].astype(F32)
        d = acc / jnp.minimum(pos + 1, w).astype(F32) - cur
        yp.append(_dot(d.astype(BF16), poolw_ref[gi]) * pools_ref[:, cols])
    bp = _dot(jnp.concatenate(yp, axis=1).astype(BF16), wbp_ref[...])

    cb = 512
    for r0 in range(0, tm, q):
        for c0 in range(0, SSD_CONV_DIM, cb):
            cols = slice(c0, c0 + cb)
            taps = _dot(shift_ref[...], xbc_ext[r0:r0 + halo + q, cols])
            acc = convb_ref[:, cols] + convw_ref[0:1, cols] * taps[0:q]
            for k in range(1, SSD_CONV):
                acc = acc + convw_ref[k:k + 1, cols] * taps[k * q:(k + 1) * q]
            act[r0:r0 + q, cols] = _silu(acc)

    xp_ext[0:halo, :] = xp_ext[tm:tm + halo, :]
    xbc_ext[0:halo, :] = xbc_ext[tm:tm + halo, :]

    lane = lax.broadcasted_iota(I32, (q, q), 1)
    row = lax.broadcasted_iota(I32, (q, q), 0)
    causal = row >= lane
    triu = jnp.where(row <= lane, 1.0, 0.0).astype(BF16)
    lo_half = lane < SSD_HEAD_DIM
    a_rows = -jnp.exp(alog_ref[...])
    col_b = SSD_INNER
    col_c = SSD_INNER + SSD_GROUPS * SSD_STATE
    gcw = SSD_INNER // SSD_GROUPS
    for c in range(tm // q):
        r0 = c * q
        dtv = _softplus(dt_ref[r0:r0 + q, :] + dtb_ref[...])
        dt_t = dtv.T
        p1, p2, p3 = _split3(dt_t * a_rows)
        acs_t = _dot(p1, triu) + _dot(p2, triu) + _dot(p3, triu)
        acs = acs_t.T
        acs2 = acs * LOG2E
        acs2_t = acs_t * LOG2E
        dt_hi, dt_lo = _hi_lo(dtv)
        fac = jnp.concatenate([dt_hi, dt_lo, jnp.exp(acs).astype(BF16),
                               jnp.exp(acs[q - 1:q, :] - acs).astype(BF16)], axis=0)
        fx = _dot(fac, expand_ref[...])
        dt_x = fx[0:q] + fx[q:2 * q]
        ea_x = fx[2 * q:3 * q]
        dec_x = fx[3 * q:4 * q]
        xdt = act[r0:r0 + q, 0:SSD_INNER] * dt_x
        xb = xdt.astype(BF16)
        xdb = (xdt * dec_x).astype(BF16)
        s_all = state[...]
        sb = s_all.astype(BF16)
        for g in range(SSD_GROUPS):
            gcols = slice(g * gcw, (g + 1) * gcw)
            bg = act[r0:r0 + q, col_b + g * SSD_STATE:col_b + (g + 1) * SSD_STATE]
            cg = act[r0:r0 + q, col_c + g * SSD_STATE:col_c + (g + 1) * SSD_STATE].astype(BF16)
            cbm = lax.dot_general(cg, bg.astype(BF16),
                                  (((1,), (1,)), ((), ())), preferred_element_type=F32)
            y_off = _dot(cg, sb[:, gcols]) * ea_x[:, gcols]
            ds = _dot(bg.T.astype(BF16), xdb[:, gcols])
            state[:, gcols] = s_all[:, gcols] * ea_x[q - 1:q, gcols] + ds
            y_diag = []
            for pq in range(gcw // LANES):
                pair = g * (gcw // LANES) + pq
                ms = []
                for hh in range(2):
                    h = 2 * pair + hh
                    a_l = jnp.broadcast_to(acs2[:, h:h + 1], (q, q))
                    a_s = jnp.broadcast_to(acs2_t[h:h + 1, :], (q, q))
                    ms.append((jnp.where(causal, jnp.exp2(a_l - a_s), 0.0) * cbm).astype(BF16))
                x_pair = xb[:, pair * LANES:(pair + 1) * LANES]
                zero = jnp.zeros_like(x_pair)
                y_diag.append(_dot(jnp.concatenate(ms, axis=1),
                                   jnp.concatenate([jnp.where(lo_half, x_pair, zero),
                                                    jnp.where(lo_half, zero, x_pair)], axis=0)))
            yssd[r0:r0 + q, gcols] = jnp.concatenate(y_diag, axis=1) + y_off

    y = yssd[...] + dskip_ref[...] * act[:, 0:SSD_INNER]
    y = y * _silu(proj_ref[:, COL_Z:COL_Z + SSD_INNER].astype(F32))
    y = _rmsnorm(y, ng_ref[...])
    bs = _dot(y.astype(BF16), wbs_ref[...])
    g_pool = _sigmoid(proj_ref[:, COL_GATE:COL_GATE + D_MODEL].astype(F32))
    g_ssd = _sigmoid(proj_ref[:, COL_GATE + D_MODEL:COL_GATE + 2 * D_MODEL].astype(F32))
    u = g_pool * bp + g_ssd * bs
    x_new = x_ref[...] + mod_ref[0, 2:3, :] * _dot(u.astype(BF16), wout_ref[...])
    xnew_ref[...] = x_new

    h2 = _rmsnorm(x_new, n2g_ref[...]) * (1.0 + mod_ref[0, 4:5, :]) + mod_ref[0, 3:4, :]
    _store_token_tiles(h2_ref, h2)
    hh, hl = _hi_lo(h2)
    nt = (((1,), (1,)), ((), ()))
    logits_t = (lax.dot_general(wrh_ref[...], hh, nt, preferred_element_type=F32)
                + lax.dot_general(wrl_ref[...], hh, nt, preferred_element_type=F32)
                + lax.dot_general(wrh_ref[...], hl, nt, preferred_element_type=F32))
    lt_ref[...] = logits_t + jnp.concatenate([br_ref[...]] * (tm // LANES), axis=1)


def _mix(proj, dt_raw, x2, mod3, consts, bsz, seq):
    t = x2.shape[0]
    tiles = seq // TM_MIX
    tok = lambda b, j: (b * tiles + j, 0)
    const2 = lambda b, j: (0, 0)
    const3 = lambda b, j: (0, 0, 0)

    def cspec(a):
        return pl.BlockSpec(a.shape, const3 if a.ndim == 3 else const2, pipeline_mode=pl.Buffered(1))

    return pl.pallas_call(
        _mix_kernel,
        out_shape=(jax.ShapeDtypeStruct((t, D_MODEL), F32),
                   jax.ShapeDtypeStruct((t * TOK_SUB, LANES), F32),
                   jax.ShapeDtypeStruct((2 * LANES, t), F32)),
        grid=(bsz, tiles),
        in_specs=[
            pl.BlockSpec((TM_MIX, MAIN_COLS), tok),
            pl.BlockSpec((TM_MIX, LANES), tok),
            pl.BlockSpec((TM_MIX, D_MODEL), tok),
            pl.BlockSpec((1, 6, D_MODEL), lambda b, j: (b, 0, 0)),
        ] + [cspec(a) for a in consts],
        out_specs=(pl.BlockSpec((TM_MIX, D_MODEL), tok),
                   pl.BlockSpec((TM_MIX * TOK_SUB, LANES), tok),
                   pl.BlockSpec((2 * LANES, TM_MIX), lambda b, j: (0, b * tiles + j))),
        scratch_shapes=[
            pltpu.VMEM((SEQ_HALO + TM_MIX, POOL_WIDTH), BF16),
            pltpu.VMEM((SEQ_HALO + TM_MIX, SSD_CONV_DIM), BF16),
            pltpu.VMEM((TM_MIX, SSD_CONV_DIM), F32),
            pltpu.VMEM((SSD_STATE, SSD_INNER), F32),
            pltpu.VMEM((TM_MIX, SSD_INNER), F32),
        ],
        compiler_params=pltpu.CompilerParams(dimension_semantics=("arbitrary", "arbitrary"),
                                             vmem_limit_bytes=VMEM_LIMIT),
        name="mix",
    )(proj, dt_raw, x2, mod3, *consts)


def _route_kernel(lt_ref, idx_ref, rf_ref, cnt_ref, carry):
    i = pl.program_id(0)
    tr = TR_ROUTE
    ck = ROUTE_CHUNK
    neg = -jnp.inf

    @pl.when(i == 0)
    def _():
        carry[...] = jnp.zeros_like(carry)

    row8 = lax.broadcasted_iota(I32, (8, tr), 0)
    lg = jnp.where(row8 < MOE_GROUPS, lt_ref[0:8, :], neg)
    mg = jnp.max(lg, axis=0, keepdims=True)
    gsel = jnp.min(jnp.where(lg == mg, row8, 8), axis=0, keepdims=True)
    gwt = 1.0 / jnp.sum(jnp.exp(lg - mg), axis=0, keepdims=True)
    rowe = lax.broadcasted_iota(I32, (N_EXPERTS, tr), 0)
    le = jnp.where(lax.shift_right_logical(rowe, 3) == gsel, lt_ref[LANES:LANES + N_EXPERTS, :], neg)
    m1 = jnp.max(le, axis=0, keepdims=True)
    e0 = jnp.min(jnp.where(le == m1, rowe, N_EXPERTS), axis=0, keepdims=True)
    le2 = jnp.where(rowe == e0, neg, le)
    m2 = jnp.max(le2, axis=0, keepdims=True)
    e1 = jnp.min(jnp.where(le2 == m2, rowe, N_EXPERTS), axis=0, keepdims=True)
    t21 = jnp.exp(m2 - m1)
    w0 = gwt / (1.0 + t21)
    w1 = gwt * t21 / (1.0 + t21)
    oh0 = jnp.where(rowe == e0, 1.0, 0.0)
    oh1 = jnp.where(rowe == e1, 1.0, 0.0)
    oh = oh0 + oh1

    kr = lax.broadcasted_iota(I32, (ck, ck), 0)
    kc = lax.broadcasted_iota(I32, (ck, ck), 1)
    before = jnp.where(kr < kc, 1.0, 0.0).astype(BF16)
    wpad = jnp.zeros((LANES - 2, ck), F32)
    for c0 in range(0, tr, ck):
        cs = slice(c0, c0 + ck)
        pre = _dot(oh[:, cs].astype(BF16), before) + carry[:, 0:1]
        idx_ref[2:3, cs] = jnp.sum(pre * oh0[:, cs], axis=0, keepdims=True).astype(I32)
        idx_ref[3:4, cs] = jnp.sum(pre * oh1[:, cs], axis=0, keepdims=True).astype(I32)
        carry[...] = carry[...] + jnp.sum(oh[:, cs], axis=1, keepdims=True)
        rf_ref[cs, :] = jnp.concatenate([w0[:, cs], w1[:, cs], wpad], axis=0).T
    idx_ref[0:1, :] = e0
    idx_ref[1:2, :] = e1
    idx_ref[4:8, :] = jnp.zeros((4, tr), I32)
    cnt_ref[...] = carry[...].astype(I32)


def _route(logits_t):
    t = logits_t.shape[1]
    return pl.pallas_call(
        _route_kernel,
        out_shape=(jax.ShapeDtypeStruct((8, t), I32),
                   jax.ShapeDtypeStruct((t, LANES), F32),
                   jax.ShapeDtypeStruct((N_EXPERTS, LANES), I32)),
        grid=(t // TR_ROUTE,),
        in_specs=[pl.BlockSpec((2 * LANES, TR_ROUTE), lambda i: (0, i))],
        out_specs=(pl.BlockSpec((8, TR_ROUTE), lambda i: (0, i)),
                   pl.BlockSpec((TR_ROUTE, LANES), lambda i: (i, 0)),
                   pl.BlockSpec((N_EXPERTS, LANES), lambda i: (0, 0))),
        scratch_shapes=[pltpu.VMEM((N_EXPERTS, LANES), F32)],
        compiler_params=pltpu.CompilerParams(dimension_semantics=("arbitrary",)),
        name="moe_route",
    )(logits_t)


def _scatter_kernel(ps_ref, pv_ref, pe_ref, nu_ref, idx_ref, h_ref, xb_ref, zrow, zblk, sem, zsem):
    i = pl.program_id(0)
    ts = TS_SCATTER
    bm = BM_EXPERT
    nb = xb_ref.shape[0] // (bm * TOK_SUB)

    @pl.when(i == 0)
    def _():
        zrow[...] = jnp.zeros_like(zrow)
        zblk[...] = jnp.zeros_like(zblk)

        def row_copy(row):
            return pltpu.make_async_copy(zrow, _token_tile(xb_ref, row), zsem)

        def blk_copy(blk):
            start = pl.multiple_of(blk * (bm * TOK_SUB), bm * TOK_SUB)
            return pltpu.make_async_copy(zblk, xb_ref.at[pl.ds(start, bm * TOK_SUB)], zsem)

        def for_pad_rows(fn):
            def per_expert(e, carry):
                lax.fori_loop(pv_ref[e], pe_ref[e], lambda row, c: fn(row_copy(row), c), 0)
                return carry
            lax.fori_loop(0, N_EXPERTS, per_expert, 0)
            lax.fori_loop(nu_ref[0], nb, lambda blk, c: fn(blk_copy(blk), c), 0)

        def start(cp, c):
            cp.start()
            return c

        def wait(cp, c):
            cp.wait()
            return c

        for_pad_rows(start)
        for_pad_rows(wait)

    def issue(j, carry):
        for u in range(ISSUE_UNROLL):
            r = j * ISSUE_UNROLL + u
            for k in range(2):
                dst = ps_ref[idx_ref[k, r]] + idx_ref[2 + k, r]
                pltpu.make_async_copy(_token_tile(h_ref, r), _token_tile(xb_ref, dst), sem).start(priority=k)
        return carry

    lax.fori_loop(0, ts // ISSUE_UNROLL, issue, 0)
    for _ in range(2):
        pltpu.make_async_copy(h_ref, xb_ref.at[pl.ds(0, ts * TOK_SUB)], sem).wait()


def _scatter(pad_starts, pad_valid_end, pad_ends, n_used, idx, h2t, n_rows):
    t = h2t.shape[0] // TOK_SUB
    return pl.pallas_call(
        _scatter_kernel,
        out_shape=jax.ShapeDtypeStruct((n_rows * TOK_SUB, LANES), F32),
        grid_spec=pltpu.PrefetchScalarGridSpec(
            num_scalar_prefetch=4,
            grid=(t // TS_SCATTER,),
            in_specs=[
                pl.BlockSpec((8, TS_SCATTER), lambda i, *_: (0, i), memory_space=pltpu.SMEM),
                pl.BlockSpec((TS_SCATTER * TOK_SUB, LANES), lambda i, *_: (i, 0)),
            ],
            out_specs=pl.BlockSpec(memory_space=pl.ANY),
            scratch_shapes=[pltpu.VMEM((TOK_SUB, LANES), F32),
                            pltpu.VMEM((BM_EXPERT * TOK_SUB, LANES), F32),
                            pltpu.SemaphoreType.DMA(()),
                            pltpu.SemaphoreType.DMA(())],
        ),
        compiler_params=pltpu.CompilerParams(dimension_semantics=("arbitrary",),
                                             disable_bounds_checks=True),
        name="moe_scatter",
    )(pad_starts, pad_valid_end, pad_ends, n_used, idx, h2t)


def _expert_kernel(be_ref, nu_ref, x_ref, w13_ref, w2_ref, y_ref, w13b, w2b):
    i = pl.program_id(0)

    @pl.when((i == 0) | (be_ref[i] != be_ref[jnp.maximum(i - 1, 0)]))
    def _():
        w13b[...] = w13_ref[0].astype(BF16)
        w2b[...] = w2_ref[0].astype(BF16)

    @pl.when(i < nu_ref[0])
    def _():
        x = _load_token_tiles(x_ref).astype(BF16)
        a = _dot(x, w13b[:, 0:EXPERT_FF])
        g = _dot(x, w13b[:, EXPERT_FF:2 * EXPERT_FF])
        _store_token_tiles(y_ref, _dot((_silu(a) * g).astype(BF16), w2b[...]))

    @pl.when(i >= nu_ref[0])
    def _():
        y_ref[...] = jnp.zeros_like(y_ref)


def _experts(block_expert, n_used, xbuf, w13, w2):
    nb = xbuf.shape[0] // (BM_EXPERT * TOK_SUB)
    used = lambda i, nu: jnp.minimum(i, nu[0] - 1)
    return pl.pallas_call(
        _expert_kernel,
        out_shape=jax.ShapeDtypeStruct(xbuf.shape, F32),
        grid_spec=pltpu.PrefetchScalarGridSpec(
            num_scalar_prefetch=2,
            grid=(nb,),
            in_specs=[
                pl.BlockSpec((BM_EXPERT * TOK_SUB, LANES), lambda i, be, nu: (used(i, nu), 0)),
                pl.BlockSpec((1, D_MODEL, 2 * EXPERT_FF), lambda i, be, nu: (be[i], 0, 0)),
                pl.BlockSpec((1, EXPERT_FF, D_MODEL), lambda i, be, nu: (be[i], 0, 0)),
            ],
            out_specs=pl.BlockSpec((BM_EXPERT * TOK_SUB, LANES), lambda i, be, nu: (i, 0)),
            scratch_shapes=[pltpu.VMEM((D_MODEL, 2 * EXPERT_FF), BF16),
                            pltpu.VMEM((EXPERT_FF, D_MODEL), BF16)],
        ),
        compiler_params=pltpu.CompilerParams(dimension_semantics=("arbitrary",),
                                             vmem_limit_bytes=VMEM_LIMIT),
        name="moe_experts",
    )(block_expert, n_used, xbuf, w13, w2)


def _combine_kernel(ps_ref, idx_ref, idxn_ref, x_ref, rf_ref, mod_ref, fg_ref, yb_ref, out_ref, gbuf, sem):
    i = pl.program_id(0)
    n = pl.num_programs(0)
    tc = TC_COMBINE
    slot = lax.rem(i, 2)

    def start_gather(idx, s):
        def issue(j, carry):
            for u in range(ISSUE_UNROLL):
                r = j * ISSUE_UNROLL + u
                for k in range(2):
                    src = ps_ref[idx[k, r]] + idx[2 + k, r]
                    pltpu.make_async_copy(_token_tile(yb_ref, src), _token_tile(gbuf.at[s, k], r),
                                          sem.at[s]).start(priority=k)
            return carry
        lax.fori_loop(0, tc // ISSUE_UNROLL, issue, 0)

    @pl.when(i == 0)
    def _():
        start_gather(idx_ref, 0)

    @pl.when(i + 1 < n)
    def _():
        start_gather(idxn_ref, 1 - slot)

    for k in range(2):
        pltpu.make_async_copy(yb_ref.at[pl.ds(0, tc * TOK_SUB)], gbuf.at[slot, k], sem.at[slot]).wait()
    moe = (rf_ref[:, 0:1] * _load_token_tiles(gbuf.at[slot, 0])
           + rf_ref[:, 1:2] * _load_token_tiles(gbuf.at[slot, 1]))
    out_ref[...] = _rmsnorm(x_ref[...] + mod_ref[0, 5:6, :] * moe, fg_ref[...])


def _combine(pad_starts, idx, x_new, rf, mod3, fg, ybuf, seq):
    t = x_new.shape[0]
    n = t // TC_COMBINE
    tiles_per_seq = seq // TC_COMBINE
    return pl.pallas_call(
        _combine_kernel,
        out_shape=jax.ShapeDtypeStruct((t, D_MODEL), F32),
        grid_spec=pltpu.PrefetchScalarGridSpec(
            num_scalar_prefetch=1,
            grid=(n,),
            in_specs=[
                pl.BlockSpec((8, TC_COMBINE), lambda i, ps: (0, i), memory_space=pltpu.SMEM),
                pl.BlockSpec((8, TC_COMBINE), lambda i, ps: (0, jnp.minimum(i + 1, n - 1)),
                             memory_space=pltpu.SMEM),
                pl.BlockSpec((TC_COMBINE, D_MODEL), lambda i, ps: (i, 0)),
                pl.BlockSpec((TC_COMBINE, LANES), lambda i, ps: (i, 0)),
                pl.BlockSpec((1, 6, D_MODEL), lambda i, ps: (i // tiles_per_seq, 0, 0)),
                pl.BlockSpec((1, D_MODEL), lambda i, ps: (0, 0)),
                pl.BlockSpec(memory_space=pl.ANY),
            ],
            out_specs=pl.BlockSpec((TC_COMBINE, D_MODEL), lambda i, ps: (i, 0)),
            scratch_shapes=[pltpu.VMEM((2, 2, TC_COMBINE * TOK_SUB, LANES), F32),
                            pltpu.SemaphoreType.DMA((2,))],
        ),
        compiler_params=pltpu.CompilerParams(dimension_semantics=("arbitrary",),
                                             disable_bounds_checks=True),
        name="moe_combine",
    )(pad_starts, idx, idx, x_new, rf, mod3, fg, ybuf)


def _pad_lanes(a, n=LANES):
    return jnp.pad(a, [(0, 0)] * (a.ndim - 1) + [(0, n - a.shape[-1])])


def _selection_matrices():
    q = SSD_CHUNK
    rel = jnp.arange(SEQ_HALO + q)[None, :] - jnp.arange(q)[:, None] - SEQ_HALO
    band = jnp.stack([(rel <= 0) & (rel > -w) for w in POOL_WINDOWS]).astype(BF16)
    shift = jnp.concatenate([rel == k - (SSD_CONV - 1) for k in range(SSD_CONV)], axis=0).astype(BF16)
    expand = (jnp.arange(LANES)[:, None] == jnp.arange(SSD_INNER)[None, :] // SSD_HEAD_DIM).astype(BF16)
    return band, shift, expand


def _layer(x2, c, bsz, seq, ada_w, ada_b, norm1_g, w_in, pool_w, pool_scale, conv_w, conv_b, dt_bias,
           a_log, d_skip, ssd_norm_g, w_branch_pool, w_branch_ssd, w_out, norm2_g, router_group_w,
           router_group_b, router_expert_w, router_expert_b, w13, w2):
    t = bsz * seq
    mod3 = _ada(c, ada_w, ada_b).reshape(bsz, 6, D_MODEL)

    i2 = POOL_WIDTH + SSD_INNER + SSD_CONV_DIM
    i3 = i2 + SSD_HEADS
    w_main = jnp.concatenate([w_in[:, :i2], w_in[:, i3:]], axis=1).astype(BF16)
    wd = _pad_lanes(w_in[:, i2:i3])
    wd_hi = wd.astype(BF16)
    wd_lo = (wd - wd_hi.astype(F32)).astype(BF16)
    proj, dt_raw = _inproj(x2, mod3, norm1_g.reshape(1, D_MODEL), w_main, wd_hi, wd_lo, seq)

    wr = jnp.concatenate([_pad_lanes(router_group_w), _pad_lanes(router_expert_w)], axis=1).T
    wr_hi = wr.astype(BF16)
    wr_lo = (wr - wr_hi.astype(F32)).astype(BF16)
    br = jnp.concatenate([_pad_lanes(router_group_b.reshape(1, -1)),
                          _pad_lanes(router_expert_b.reshape(1, -1))], axis=1)
    br = jnp.broadcast_to(br.reshape(2 * LANES, 1), (2 * LANES, LANES))
    consts = [
        pool_w.astype(BF16),
        pool_scale.reshape(1, POOL_WIDTH),
        conv_w,
        conv_b.reshape(1, SSD_CONV_DIM),
        _pad_lanes(dt_bias.reshape(1, SSD_HEADS)),
        jnp.broadcast_to(jnp.pad(a_log, (0, LANES - SSD_HEADS))[:, None], (LANES, LANES)),
        jnp.repeat(d_skip, SSD_HEAD_DIM).reshape(1, SSD_INNER),
        ssd_norm_g.reshape(1, SSD_INNER),
        w_branch_pool.astype(BF16),
        w_branch_ssd.astype(BF16),
        w_out.astype(BF16),
        norm2_g.reshape(1, D_MODEL),
        wr_hi, wr_lo, br,
        *_selection_matrices(),
    ]
    x_new, h2, logits_t = _mix(proj, dt_raw, x2, mod3, consts, bsz, seq)
    idx, rf, cnt = _route(logits_t)

    counts = cnt[:, 0]
    padded = ((counts + BM_EXPERT - 1) // BM_EXPERT) * BM_EXPERT
    pad_ends = jnp.cumsum(padded)
    pad_starts = (pad_ends - padded).astype(I32)
    n_rows = 2 * t + N_EXPERTS * BM_EXPERT
    nb = n_rows // BM_EXPERT
    block_row0 = jnp.arange(nb, dtype=I32) * BM_EXPERT
    block_expert = jnp.minimum(jnp.sum((pad_ends[None, :] <= block_row0[:, None]).astype(I32), axis=1),
                               N_EXPERTS - 1)
    n_used = (pad_ends[-1:] // BM_EXPERT).astype(I32)

    xbuf = _scatter(pad_starts, pad_starts + counts, pad_ends.astype(I32), n_used, idx, h2, n_rows)
    ybuf = _experts(block_expert, n_used, xbuf, w13, w2)
    return pad_starts, idx, x_new, rf, mod3, ybuf


def kernel(x, c, ada_w, ada_b, norm1_g, w_in, pool_w, pool_scale, conv_w, conv_b, dt_bias, a_log, d_skip, ssd_norm_g, w_branch_pool, w_branch_ssd, w_out, norm2_g, router_group_w, router_group_b, router_expert_w, router_expert_b, w13, w2, final_norm_g):
    bsz, seq, d = x.shape
    depth = ada_w.shape[0]
    assert depth == 1 and d == D_MODEL and seq % TM_IN == 0
    x2 = x.reshape(bsz * seq, d)
    l = 0
    pad_starts, idx_c, x_new, rf, mod3, ybuf = _layer(
        x2, c, bsz, seq, ada_w[l], ada_b[l], norm1_g[l], w_in[l], pool_w[l], pool_scale[l], conv_w[l],
        conv_b[l], dt_bias[l], a_log[l], d_skip[l], ssd_norm_g[l], w_branch_pool[l], w_branch_ssd[l],
        w_out[l], norm2_g[l], router_group_w[l], router_group_b[l], router_expert_w[l],
        router_expert_b[l], w13[l], w2[l])
    out = _combine(pad_starts, idx_c, x_new, rf, mod3, final_norm_g.reshape(1, d), ybuf, seq)
    return out.reshape(bsz, seq, d)
```

```python
import functools

import jax
import jax.numpy as jnp
from jax import lax
from jax.experimental import pallas as pl
from jax.experimental.pallas import tpu as pltpu

F32 = jnp.float32
BF16 = jnp.bfloat16
I32 = jnp.int32
U32 = jnp.uint32

D_MODEL = 1024
POOL_WIDTH = 1024
POOL_GROUPS = 4
POOL_GROUP_WIDTH = 256
POOL_WINDOWS = (2, 4, 8, 16)
SSD_INNER = 2048
SSD_HEAD_DIM = 64
SSD_HEADS = 32
SSD_GROUPS = 4
SSD_STATE = 128
SSD_CONV = 4
SSD_CHUNK = 128
SSD_CONV_DIM = SSD_INNER + 2 * SSD_GROUPS * SSD_STATE
N_EXPERTS = 32
EXPERTS_PER_GROUP = 8
MOE_GROUPS = 4
EXPERT_FF = 512
EPS = 1e-6

LANES = 128
TOK_SUB = D_MODEL // (2 * LANES)
MAIN_COLS = POOL_WIDTH + SSD_INNER + SSD_CONV_DIM + 2 * D_MODEL
COL_Z = POOL_WIDTH
COL_XBC = COL_Z + SSD_INNER
COL_GATE = COL_XBC + SSD_CONV_DIM

TM_MIX = 256
PROJ_PIECE = 256
SEQ_HALO = SSD_CHUNK
CONV_HALO = 8
LOG2E = 1.4426950408889634
TR_ROUTE = 2048
ROUTE_CHUNK = 256
TS_SCATTER = 1024
ISSUE_UNROLL = 8
BM_EXPERT = 512
TC_COMBINE = 256
VMEM_LIMIT = 56 * 1024 * 1024
VMEM_LIMIT_PROMIX = 60 * 1024 * 1024


def _dot(a, b):
    return jnp.dot(a, b, preferred_element_type=F32)


def _hi_lo(x):
    hi = x.astype(BF16)
    lo = (x - hi.astype(F32)).astype(BF16)
    return hi, lo


def _split3(x):
    hi = x.astype(BF16)
    r = x - hi.astype(F32)
    mid = r.astype(BF16)
    lo = (r - mid.astype(F32)).astype(BF16)
    return hi, mid, lo


def _dot3(a, b):
    ah, al = _hi_lo(a)
    bh, bl = _hi_lo(b)
    return _dot(ah, bh) + _dot(ah, bl) + _dot(al, bh)


def _sigmoid(x):
    return 1.0 / (1.0 + jnp.exp(-x))


def _silu(x):
    return x * _sigmoid(x)


def _softplus(x):
    return jnp.maximum(x, 0.0) + jnp.log1p(jnp.exp(-jnp.abs(x)))


def _rmsnorm(x, g):
    return x * lax.rsqrt(jnp.mean(x * x, axis=-1, keepdims=True) + EPS) * g


def _bf16_bits(v):
    return lax.bitcast_convert_type(v.astype(BF16).astype(F32), U32) & jnp.uint32(0xFFFF0000)


def _store_token_tiles(ref, v):
    n = v.shape[0]
    half = v.shape[1] // 2
    words = _bf16_bits(v[:, half:]) | (_bf16_bits(v[:, :half]) >> 16)
    for j in range(TOK_SUB):
        ref[pl.ds(j, n, stride=TOK_SUB), :] = words[:, j * LANES:(j + 1) * LANES]


def _load_token_tiles(ref):
    n = ref.shape[0] // TOK_SUB
    words = jnp.concatenate([ref[pl.ds(j, n, stride=TOK_SUB), :] for j in range(TOK_SUB)], axis=1)
    lo = lax.bitcast_convert_type(words << 16, F32)
    hi = lax.bitcast_convert_type(words & jnp.uint32(0xFFFF0000), F32)
    return jnp.concatenate([lo, hi], axis=1)


def _token_tile(ref, row):
    return ref.at[pl.ds(pl.multiple_of(row * TOK_SUB, TOK_SUB), TOK_SUB)]


def _ada_kernel(c_ref, w_ref, b_ref, o_ref):
    o_ref[...] = _dot3(_silu(c_ref[...]), w_ref[...]) + b_ref[...]


def _ada(c, w, b):
    bsz = c.shape[0]
    n = w.shape[1]
    tn = 1024
    return pl.pallas_call(
        _ada_kernel,
        out_shape=jax.ShapeDtypeStruct((bsz, n), F32),
        grid=(n // tn,),
        in_specs=[
            pl.BlockSpec((bsz, D_MODEL), lambda j: (0, 0)),
            pl.BlockSpec((D_MODEL, tn), lambda j: (0, j)),
            pl.BlockSpec((1, tn), lambda j: (0, j)),
        ],
        out_specs=pl.BlockSpec((bsz, tn), lambda j: (0, j)),
        compiler_params=pltpu.CompilerParams(dimension_semantics=("arbitrary",)),
        name="ada",
    )(c, w, b.reshape(1, n))


N_MIX_CONSTS = 17


def _promix_kernel(tiles_per_seq, xa_ref, moda_ref, g_ref, w_ref, wdh_ref, wdl_ref, xb_ref, modb_ref, *rest):
    consts = rest[:N_MIX_CONSTS]
    xnew_ref, h2_ref, lt_ref = rest[N_MIX_CONSTS:N_MIX_CONSTS + 3]
    pbuf, dtbuf, zg, dts, xp_ext, xbc_ext, act, state, yssd = rest[N_MIX_CONSTS + 3:]
    s = pl.program_id(0)
    j = lax.rem(jnp.maximum(s - 1, 0), tiles_per_seq)
    tm = TM_MIX

    @pl.when(s == 0)
    def _():
        pbuf[...] = jnp.zeros_like(pbuf)
        dtbuf[...] = jnp.zeros_like(dtbuf)

    @pl.when(j == 0)
    def _():
        xp_ext[0:SEQ_HALO, :] = jnp.zeros((SEQ_HALO, POOL_WIDTH), BF16)
        xbc_ext[0:CONV_HALO, :] = jnp.zeros((CONV_HALO, SSD_CONV_DIM), F32)
        state[...] = jnp.zeros_like(state)

    xp_ext[SEQ_HALO:SEQ_HALO + tm, :] = pbuf[:, 0:POOL_WIDTH]
    xbc_ext[CONV_HALO:CONV_HALO + tm, :] = pbuf[:, COL_XBC:COL_XBC + SSD_CONV_DIM].astype(F32)
    zg[:, 0:SSD_INNER] = pbuf[:, COL_Z:COL_Z + SSD_INNER]
    zg[:, SSD_INNER:SSD_INNER + 2 * D_MODEL] = pbuf[:, COL_GATE:COL_GATE + 2 * D_MODEL]
    dts[...] = dtbuf[...]

    h = _rmsnorm(xa_ref[...], g_ref[...]) * (1.0 + moda_ref[0, 1:2, :]) + moda_ref[0, 0:1, :]
    hb, hl = _hi_lo(h)

    def project(c0):
        def piece():
            pbuf[:, c0:c0 + PROJ_PIECE] = _dot(hb, w_ref[:, c0:c0 + PROJ_PIECE]).astype(BF16)
        return piece

    def project_dt():
        dtbuf[...] = _dot(hb, wdh_ref[...]) + _dot(hb, wdl_ref[...]) + _dot(hl, wdh_ref[...])

    fill = _Filler([project(c0) for c0 in range(0, MAIN_COLS, PROJ_PIECE)] + [project_dt])
    _mix_main(fill, j, zg, dts, xb_ref, modb_ref, *consts,
              xnew_ref, h2_ref, lt_ref, xp_ext, xbc_ext, act, state, yssd)


class _Filler:
    def __init__(self, thunks):
        self.pending = list(thunks)

    def __call__(self, n):
        n = len(self.pending) if n is None else n
        for _ in range(min(n, len(self.pending))):
            self.pending.pop(0)()


def _mix_main(fill, j, zg_ref, dt_ref, x_ref, mod_ref, poolw_ref, pools_ref, convw_ref, convb_ref,
              dtb_ref, alog_ref, dskip_ref, ng_ref, wbp_ref, wbs_ref, wout_ref, n2g_ref,
              wrh_ref, wrl_ref, br_ref, band_ref, expand_ref,
              xnew_ref, h2_ref, lt_ref,
              xp_ext, xbc_ext, act, state, yssd):
    tm = TM_MIX
    q = SSD_CHUNK
    halo = SEQ_HALO


    gw = POOL_GROUP_WIDTH
    pos = j * tm + lax.broadcasted_iota(I32, (tm, gw), 0)
    yp = []
    for gi, w in enumerate(POOL_WINDOWS):
        cols = slice(gi * gw, (gi + 1) * gw)
        acc = jnp.concatenate([_dot(band_ref[gi], xp_ext[r0:r0 + halo + q, cols])
                               for r0 in range(0, tm, q)], axis=0)
        cur = xp_ext[halo:halo + tm, cols].astype(F32)
        d = acc / jnp.minimum(pos + 1, w).astype(F32) - cur
        yp.append(_dot(d.astype(BF16), poolw_ref[gi]) * pools_ref[:, cols])
        fill(1)
    bp = _dot(jnp.concatenate(yp, axis=1).astype(BF16), wbp_ref[...])

    cb = 512
    for c0 in range(0, SSD_CONV_DIM, cb):
        cols = slice(c0, c0 + cb)
        first = CONV_HALO - (SSD_CONV - 1)
        acc = convb_ref[:, cols] + convw_ref[0:1, cols] * xbc_ext[first:first + tm, cols]
        for k in range(1, SSD_CONV):
            acc = acc + convw_ref[k:k + 1, cols] * xbc_ext[first + k:first + k + tm, cols]
        act[:, cols] = _silu(acc)
        fill(2)

    xp_ext[0:halo, :] = xp_ext[tm:tm + halo, :]
    xbc_ext[0:CONV_HALO, :] = xbc_ext[tm:tm + CONV_HALO, :]

    lane = lax.broadcasted_iota(I32, (q, q), 1)
    row = lax.broadcasted_iota(I32, (q, q), 0)
    causal = row >= lane
    triu = jnp.where(row <= lane, 1.0, 0.0).astype(BF16)
    lo_half = lane < SSD_HEAD_DIM
    a_rows = -jnp.exp(alog_ref[...])
    col_b = SSD_INNER
    col_c = SSD_INNER + SSD_GROUPS * SSD_STATE
    gcw = SSD_INNER // SSD_GROUPS
    for c in range(tm // q):
        r0 = c * q
        dtv = _softplus(dt_ref[r0:r0 + q, :] + dtb_ref[...])
        dt_t = dtv.T
        p1, p2, p3 = _split3(dt_t * a_rows)
        acs_t = _dot(p1, triu) + _dot(p2, triu) + _dot(p3, triu)
        acs = acs_t.T
        acs2 = acs * LOG2E
        acs2_t = acs_t * LOG2E
        expand = expand_ref[...]
        dt_x = _dot(jnp.concatenate(_hi_lo(dtv), axis=1), jnp.concatenate([expand, expand], axis=0))
        fx = _dot(jnp.concatenate([jnp.exp(acs).astype(BF16),
                                   jnp.exp(acs[q - 1:q, :] - acs).astype(BF16)], axis=0), expand)
        ea_x = fx[0:q]
        dec_x = fx[q:2 * q]
        xdt = act[r0:r0 + q, 0:SSD_INNER] * dt_x
        xb = xdt.astype(BF16)
        xdb = (xdt * dec_x).astype(BF16)
        s_all = state[...]
        sb = s_all.astype(BF16)
        for g in range(SSD_GROUPS):
            gcols = slice(g * gcw, (g + 1) * gcw)
            bg = act[r0:r0 + q, col_b + g * SSD_STATE:col_b + (g + 1) * SSD_STATE]
            cg = act[r0:r0 + q, col_c + g * SSD_STATE:col_c + (g + 1) * SSD_STATE].astype(BF16)
            cbm = lax.dot_general(cg, bg.astype(BF16),
                                  (((1,), (1,)), ((), ())), preferred_element_type=F32)
            y_off = _dot(cg, sb[:, gcols]) * ea_x[:, gcols]
            ds = _dot(bg.T.astype(BF16), xdb[:, gcols])
            state[:, gcols] = s_all[:, gcols] * ea_x[q - 1:q, gcols] + ds
            y_diag = []
            for pq in range(gcw // LANES):
                pair = g * (gcw // LANES) + pq
                ms = []
                for hh in range(2):
                    h = 2 * pair + hh
                    a_l = jnp.broadcast_to(acs2[:, h:h + 1], (q, q))
                    a_s = jnp.broadcast_to(acs2_t[h:h + 1, :], (q, q))
                    ms.append((jnp.where(causal, jnp.exp2(a_l - a_s), 0.0) * cbm).astype(BF16))
                x_pair = xb[:, pair * LANES:(pair + 1) * LANES]
                zero = jnp.zeros_like(x_pair)
                y_diag.append(_dot(jnp.concatenate(ms, axis=1),
                                   jnp.concatenate([jnp.where(lo_half, x_pair, zero),
                                                    jnp.where(lo_half, zero, x_pair)], axis=0)))
                fill(1 if pq % 2 == 0 else 0)
            yssd[r0:r0 + q, gcols] = jnp.concatenate(y_diag, axis=1) + y_off

    y = yssd[...] + dskip_ref[...] * act[:, 0:SSD_INNER]
    y = y * _silu(zg_ref[:, 0:SSD_INNER].astype(F32))
    fill(1)
    y = _rmsnorm(y, ng_ref[...])
    fill(1)
    bs = _dot(y.astype(BF16), wbs_ref[...])
    g_pool = _sigmoid(zg_ref[:, SSD_INNER:SSD_INNER + D_MODEL].astype(F32))
    g_ssd = _sigmoid(zg_ref[:, SSD_INNER + D_MODEL:SSD_INNER + 2 * D_MODEL].astype(F32))
    fill(1)
    u = g_pool * bp + g_ssd * bs
    x_new = x_ref[...] + mod_ref[0, 2:3, :] * _dot(u.astype(BF16), wout_ref[...])
    xnew_ref[...] = x_new
    fill(None)

    h2 = _rmsnorm(x_new, n2g_ref[...]) * (1.0 + mod_ref[0, 4:5, :]) + mod_ref[0, 3:4, :]
    _store_token_tiles(h2_ref, h2)
    hh, hl = _hi_lo(h2)
    nt = (((1,), (1,)), ((), ()))
    logits_t = (lax.dot_general(wrh_ref[...], hh, nt, preferred_element_type=F32)
                + lax.dot_general(wrl_ref[...], hh, nt, preferred_element_type=F32)
                + lax.dot_general(wrh_ref[...], hl, nt, preferred_element_type=F32))
    lt_ref[...] = logits_t + jnp.concatenate([br_ref[...]] * (tm // LANES), axis=1)


def _promix(x2, mod3, g1, w_main, wd_hi, wd_lo, consts, seq):
    assert len(consts) == N_MIX_CONSTS
    t = x2.shape[0]
    n = t // TM_MIX
    tiles = seq // TM_MIX
    nxt = lambda s: jnp.minimum(s, n - 1)
    cur = lambda s: jnp.maximum(s - 1, 0)

    def cspec(a):
        zeros = (0,) * a.ndim
        return pl.BlockSpec(a.shape, lambda s: zeros, pipeline_mode=pl.Buffered(1))

    return pl.pallas_call(
        functools.partial(_promix_kernel, tiles),
        out_shape=(jax.ShapeDtypeStruct((t, D_MODEL), F32),
                   jax.ShapeDtypeStruct((t * TOK_SUB, LANES), U32),
                   jax.ShapeDtypeStruct((2 * LANES, t), F32)),
        grid=(n + 1,),
        in_specs=[
            pl.BlockSpec((TM_MIX, D_MODEL), lambda s: (nxt(s), 0)),
            pl.BlockSpec((1, 6, D_MODEL), lambda s: (nxt(s) // tiles, 0, 0)),
            cspec(g1), cspec(w_main), cspec(wd_hi), cspec(wd_lo),
            pl.BlockSpec((TM_MIX, D_MODEL), lambda s: (cur(s), 0)),
            pl.BlockSpec((1, 6, D_MODEL), lambda s: (cur(s) // tiles, 0, 0)),
        ] + [cspec(a) for a in consts],
        out_specs=(pl.BlockSpec((TM_MIX, D_MODEL), lambda s: (cur(s), 0)),
                   pl.BlockSpec((TM_MIX * TOK_SUB, LANES), lambda s: (cur(s), 0)),
                   pl.BlockSpec((2 * LANES, TM_MIX), lambda s: (0, cur(s)))),
        scratch_shapes=[
            pltpu.VMEM((TM_MIX, MAIN_COLS), BF16),
            pltpu.VMEM((TM_MIX, LANES), F32),
            pltpu.VMEM((TM_MIX, SSD_INNER + 2 * D_MODEL), BF16),
            pltpu.VMEM((TM_MIX, LANES), F32),
            pltpu.VMEM((SEQ_HALO + TM_MIX, POOL_WIDTH), BF16),
            pltpu.VMEM((CONV_HALO + TM_MIX, SSD_CONV_DIM), F32),
            pltpu.VMEM((TM_MIX, SSD_CONV_DIM), F32),
            pltpu.VMEM((SSD_STATE, SSD_INNER), F32),
            pltpu.VMEM((TM_MIX, SSD_INNER), F32),
        ],
        compiler_params=pltpu.CompilerParams(dimension_semantics=("arbitrary",),
                                             vmem_limit_bytes=VMEM_LIMIT_PROMIX),
        name="promix",
    )(x2, mod3, g1, w_main, wd_hi, wd_lo, x2, mod3, *consts)


def _route_kernel(lt_ref, idx_ref, rf_ref, cnt_ref, carry):
    i = pl.program_id(0)
    tr = TR_ROUTE
    ck = ROUTE_CHUNK
    neg = -jnp.inf

    @pl.when(i == 0)
    def _():
        carry[...] = jnp.zeros_like(carry)

    row8 = lax.broadcasted_iota(I32, (8, tr), 0)
    lg = jnp.where(row8 < MOE_GROUPS, lt_ref[0:8, :], neg)
    mg = jnp.max(lg, axis=0, keepdims=True)
    gsel = jnp.min(jnp.where(lg == mg, row8, 8), axis=0, keepdims=True)
    gwt = 1.0 / jnp.sum(jnp.exp(lg - mg), axis=0, keepdims=True)
    rowe = lax.broadcasted_iota(I32, (N_EXPERTS, tr), 0)
    le = jnp.where(lax.shift_right_logical(rowe, 3) == gsel, lt_ref[LANES:LANES + N_EXPERTS, :], neg)
    m1 = jnp.max(le, axis=0, keepdims=True)
    e0 = jnp.min(jnp.where(le == m1, rowe, N_EXPERTS), axis=0, keepdims=True)
    le2 = jnp.where(rowe == e0, neg, le)
    m2 = jnp.max(le2, axis=0, keepdims=True)
    e1 = jnp.min(jnp.where(le2 == m2, rowe, N_EXPERTS), axis=0, keepdims=True)
    t21 = jnp.exp(m2 - m1)
    w0 = gwt / (1.0 + t21)
    w1 = gwt * t21 / (1.0 + t21)
    oh0 = jnp.where(rowe == e0, 1.0, 0.0)
    oh1 = jnp.where(rowe == e1, 1.0, 0.0)
    oh = oh0 + oh1

    kr = lax.broadcasted_iota(I32, (ck, ck), 0)
    kc = lax.broadcasted_iota(I32, (ck, ck), 1)
    before = jnp.where(kr < kc, 1.0, 0.0).astype(BF16)
    wpad = jnp.zeros((LANES - 2, ck), F32)
    for c0 in range(0, tr, ck):
        cs = slice(c0, c0 + ck)
        pre = _dot(oh[:, cs].astype(BF16), before) + carry[:, 0:1]
        idx_ref[2:3, cs] = jnp.sum(pre * oh0[:, cs], axis=0, keepdims=True).astype(I32)
        idx_ref[3:4, cs] = jnp.sum(pre * oh1[:, cs], axis=0, keepdims=True).astype(I32)
        carry[...] = carry[...] + jnp.sum(oh[:, cs], axis=1, keepdims=True)
        rf_ref[cs, :] = jnp.concatenate([w0[:, cs], w1[:, cs], wpad], axis=0).T
    idx_ref[0:1, :] = e0
    idx_ref[1:2, :] = e1
    idx_ref[4:8, :] = jnp.zeros((4, tr), I32)
    cnt_ref[...] = carry[...].astype(I32)


def _route(logits_t):
    t = logits_t.shape[1]
    return pl.pallas_call(
        _route_kernel,
        out_shape=(jax.ShapeDtypeStruct((8, t), I32),
                   jax.ShapeDtypeStruct((t, LANES), F32),
                   jax.ShapeDtypeStruct((N_EXPERTS, LANES), I32)),
        grid=(t // TR_ROUTE,),
        in_specs=[pl.BlockSpec((2 * LANES, TR_ROUTE), lambda i: (0, i))],
        out_specs=(pl.BlockSpec((8, TR_ROUTE), lambda i: (0, i)),
                   pl.BlockSpec((TR_ROUTE, LANES), lambda i: (i, 0)),
                   pl.BlockSpec((N_EXPERTS, LANES), lambda i: (0, 0))),
        scratch_shapes=[pltpu.VMEM((N_EXPERTS, LANES), F32)],
        compiler_params=pltpu.CompilerParams(dimension_semantics=("arbitrary",)),
        name="moe_route",
    )(logits_t)


def _scatter_kernel(ps_ref, pv_ref, pe_ref, nu_ref, idx_ref, h_ref, xb_ref, zrow, zblk, sem, zsem):
    i = pl.program_id(0)
    ts = TS_SCATTER
    bm = BM_EXPERT
    nb = xb_ref.shape[0] // (bm * TOK_SUB)

    @pl.when(i == 0)
    def _():
        zrow[...] = jnp.zeros_like(zrow)
        zblk[...] = jnp.zeros_like(zblk)

        def row_copy(row):
            return pltpu.make_async_copy(zrow, _token_tile(xb_ref, row), zsem)

        def blk_copy(blk):
            start = pl.multiple_of(blk * (bm * TOK_SUB), bm * TOK_SUB)
            return pltpu.make_async_copy(zblk, xb_ref.at[pl.ds(start, bm * TOK_SUB)], zsem)

        def for_pad_rows(fn):
            def per_expert(e, carry):
                lax.fori_loop(pv_ref[e], pe_ref[e], lambda row, c: fn(row_copy(row), c), 0)
                return carry
            lax.fori_loop(0, N_EXPERTS, per_expert, 0)
            lax.fori_loop(nu_ref[0], nb, lambda blk, c: fn(blk_copy(blk), c), 0)

        def start(cp, c):
            cp.start()
            return c

        def wait(cp, c):
            cp.wait()
            return c

        for_pad_rows(start)
        for_pad_rows(wait)

    def issue(j, carry):
        for u in range(ISSUE_UNROLL):
            r = j * ISSUE_UNROLL + u
            for k in range(2):
                dst = ps_ref[idx_ref[k, r]] + idx_ref[2 + k, r]
                pltpu.make_async_copy(_token_tile(h_ref, r), _token_tile(xb_ref, dst), sem).start(priority=k)
        return carry

    lax.fori_loop(0, ts // ISSUE_UNROLL, issue, 0)
    for _ in range(2):
        pltpu.make_async_copy(h_ref, xb_ref.at[pl.ds(0, ts * TOK_SUB)], sem).wait()


def _scatter(pad_starts, pad_valid_end, pad_ends, n_used, idx, h2t, n_rows):
    t = h2t.shape[0] // TOK_SUB
    return pl.pallas_call(
        _scatter_kernel,
        out_shape=jax.ShapeDtypeStruct((n_rows * TOK_SUB, LANES), U32),
        grid_spec=pltpu.PrefetchScalarGridSpec(
            num_scalar_prefetch=4,
            grid=(t // TS_SCATTER,),
            in_specs=[
                pl.BlockSpec((8, TS_SCATTER), lambda i, *_: (0, i), memory_space=pltpu.SMEM),
                pl.BlockSpec((TS_SCATTER * TOK_SUB, LANES), lambda i, *_: (i, 0)),
            ],
            out_specs=pl.BlockSpec(memory_space=pl.ANY),
            scratch_shapes=[pltpu.VMEM((TOK_SUB, LANES), U32),
                            pltpu.VMEM((BM_EXPERT * TOK_SUB, LANES), U32),
                            pltpu.SemaphoreType.DMA(()),
                            pltpu.SemaphoreType.DMA(())],
        ),
        compiler_params=pltpu.CompilerParams(dimension_semantics=("arbitrary",),
                                             disable_bounds_checks=True),
        name="moe_scatter",
    )(pad_starts, pad_valid_end, pad_ends, n_used, idx, h2t)


def _expert_kernel(be_ref, nu_ref, x_ref, w13_ref, w2_ref, y_ref, w13b, w2b):
    i = pl.program_id(0)

    @pl.when((i == 0) | (be_ref[i] != be_ref[jnp.maximum(i - 1, 0)]))
    def _():
        w13b[...] = w13_ref[0].astype(BF16)
        w2b[...] = w2_ref[0].astype(BF16)

    @pl.when(i < nu_ref[0])
    def _():
        x = _load_token_tiles(x_ref).astype(BF16)
        a = _dot(x, w13b[:, 0:EXPERT_FF])
        g = _dot(x, w13b[:, EXPERT_FF:2 * EXPERT_FF])
        _store_token_tiles(y_ref, _dot((_silu(a) * g).astype(BF16), w2b[...]))

    @pl.when(i >= nu_ref[0])
    def _():
        y_ref[...] = jnp.zeros_like(y_ref)


def _experts(block_expert, n_used, xbuf, w13, w2):
    nb = xbuf.shape[0] // (BM_EXPERT * TOK_SUB)
    used = lambda i, nu: jnp.minimum(i, nu[0] - 1)
    return pl.pallas_call(
        _expert_kernel,
        out_shape=jax.ShapeDtypeStruct(xbuf.shape, U32),
        grid_spec=pltpu.PrefetchScalarGridSpec(
            num_scalar_prefetch=2,
            grid=(nb,),
            in_specs=[
                pl.BlockSpec((BM_EXPERT * TOK_SUB, LANES), lambda i, be, nu: (used(i, nu), 0)),
                pl.BlockSpec((1, D_MODEL, 2 * EXPERT_FF), lambda i, be, nu: (be[i], 0, 0)),
                pl.BlockSpec((1, EXPERT_FF, D_MODEL), lambda i, be, nu: (be[i], 0, 0)),
            ],
            out_specs=pl.BlockSpec((BM_EXPERT * TOK_SUB, LANES), lambda i, be, nu: (i, 0)),
            scratch_shapes=[pltpu.VMEM((D_MODEL, 2 * EXPERT_FF), BF16),
                            pltpu.VMEM((EXPERT_FF, D_MODEL), BF16)],
        ),
        compiler_params=pltpu.CompilerParams(dimension_semantics=("arbitrary",),
                                             vmem_limit_bytes=VMEM_LIMIT),
        name="moe_experts",
    )(block_expert, n_used, xbuf, w13, w2)


def _combine_kernel(ps_ref, idx_ref, idxn_ref, x_ref, rf_ref, mod_ref, fg_ref, yb_ref, out_ref, gbuf, sem):
    i = pl.program_id(0)
    n = pl.num_programs(0)
    tc = TC_COMBINE
    slot = lax.rem(i, 2)

    def start_gather(idx, s):
        def issue(j, carry):
            for u in range(ISSUE_UNROLL):
                r = j * ISSUE_UNROLL + u
                for k in range(2):
                    src = ps_ref[idx[k, r]] + idx[2 + k, r]
                    pltpu.make_async_copy(_token_tile(yb_ref, src), _token_tile(gbuf.at[s, k], r),
                                          sem.at[s]).start(priority=k)
            return carry
        lax.fori_loop(0, tc // ISSUE_UNROLL, issue, 0)

    @pl.when(i == 0)
    def _():
        start_gather(idx_ref, 0)

    @pl.when(i + 1 < n)
    def _():
        start_gather(idxn_ref, 1 - slot)

    for k in range(2):
        pltpu.make_async_copy(yb_ref.at[pl.ds(0, tc * TOK_SUB)], gbuf.at[slot, k], sem.at[slot]).wait()
    moe = (rf_ref[:, 0:1] * _load_token_tiles(gbuf.at[slot, 0])
           + rf_ref[:, 1:2] * _load_token_tiles(gbuf.at[slot, 1]))
    out_ref[...] = _rmsnorm(x_ref[...] + mod_ref[0, 5:6, :] * moe, fg_ref[...])


def _combine(pad_starts, idx, x_new, rf, mod3, fg, ybuf, seq):
    t = x_new.shape[0]
    n = t // TC_COMBINE
    tiles_per_seq = seq // TC_COMBINE
    return pl.pallas_call(
        _combine_kernel,
        out_shape=jax.ShapeDtypeStruct((t, D_MODEL), F32),
        grid_spec=pltpu.PrefetchScalarGridSpec(
            num_scalar_prefetch=1,
            grid=(n,),
            in_specs=[
                pl.BlockSpec((8, TC_COMBINE), lambda i, ps: (0, i), memory_space=pltpu.SMEM),
                pl.BlockSpec((8, TC_COMBINE), lambda i, ps: (0, jnp.minimum(i + 1, n - 1)),
                             memory_space=pltpu.SMEM),
                pl.BlockSpec((TC_COMBINE, D_MODEL), lambda i, ps: (i, 0)),
                pl.BlockSpec((TC_COMBINE, LANES), lambda i, ps: (i, 0)),
                pl.BlockSpec((1, 6, D_MODEL), lambda i, ps: (i // tiles_per_seq, 0, 0)),
                pl.BlockSpec((1, D_MODEL), lambda i, ps: (0, 0)),
                pl.BlockSpec(memory_space=pl.ANY),
            ],
            out_specs=pl.BlockSpec((TC_COMBINE, D_MODEL), lambda i, ps: (i, 0)),
            scratch_shapes=[pltpu.VMEM((2, 2, TC_COMBINE * TOK_SUB, LANES), U32),
                            pltpu.SemaphoreType.DMA((2,))],
        ),
        compiler_params=pltpu.CompilerParams(dimension_semantics=("arbitrary",),
                                             disable_bounds_checks=True),
        name="moe_combine",
    )(pad_starts, idx, idx, x_new, rf, mod3, fg, ybuf)


def _pad_lanes(a, n=LANES):
    return jnp.pad(a, [(0, 0)] * (a.ndim - 1) + [(0, n - a.shape[-1])])


def _selection_matrices():
    q = SSD_CHUNK
    rel = jnp.arange(SEQ_HALO + q)[None, :] - jnp.arange(q)[:, None] - SEQ_HALO
    band = jnp.stack([(rel <= 0) & (rel > -w) for w in POOL_WINDOWS]).astype(BF16)
    expand = (jnp.arange(LANES)[:, None] == jnp.arange(SSD_INNER)[None, :] // SSD_HEAD_DIM).astype(BF16)
    return band, expand


def _layer(x2, c, bsz, seq, ada_w, ada_b, norm1_g, w_in, pool_w, pool_scale, conv_w, conv_b, dt_bias,
           a_log, d_skip, ssd_norm_g, w_branch_pool, w_branch_ssd, w_out, norm2_g, router_group_w,
           router_group_b, router_expert_w, router_expert_b, w13, w2):
    t = bsz * seq
    mod3 = _ada(c, ada_w, ada_b).reshape(bsz, 6, D_MODEL)

    i2 = POOL_WIDTH + SSD_INNER + SSD_CONV_DIM
    i3 = i2 + SSD_HEADS
    w_main = jnp.concatenate([w_in[:, :i2], w_in[:, i3:]], axis=1).astype(BF16)
    wd = _pad_lanes(w_in[:, i2:i3])
    wd_hi = wd.astype(BF16)
    wd_lo = (wd - wd_hi.astype(F32)).astype(BF16)

    wr = jnp.concatenate([_pad_lanes(router_group_w), _pad_lanes(router_expert_w)], axis=1).T
    wr_hi = wr.astype(BF16)
    wr_lo = (wr - wr_hi.astype(F32)).astype(BF16)
    br = jnp.concatenate([_pad_lanes(router_group_b.reshape(1, -1)),
                          _pad_lanes(router_expert_b.reshape(1, -1))], axis=1)
    br = jnp.broadcast_to(br.reshape(2 * LANES, 1), (2 * LANES, LANES))
    consts = [
        pool_w.astype(BF16),
        pool_scale.reshape(1, POOL_WIDTH),
        conv_w,
        conv_b.reshape(1, SSD_CONV_DIM),
        _pad_lanes(dt_bias.reshape(1, SSD_HEADS)),
        jnp.broadcast_to(jnp.pad(a_log, (0, LANES - SSD_HEADS))[:, None], (LANES, LANES)),
        jnp.repeat(d_skip, SSD_HEAD_DIM).reshape(1, SSD_INNER),
        ssd_norm_g.reshape(1, SSD_INNER),
        w_branch_pool.astype(BF16),
        w_branch_ssd.astype(BF16),
        w_out.astype(BF16),
        norm2_g.reshape(1, D_MODEL),
        wr_hi, wr_lo, br,
        *_selection_matrices(),
    ]
    x_new, h2, logits_t = _promix(x2, mod3, norm1_g.reshape(1, D_MODEL), w_main, wd_hi, wd_lo, consts, seq)
    idx, rf, cnt = _route(logits_t)

    counts = cnt[:, 0]
    padded = ((counts + BM_EXPERT - 1) // BM_EXPERT) * BM_EXPERT
    pad_ends = jnp.cumsum(padded)
    pad_starts = (pad_ends - padded).astype(I32)
    n_rows = 2 * t + N_EXPERTS * BM_EXPERT
    nb = n_rows // BM_EXPERT
    block_row0 = jnp.arange(nb, dtype=I32) * BM_EXPERT
    block_expert = jnp.minimum(jnp.sum((pad_ends[None, :] <= block_row0[:, None]).astype(I32), axis=1),
                               N_EXPERTS - 1)
    n_used = (pad_ends[-1:] // BM_EXPERT).astype(I32)

    xbuf = _scatter(pad_starts, pad_starts + counts, pad_ends.astype(I32), n_used, idx, h2, n_rows)
    ybuf = _experts(block_expert, n_used, xbuf, w13, w2)
    return pad_starts, idx, x_new, rf, mod3, ybuf


def kernel(x, c, ada_w, ada_b, norm1_g, w_in, pool_w, pool_scale, conv_w, conv_b, dt_bias, a_log, d_skip, ssd_norm_g, w_branch_pool, w_branch_ssd, w_out, norm2_g, router_group_w, router_group_b, router_expert_w, router_expert_b, w13, w2, final_norm_g):
    bsz, seq, d = x.shape
    depth = ada_w.shape[0]
    assert depth == 1 and d == D_MODEL and seq % TM_MIX == 0
    x2 = x.reshape(bsz * seq, d)
    l = 0
    pad_starts, idx_c, x_new, rf, mod3, ybuf = _layer(
        x2, c, bsz, seq, ada_w[l], ada_b[l], norm1_g[l], w_in[l], pool_w[l], pool_scale[l], conv_w[l],
        conv_b[l], dt_bias[l], a_log[l], d_skip[l], ssd_norm_g[l], w_branch_pool[l], w_branch_ssd[l],
        w_out[l], norm2_g[l], router_group_w[l], router_group_b[l], router_expert_w[l],
        router_expert_b[l], w13[l], w2[l])
    out = _combine(pad_starts, idx_c, x_new, rf, mod3, final_norm_g.reshape(1, d), ybuf, seq)
    return out.reshape(bsz, seq, d)
```

```python
import functools

import jax
import jax.numpy as jnp
from jax import lax
from jax.experimental import pallas as pl
from jax.experimental.pallas import tpu as pltpu

F32 = jnp.float32
BF16 = jnp.bfloat16
I32 = jnp.int32
U32 = jnp.uint32

D_MODEL = 1024
POOL_WIDTH = 1024
POOL_GROUPS = 4
POOL_GROUP_WIDTH = 256
POOL_WINDOWS = (2, 4, 8, 16)
SSD_INNER = 2048
SSD_HEAD_DIM = 64
SSD_HEADS = 32
SSD_GROUPS = 4
SSD_STATE = 128
SSD_CONV = 4
SSD_CHUNK = 128
SSD_CONV_DIM = SSD_INNER + 2 * SSD_GROUPS * SSD_STATE
N_EXPERTS = 32
EXPERTS_PER_GROUP = 8
MOE_GROUPS = 4
EXPERT_FF = 512
EPS = 1e-6

LANES = 128
TOK_SUB = D_MODEL // (2 * LANES)
MAIN_COLS = POOL_WIDTH + SSD_INNER + SSD_CONV_DIM + 2 * D_MODEL
COL_Z = POOL_WIDTH
COL_XBC = COL_Z + SSD_INNER
COL_GATE = COL_XBC + SSD_CONV_DIM

TM_MIX = 256
PROJ_PIECE = 256
SEQ_HALO = SSD_CHUNK
CONV_HALO = 8
CONV_SLABS = SSD_CONV_DIM // LANES
LOG2E = 1.4426950408889634
TR_ROUTE = 2048
ROUTE_CHUNK = 256
TS_SCATTER = 1024
ISSUE_UNROLL = 8
BM_EXPERT = 512
TC_COMBINE = 256
VMEM_LIMIT = 56 * 1024 * 1024
VMEM_LIMIT_PROMIX = 60 * 1024 * 1024


def _dot(a, b):
    return jnp.dot(a, b, preferred_element_type=F32)


def _hi_lo(x):
    hi = x.astype(BF16)
    lo = (x - hi.astype(F32)).astype(BF16)
    return hi, lo


def _split3(x):
    hi = x.astype(BF16)
    r = x - hi.astype(F32)
    mid = r.astype(BF16)
    lo = (r - mid.astype(F32)).astype(BF16)
    return hi, mid, lo


def _dot3(a, b):
    ah, al = _hi_lo(a)
    bh, bl = _hi_lo(b)
    return _dot(ah, bh) + _dot(ah, bl) + _dot(al, bh)


def _sigmoid(x):
    return 1.0 / (1.0 + jnp.exp(-x))


def _silu(x):
    return x * _sigmoid(x)


def _softplus(x):
    return jnp.maximum(x, 0.0) + jnp.log1p(jnp.exp(-jnp.abs(x)))


def _rmsnorm(x, g):
    return x * lax.rsqrt(jnp.mean(x * x, axis=-1, keepdims=True) + EPS) * g


def _bf16_bits(v):
    return lax.bitcast_convert_type(v.astype(BF16).astype(F32), U32) & jnp.uint32(0xFFFF0000)


def _store_token_tiles(ref, v):
    n = v.shape[0]
    half = v.shape[1] // 2
    words = _bf16_bits(v[:, half:]) | (_bf16_bits(v[:, :half]) >> 16)
    for j in range(TOK_SUB):
        ref[pl.ds(j, n, stride=TOK_SUB), :] = words[:, j * LANES:(j + 1) * LANES]


def _load_token_tiles(ref):
    n = ref.shape[0] // TOK_SUB
    words = jnp.concatenate([ref[pl.ds(j, n, stride=TOK_SUB), :] for j in range(TOK_SUB)], axis=1)
    lo = lax.bitcast_convert_type(words << 16, F32)
    hi = lax.bitcast_convert_type(words & jnp.uint32(0xFFFF0000), F32)
    return jnp.concatenate([lo, hi], axis=1)


def _token_tile(ref, row):
    return ref.at[pl.ds(pl.multiple_of(row * TOK_SUB, TOK_SUB), TOK_SUB)]


def _ada_kernel(c_ref, w_ref, b_ref, o_ref):
    o_ref[...] = _dot3(_silu(c_ref[...]), w_ref[...]) + b_ref[...]


def _ada(c, w, b):
    bsz = c.shape[0]
    n = w.shape[1]
    tn = 1024
    return pl.pallas_call(
        _ada_kernel,
        out_shape=jax.ShapeDtypeStruct((bsz, n), F32),
        grid=(n // tn,),
        in_specs=[
            pl.BlockSpec((bsz, D_MODEL), lambda j: (0, 0)),
            pl.BlockSpec((D_MODEL, tn), lambda j: (0, j)),
            pl.BlockSpec((1, tn), lambda j: (0, j)),
        ],
        out_specs=pl.BlockSpec((bsz, tn), lambda j: (0, j)),
        compiler_params=pltpu.CompilerParams(dimension_semantics=("arbitrary",)),
        name="ada",
    )(c, w, b.reshape(1, n))


N_MIX_CONSTS = 17


def _promix_kernel(tiles_per_seq, xa_ref, moda_ref, g_ref, w_ref, wdh_ref, wdl_ref, xb_ref, modb_ref, *rest):
    consts = rest[:N_MIX_CONSTS]
    xnew_ref, h2_ref, lt_ref = rest[N_MIX_CONSTS:N_MIX_CONSTS + 3]
    pbuf, dtbuf, zg, dts, xp_ext, xbc_ext, act, state, yssd = rest[N_MIX_CONSTS + 3:]
    s = pl.program_id(0)
    j = lax.rem(jnp.maximum(s - 1, 0), tiles_per_seq)
    tm = TM_MIX

    @pl.when(s == 0)
    def _():
        pbuf[...] = jnp.zeros_like(pbuf)
        dtbuf[...] = jnp.zeros_like(dtbuf)

    @pl.when(j == 0)
    def _():
        xp_ext[0:SEQ_HALO, :] = jnp.zeros((SEQ_HALO, POOL_WIDTH), BF16)
        xbc_ext[:, 0:CONV_HALO, :] = jnp.zeros((CONV_SLABS, CONV_HALO, LANES), F32)
        state[...] = jnp.zeros_like(state)

    xp_ext[SEQ_HALO:SEQ_HALO + tm, :] = pbuf[:, 0:POOL_WIDTH]
    for sl in range(CONV_SLABS):
        xbc_ext[sl, CONV_HALO:CONV_HALO + tm, :] = (
            pbuf[:, COL_XBC + sl * LANES:COL_XBC + (sl + 1) * LANES].astype(F32))
    zg[:, 0:SSD_INNER] = pbuf[:, COL_Z:COL_Z + SSD_INNER]
    zg[:, SSD_INNER:SSD_INNER + 2 * D_MODEL] = pbuf[:, COL_GATE:COL_GATE + 2 * D_MODEL]
    dts[...] = dtbuf[...]

    h = _rmsnorm(xa_ref[...], g_ref[...]) * (1.0 + moda_ref[0, 1:2, :]) + moda_ref[0, 0:1, :]
    hb, hl = _hi_lo(h)

    def project(c0):
        def piece():
            pbuf[:, c0:c0 + PROJ_PIECE] = _dot(hb, w_ref[:, c0:c0 + PROJ_PIECE]).astype(BF16)
        return piece

    def project_dt():
        dtbuf[...] = _dot(hb, wdh_ref[...]) + _dot(hb, wdl_ref[...]) + _dot(hl, wdh_ref[...])

    fill = _Filler([project(c0) for c0 in range(0, MAIN_COLS, PROJ_PIECE)] + [project_dt])
    _mix_main(fill, j, zg, dts, xb_ref, modb_ref, *consts,
              xnew_ref, h2_ref, lt_ref, xp_ext, xbc_ext, act, state, yssd)


class _Filler:
    def __init__(self, thunks):
        self.pending = list(thunks)

    def __call__(self, n):
        n = len(self.pending) if n is None else n
        for _ in range(min(n, len(self.pending))):
            self.pending.pop(0)()


def _mix_main(fill, j, zg_ref, dt_ref, x_ref, mod_ref, poolw_ref, pools_ref, convw_ref, convb_ref,
              dtb_ref, alog_ref, dskip_ref, ng_ref, wbp_ref, wbs_ref, wout_ref, n2g_ref,
              wrh_ref, wrl_ref, br_ref, band_ref, expand_ref,
              xnew_ref, h2_ref, lt_ref,
              xp_ext, xbc_ext, act, state, yssd):
    tm = TM_MIX
    q = SSD_CHUNK
    halo = SEQ_HALO


    gw = POOL_GROUP_WIDTH
    pos = j * tm + lax.broadcasted_iota(I32, (tm, gw), 0)
    yp = []
    for gi, w in enumerate(POOL_WINDOWS):
        cols = slice(gi * gw, (gi + 1) * gw)
        acc = jnp.concatenate([_dot(band_ref[gi], xp_ext[r0:r0 + halo + q, cols])
                               for r0 in range(0, tm, q)], axis=0)
        cur = xp_ext[halo:halo + tm, cols].astype(F32)
        d = acc / jnp.minimum(pos + 1, w).astype(F32) - cur
        yp.append(_dot(d.astype(BF16), poolw_ref[gi]) * pools_ref[:, cols])
        fill(1)
    bp = _dot(jnp.concatenate(yp, axis=1).astype(BF16), wbp_ref[...])

    first = CONV_HALO - (SSD_CONV - 1)
    for sl in range(CONV_SLABS):
        cols = slice(sl * LANES, (sl + 1) * LANES)
        for par in range(2):
            acc = convb_ref[:, cols] + convw_ref[0:1, cols] * xbc_ext[sl, pl.ds(first + par, tm // 2, stride=2), :]
            for k in range(1, SSD_CONV):
                acc = acc + convw_ref[k:k + 1, cols] * xbc_ext[sl, pl.ds(first + par + k, tm // 2, stride=2), :]
            act[sl, pl.ds(par, tm // 2, stride=2), :] = _silu(acc)
        if sl % 2 == 1:
            fill(1)

    xp_ext[0:halo, :] = xp_ext[tm:tm + halo, :]
    xbc_ext[:, 0:CONV_HALO, :] = xbc_ext[:, tm:tm + CONV_HALO, :]

    lane = lax.broadcasted_iota(I32, (q, q), 1)
    row = lax.broadcasted_iota(I32, (q, q), 0)
    causal = row >= lane
    triu = jnp.where(row <= lane, 1.0, 0.0).astype(BF16)
    lo_half = lane < SSD_HEAD_DIM
    a_rows = -jnp.exp(alog_ref[...])
    col_b = SSD_INNER
    col_c = SSD_INNER + SSD_GROUPS * SSD_STATE
    gcw = SSD_INNER // SSD_GROUPS
    for c in range(tm // q):
        r0 = c * q
        dtv = _softplus(dt_ref[r0:r0 + q, :] + dtb_ref[...])
        dt_t = dtv.T
        p1, p2, p3 = _split3(dt_t * a_rows)
        acs_t = _dot(p1, triu) + _dot(p2, triu) + _dot(p3, triu)
        acs = acs_t.T
        acs2 = acs * LOG2E
        acs2_t = acs_t * LOG2E
        expand = expand_ref[...]
        dt_x = _dot(jnp.concatenate(_hi_lo(dtv), axis=1), jnp.concatenate([expand, expand], axis=0))
        fx = _dot(jnp.concatenate([jnp.exp(acs).astype(BF16),
                                   jnp.exp(acs[q - 1:q, :] - acs).astype(BF16)], axis=0), expand)
        ea_x = fx[0:q]
        dec_x = fx[q:2 * q]
        xdt = jnp.concatenate([act[sl, r0:r0 + q, :] for sl in range(SSD_INNER // LANES)], axis=1) * dt_x
        xb = xdt.astype(BF16)
        xdb = (xdt * dec_x).astype(BF16)
        s_all = state[...]
        sb = s_all.astype(BF16)
        for g in range(SSD_GROUPS):
            gcols = slice(g * gcw, (g + 1) * gcw)
            bg = act[col_b // LANES + g, r0:r0 + q, :]
            cg = act[col_c // LANES + g, r0:r0 + q, :].astype(BF16)
            cbm = lax.dot_general(cg, bg.astype(BF16),
                                  (((1,), (1,)), ((), ())), preferred_element_type=F32)
            y_off = _dot(cg, sb[:, gcols]) * ea_x[:, gcols]
            ds = _dot(bg.T.astype(BF16), xdb[:, gcols])
            state[:, gcols] = s_all[:, gcols] * ea_x[q - 1:q, gcols] + ds
            y_diag = []
            for pq in range(gcw // LANES):
                pair = g * (gcw // LANES) + pq
                ms = []
                for hh in range(2):
                    h = 2 * pair + hh
                    a_l = jnp.broadcast_to(acs2[:, h:h + 1], (q, q))
                    a_s = jnp.broadcast_to(acs2_t[h:h + 1, :], (q, q))
                    ms.append((jnp.where(causal, jnp.exp2(a_l - a_s), 0.0) * cbm).astype(BF16))
                x_pair = xb[:, pair * LANES:(pair + 1) * LANES]
                zero = jnp.zeros_like(x_pair)
                y_diag.append(_dot(jnp.concatenate(ms, axis=1),
                                   jnp.concatenate([jnp.where(lo_half, x_pair, zero),
                                                    jnp.where(lo_half, zero, x_pair)], axis=0)))
                fill(1 if pq % 2 == 0 else 0)
            yssd[r0:r0 + q, gcols] = jnp.concatenate(y_diag, axis=1) + y_off

    y = yssd[...] + dskip_ref[...] * jnp.concatenate([act[sl] for sl in range(SSD_INNER // LANES)], axis=1)
    y = y * _silu(zg_ref[:, 0:SSD_INNER].astype(F32))
    fill(1)
    y = _rmsnorm(y, ng_ref[...])
    fill(1)
    bs = _dot(y.astype(BF16), wbs_ref[...])
    g_pool = _sigmoid(zg_ref[:, SSD_INNER:SSD_INNER + D_MODEL].astype(F32))
    g_ssd = _sigmoid(zg_ref[:, SSD_INNER + D_MODEL:SSD_INNER + 2 * D_MODEL].astype(F32))
    fill(1)
    u = g_pool * bp + g_ssd * bs
    x_new = x_ref[...] + mod_ref[0, 2:3, :] * _dot(u.astype(BF16), wout_ref[...])
    xnew_ref[...] = x_new
    fill(None)

    h2 = _rmsnorm(x_new, n2g_ref[...]) * (1.0 + mod_ref[0, 4:5, :]) + mod_ref[0, 3:4, :]
    _store_token_tiles(h2_ref, h2)
    hh, hl = _hi_lo(h2)
    nt = (((1,), (1,)), ((), ()))
    logits_t = (lax.dot_general(wrh_ref[...], hh, nt, preferred_element_type=F32)
                + lax.dot_general(wrl_ref[...], hh, nt, preferred_element_type=F32)
                + lax.dot_general(wrh_ref[...], hl, nt, preferred_element_type=F32))
    lt_ref[...] = logits_t + jnp.concatenate([br_ref[...]] * (tm // LANES), axis=1)


def _promix(x2, mod3, g1, w_main, wd_hi, wd_lo, consts, seq):
    assert len(consts) == N_MIX_CONSTS
    t = x2.shape[0]
    n = t // TM_MIX
    tiles = seq // TM_MIX
    nxt = lambda s: jnp.minimum(s, n - 1)
    cur = lambda s: jnp.maximum(s - 1, 0)

    def cspec(a):
        zeros = (0,) * a.ndim
        return pl.BlockSpec(a.shape, lambda s: zeros, pipeline_mode=pl.Buffered(1))

    return pl.pallas_call(
        functools.partial(_promix_kernel, tiles),
        out_shape=(jax.ShapeDtypeStruct((t, D_MODEL), F32),
                   jax.ShapeDtypeStruct((t * TOK_SUB, LANES), U32),
                   jax.ShapeDtypeStruct((2 * LANES, t), F32)),
        grid=(n + 1,),
        in_specs=[
            pl.BlockSpec((TM_MIX, D_MODEL), lambda s: (nxt(s), 0)),
            pl.BlockSpec((1, 6, D_MODEL), lambda s: (nxt(s) // tiles, 0, 0)),
            cspec(g1), cspec(w_main), cspec(wd_hi), cspec(wd_lo),
            pl.BlockSpec((TM_MIX, D_MODEL), lambda s: (cur(s), 0)),
            pl.BlockSpec((1, 6, D_MODEL), lambda s: (cur(s) // tiles, 0, 0)),
        ] + [cspec(a) for a in consts],
        out_specs=(pl.BlockSpec((TM_MIX, D_MODEL), lambda s: (cur(s), 0)),
                   pl.BlockSpec((TM_MIX * TOK_SUB, LANES), lambda s: (cur(s), 0)),
                   pl.BlockSpec((2 * LANES, TM_MIX), lambda s: (0, cur(s)))),
        scratch_shapes=[
            pltpu.VMEM((TM_MIX, MAIN_COLS), BF16),
            pltpu.VMEM((TM_MIX, LANES), F32),
            pltpu.VMEM((TM_MIX, SSD_INNER + 2 * D_MODEL), BF16),
            pltpu.VMEM((TM_MIX, LANES), F32),
            pltpu.VMEM((SEQ_HALO + TM_MIX, POOL_WIDTH), BF16),
            pltpu.VMEM((CONV_SLABS, CONV_HALO + TM_MIX, LANES), F32),
            pltpu.VMEM((CONV_SLABS, TM_MIX, LANES), F32),
            pltpu.VMEM((SSD_STATE, SSD_INNER), F32),
            pltpu.VMEM((TM_MIX, SSD_INNER), F32),
        ],
        compiler_params=pltpu.CompilerParams(dimension_semantics=("arbitrary",),
                                             vmem_limit_bytes=VMEM_LIMIT_PROMIX),
        name="promix",
    )(x2, mod3, g1, w_main, wd_hi, wd_lo, x2, mod3, *consts)


def _route_kernel(lt_ref, idx_ref, rf_ref, cnt_ref, carry):
    i = pl.program_id(0)
    tr = TR_ROUTE
    ck = ROUTE_CHUNK
    neg = -jnp.inf

    @pl.when(i == 0)
    def _():
        carry[...] = jnp.zeros_like(carry)

    row8 = lax.broadcasted_iota(I32, (8, tr), 0)
    lg = jnp.where(row8 < MOE_GROUPS, lt_ref[0:8, :], neg)
    mg = jnp.max(lg, axis=0, keepdims=True)
    gsel = jnp.min(jnp.where(lg == mg, row8, 8), axis=0, keepdims=True)
    gwt = 1.0 / jnp.sum(jnp.exp(lg - mg), axis=0, keepdims=True)
    rowe = lax.broadcasted_iota(I32, (N_EXPERTS, tr), 0)
    le = jnp.where(lax.shift_right_logical(rowe, 3) == gsel, lt_ref[LANES:LANES + N_EXPERTS, :], neg)
    m1 = jnp.max(le, axis=0, keepdims=True)
    e0 = jnp.min(jnp.where(le == m1, rowe, N_EXPERTS), axis=0, keepdims=True)
    le2 = jnp.where(rowe == e0, neg, le)
    m2 = jnp.max(le2, axis=0, keepdims=True)
    e1 = jnp.min(jnp.where(le2 == m2, rowe, N_EXPERTS), axis=0, keepdims=True)
    t21 = jnp.exp(m2 - m1)
    w0 = gwt / (1.0 + t21)
    w1 = gwt * t21 / (1.0 + t21)
    oh0 = jnp.where(rowe == e0, 1.0, 0.0)
    oh1 = jnp.where(rowe == e1, 1.0, 0.0)
    oh = oh0 + oh1

    kr = lax.broadcasted_iota(I32, (ck, ck), 0)
    kc = lax.broadcasted_iota(I32, (ck, ck), 1)
    before = jnp.where(kr < kc, 1.0, 0.0).astype(BF16)
    wpad = jnp.zeros((LANES - 2, ck), F32)
    for c0 in range(0, tr, ck):
        cs = slice(c0, c0 + ck)
        pre = _dot(oh[:, cs].astype(BF16), before) + carry[:, 0:1]
        idx_ref[2:3, cs] = jnp.sum(pre * oh0[:, cs], axis=0, keepdims=True).astype(I32)
        idx_ref[3:4, cs] = jnp.sum(pre * oh1[:, cs], axis=0, keepdims=True).astype(I32)
        carry[...] = carry[...] + jnp.sum(oh[:, cs], axis=1, keepdims=True)
        rf_ref[cs, :] = jnp.concatenate([w0[:, cs], w1[:, cs], wpad], axis=0).T
    idx_ref[0:1, :] = e0
    idx_ref[1:2, :] = e1
    idx_ref[4:8, :] = jnp.zeros((4, tr), I32)
    cnt_ref[...] = carry[...].astype(I32)


def _route(logits_t):
    t = logits_t.shape[1]
    return pl.pallas_call(
        _route_kernel,
        out_shape=(jax.ShapeDtypeStruct((8, t), I32),
                   jax.ShapeDtypeStruct((t, LANES), F32),
                   jax.ShapeDtypeStruct((N_EXPERTS, LANES), I32)),
        grid=(t // TR_ROUTE,),
        in_specs=[pl.BlockSpec((2 * LANES, TR_ROUTE), lambda i: (0, i))],
        out_specs=(pl.BlockSpec((8, TR_ROUTE), lambda i: (0, i)),
                   pl.BlockSpec((TR_ROUTE, LANES), lambda i: (i, 0)),
                   pl.BlockSpec((N_EXPERTS, LANES), lambda i: (0, 0))),
        scratch_shapes=[pltpu.VMEM((N_EXPERTS, LANES), F32)],
        compiler_params=pltpu.CompilerParams(dimension_semantics=("arbitrary",)),
        name="moe_route",
    )(logits_t)


def _scatter_kernel(ps_ref, pv_ref, pe_ref, nu_ref, idx_ref, h_ref, xb_ref, zrow, zblk, sem, zsem):
    i = pl.program_id(0)
    ts = TS_SCATTER
    bm = BM_EXPERT
    nb = xb_ref.shape[0] // (bm * TOK_SUB)

    @pl.when(i == 0)
    def _():
        zrow[...] = jnp.zeros_like(zrow)
        zblk[...] = jnp.zeros_like(zblk)

        def row_copy(row):
            return pltpu.make_async_copy(zrow, _token_tile(xb_ref, row), zsem)

        def blk_copy(blk):
            start = pl.multiple_of(blk * (bm * TOK_SUB), bm * TOK_SUB)
            return pltpu.make_async_copy(zblk, xb_ref.at[pl.ds(start, bm * TOK_SUB)], zsem)

        def for_pad_rows(fn):
            def per_expert(e, carry):
                lax.fori_loop(pv_ref[e], pe_ref[e], lambda row, c: fn(row_copy(row), c), 0)
                return carry
            lax.fori_loop(0, N_EXPERTS, per_expert, 0)
            lax.fori_loop(nu_ref[0], nb, lambda blk, c: fn(blk_copy(blk), c), 0)

        def start(cp, c):
            cp.start()
            return c

        def wait(cp, c):
            cp.wait()
            return c

        for_pad_rows(start)
        for_pad_rows(wait)

    def issue(j, carry):
        for u in range(ISSUE_UNROLL):
            r = j * ISSUE_UNROLL + u
            for k in range(2):
                dst = ps_ref[idx_ref[k, r]] + idx_ref[2 + k, r]
                pltpu.make_async_copy(_token_tile(h_ref, r), _token_tile(xb_ref, dst), sem).start(priority=k)
        return carry

    lax.fori_loop(0, ts // ISSUE_UNROLL, issue, 0)
    for _ in range(2):
        pltpu.make_async_copy(h_ref, xb_ref.at[pl.ds(0, ts * TOK_SUB)], sem).wait()


def _scatter(pad_starts, pad_valid_end, pad_ends, n_used, idx, h2t, n_rows):
    t = h2t.shape[0] // TOK_SUB
    return pl.pallas_call(
        _scatter_kernel,
        out_shape=jax.ShapeDtypeStruct((n_rows * TOK_SUB, LANES), U32),
        grid_spec=pltpu.PrefetchScalarGridSpec(
            num_scalar_prefetch=4,
            grid=(t // TS_SCATTER,),
            in_specs=[
                pl.BlockSpec((8, TS_SCATTER), lambda i, *_: (0, i), memory_space=pltpu.SMEM),
                pl.BlockSpec((TS_SCATTER * TOK_SUB, LANES), lambda i, *_: (i, 0)),
            ],
            out_specs=pl.BlockSpec(memory_space=pl.ANY),
            scratch_shapes=[pltpu.VMEM((TOK_SUB, LANES), U32),
                            pltpu.VMEM((BM_EXPERT * TOK_SUB, LANES), U32),
                            pltpu.SemaphoreType.DMA(()),
                            pltpu.SemaphoreType.DMA(())],
        ),
        compiler_params=pltpu.CompilerParams(dimension_semantics=("arbitrary",),
                                             disable_bounds_checks=True),
        name="moe_scatter",
    )(pad_starts, pad_valid_end, pad_ends, n_used, idx, h2t)


def _expert_kernel(be_ref, nu_ref, x_ref, w13_ref, w2_ref, y_ref, w13b, w2b):
    i = pl.program_id(0)

    @pl.when((i == 0) | (be_ref[i] != be_ref[jnp.maximum(i - 1, 0)]))
    def _():
        w13b[...] = w13_ref[0].astype(BF16)
        w2b[...] = w2_ref[0].astype(BF16)

    @pl.when(i < nu_ref[0])
    def _():
        x = _load_token_tiles(x_ref).astype(BF16)
        a = _dot(x, w13b[:, 0:EXPERT_FF])
        g = _dot(x, w13b[:, EXPERT_FF:2 * EXPERT_FF])
        _store_token_tiles(y_ref, _dot((_silu(a) * g).astype(BF16), w2b[...]))

    @pl.when(i >= nu_ref[0])
    def _():
        y_ref[...] = jnp.zeros_like(y_ref)


def _experts(block_expert, n_used, xbuf, w13, w2):
    nb = xbuf.shape[0] // (BM_EXPERT * TOK_SUB)
    used = lambda i, nu: jnp.minimum(i, nu[0] - 1)
    return pl.pallas_call(
        _expert_kernel,
        out_shape=jax.ShapeDtypeStruct(xbuf.shape, U32),
        grid_spec=pltpu.PrefetchScalarGridSpec(
            num_scalar_prefetch=2,
            grid=(nb,),
            in_specs=[
                pl.BlockSpec((BM_EXPERT * TOK_SUB, LANES), lambda i, be, nu: (used(i, nu), 0)),
                pl.BlockSpec((1, D_MODEL, 2 * EXPERT_FF), lambda i, be, nu: (be[i], 0, 0)),
                pl.BlockSpec((1, EXPERT_FF, D_MODEL), lambda i, be, nu: (be[i], 0, 0)),
            ],
            out_specs=pl.BlockSpec((BM_EXPERT * TOK_SUB, LANES), lambda i, be, nu: (i, 0)),
            scratch_shapes=[pltpu.VMEM((D_MODEL, 2 * EXPERT_FF), BF16),
                            pltpu.VMEM((EXPERT_FF, D_MODEL), BF16)],
        ),
        compiler_params=pltpu.CompilerParams(dimension_semantics=("arbitrary",),
                                             vmem_limit_bytes=VMEM_LIMIT),
        name="moe_experts",
    )(block_expert, n_used, xbuf, w13, w2)


def _combine_kernel(ps_ref, idx_ref, idxn_ref, x_ref, rf_ref, mod_ref, fg_ref, yb_ref, out_ref, gbuf, sem):
    i = pl.program_id(0)
    n = pl.num_programs(0)
    tc = TC_COMBINE
    slot = lax.rem(i, 2)

    def start_gather(idx, s):
        def issue(j, carry):
            for u in range(ISSUE_UNROLL):
                r = j * ISSUE_UNROLL + u
                for k in range(2):
                    src = ps_ref[idx[k, r]] + idx[2 + k, r]
                    pltpu.make_async_copy(_token_tile(yb_ref, src), _token_tile(gbuf.at[s, k], r),
                                          sem.at[s]).start(priority=k)
            return carry
        lax.fori_loop(0, tc // ISSUE_UNROLL, issue, 0)

    @pl.when(i == 0)
    def _():
        start_gather(idx_ref, 0)

    @pl.when(i + 1 < n)
    def _():
        start_gather(idxn_ref, 1 - slot)

    for k in range(2):
        pltpu.make_async_copy(yb_ref.at[pl.ds(0, tc * TOK_SUB)], gbuf.at[slot, k], sem.at[slot]).wait()
    moe = (rf_ref[:, 0:1] * _load_token_tiles(gbuf.at[slot, 0])
           + rf_ref[:, 1:2] * _load_token_tiles(gbuf.at[slot, 1]))
    out_ref[...] = _rmsnorm(x_ref[...] + mod_ref[0, 5:6, :] * moe, fg_ref[...])


def _combine(pad_starts, idx, x_new, rf, mod3, fg, ybuf, seq):
    t = x_new.shape[0]
    n = t // TC_COMBINE
    tiles_per_seq = seq // TC_COMBINE
    return pl.pallas_call(
        _combine_kernel,
        out_shape=jax.ShapeDtypeStruct((t, D_MODEL), F32),
        grid_spec=pltpu.PrefetchScalarGridSpec(
            num_scalar_prefetch=1,
            grid=(n,),
            in_specs=[
                pl.BlockSpec((8, TC_COMBINE), lambda i, ps: (0, i), memory_space=pltpu.SMEM),
                pl.BlockSpec((8, TC_COMBINE), lambda i, ps: (0, jnp.minimum(i + 1, n - 1)),
                             memory_space=pltpu.SMEM),
                pl.BlockSpec((TC_COMBINE, D_MODEL), lambda i, ps: (i, 0)),
                pl.BlockSpec((TC_COMBINE, LANES), lambda i, ps: (i, 0)),
                pl.BlockSpec((1, 6, D_MODEL), lambda i, ps: (i // tiles_per_seq, 0, 0)),
                pl.BlockSpec((1, D_MODEL), lambda i, ps: (0, 0)),
                pl.BlockSpec(memory_space=pl.ANY),
            ],
            out_specs=pl.BlockSpec((TC_COMBINE, D_MODEL), lambda i, ps: (i, 0)),
            scratch_shapes=[pltpu.VMEM((2, 2, TC_COMBINE * TOK_SUB, LANES), U32),
                            pltpu.SemaphoreType.DMA((2,))],
        ),
        compiler_params=pltpu.CompilerParams(dimension_semantics=("arbitrary",),
                                             disable_bounds_checks=True),
        name="moe_combine",
    )(pad_starts, idx, idx, x_new, rf, mod3, fg, ybuf)


def _pad_lanes(a, n=LANES):
    return jnp.pad(a, [(0, 0)] * (a.ndim - 1) + [(0, n - a.shape[-1])])


def _selection_matrices():
    q = SSD_CHUNK
    rel = jnp.arange(SEQ_HALO + q)[None, :] - jnp.arange(q)[:, None] - SEQ_HALO
    band = jnp.stack([(rel <= 0) & (rel > -w) for w in POOL_WINDOWS]).astype(BF16)
    expand = (jnp.arange(LANES)[:, None] == jnp.arange(SSD_INNER)[None, :] // SSD_HEAD_DIM).astype(BF16)
    return band, expand


def _layer(x2, c, bsz, seq, ada_w, ada_b, norm1_g, w_in, pool_w, pool_scale, conv_w, conv_b, dt_bias,
           a_log, d_skip, ssd_norm_g, w_branch_pool, w_branch_ssd, w_out, norm2_g, router_group_w,
           router_group_b, router_expert_w, router_expert_b, w13, w2):
    t = bsz * seq
    mod3 = _ada(c, ada_w, ada_b).reshape(bsz, 6, D_MODEL)

    i2 = POOL_WIDTH + SSD_INNER + SSD_CONV_DIM
    i3 = i2 + SSD_HEADS
    w_main = jnp.concatenate([w_in[:, :i2], w_in[:, i3:]], axis=1).astype(BF16)
    wd = _pad_lanes(w_in[:, i2:i3])
    wd_hi = wd.astype(BF16)
    wd_lo = (wd - wd_hi.astype(F32)).astype(BF16)

    wr = jnp.concatenate([_pad_lanes(router_group_w), _pad_lanes(router_expert_w)], axis=1).T
    wr_hi = wr.astype(BF16)
    wr_lo = (wr - wr_hi.astype(F32)).astype(BF16)
    br = jnp.concatenate([_pad_lanes(router_group_b.reshape(1, -1)),
                          _pad_lanes(router_expert_b.reshape(1, -1))], axis=1)
    br = jnp.broadcast_to(br.reshape(2 * LANES, 1), (2 * LANES, LANES))
    consts = [
        pool_w.astype(BF16),
        pool_scale.reshape(1, POOL_WIDTH),
        conv_w,
        conv_b.reshape(1, SSD_CONV_DIM),
        _pad_lanes(dt_bias.reshape(1, SSD_HEADS)),
        jnp.broadcast_to(jnp.pad(a_log, (0, LANES - SSD_HEADS))[:, None], (LANES, LANES)),
        jnp.repeat(d_skip, SSD_HEAD_DIM).reshape(1, SSD_INNER),
        ssd_norm_g.reshape(1, SSD_INNER),
        w_branch_pool.astype(BF16),
        w_branch_ssd.astype(BF16),
        w_out.astype(BF16),
        norm2_g.reshape(1, D_MODEL),
        wr_hi, wr_lo, br,
        *_selection_matrices(),
    ]
    x_new, h2, logits_t = _promix(x2, mod3, norm1_g.reshape(1, D_MODEL), w_main, wd_hi, wd_lo, consts, seq)
    idx, rf, cnt = _route(logits_t)

    counts = cnt[:, 0]
    padded = ((counts + BM_EXPERT - 1) // BM_EXPERT) * BM_EXPERT
    pad_ends = jnp.cumsum(padded)
    pad_starts = (pad_ends - padded).astype(I32)
    n_rows = 2 * t + N_EXPERTS * BM_EXPERT
    nb = n_rows // BM_EXPERT
    block_row0 = jnp.arange(nb, dtype=I32) * BM_EXPERT
    block_expert = jnp.minimum(jnp.sum((pad_ends[None, :] <= block_row0[:, None]).astype(I32), axis=1),
                               N_EXPERTS - 1)
    n_used = (pad_ends[-1:] // BM_EXPERT).astype(I32)

    xbuf = _scatter(pad_starts, pad_starts + counts, pad_ends.astype(I32), n_used, idx, h2, n_rows)
    ybuf = _experts(block_expert, n_used, xbuf, w13, w2)
    return pad_starts, idx, x_new, rf, mod3, ybuf


def kernel(x, c, ada_w, ada_b, norm1_g, w_in, pool_w, pool_scale, conv_w, conv_b, dt_bias, a_log, d_skip, ssd_norm_g, w_branch_pool, w_branch_ssd, w_out, norm2_g, router_group_w, router_group_b, router_expert_w, router_expert_b, w13, w2, final_norm_g):
    bsz, seq, d = x.shape
    depth = ada_w.shape[0]
    assert depth == 1 and d == D_MODEL and seq % TM_MIX == 0
    x2 = x.reshape(bsz * seq, d)
    l = 0
    pad_starts, idx_c, x_new, rf, mod3, ybuf = _layer(
        x2, c, bsz, seq, ada_w[l], ada_b[l], norm1_g[l], w_in[l], pool_w[l], pool_scale[l], conv_w[l],
        conv_b[l], dt_bias[l], a_log[l], d_skip[l], ssd_norm_g[l], w_branch_pool[l], w_branch_ssd[l],
        w_out[l], norm2_g[l], router_group_w[l], router_group_b[l], router_expert_w[l],
        router_expert_b[l], w13[l], w2[l])
    out = _combine(pad_starts, idx_c, x_new, rf, mod3, final_norm_g.reshape(1, d), ybuf, seq)
    return out.reshape(bsz, seq, d)
```

```python
import functools

import jax
import jax.numpy as jnp
from jax import lax
from jax.experimental import pallas as pl
from jax.experimental.pallas import tpu as pltpu

F32 = jnp.float32
BF16 = jnp.bfloat16
I32 = jnp.int32
U32 = jnp.uint32

D_MODEL = 1024
POOL_WIDTH = 1024
POOL_GROUPS = 4
POOL_GROUP_WIDTH = 256
POOL_WINDOWS = (2, 4, 8, 16)
SSD_INNER = 2048
SSD_HEAD_DIM = 64
SSD_HEADS = 32
SSD_GROUPS = 4
SSD_STATE = 128
SSD_CONV = 4
SSD_CHUNK = 128
SSD_CONV_DIM = SSD_INNER + 2 * SSD_GROUPS * SSD_STATE
N_EXPERTS = 32
EXPERTS_PER_GROUP = 8
MOE_GROUPS = 4
EXPERT_FF = 512
EPS = 1e-6

LANES = 128
TOK_SUB = D_MODEL // (2 * LANES)
MAIN_COLS = POOL_WIDTH + SSD_INNER + SSD_CONV_DIM + 2 * D_MODEL
COL_Z = POOL_WIDTH
COL_XBC = COL_Z + SSD_INNER
COL_GATE = COL_XBC + SSD_CONV_DIM

TM_MIX = 256
PROJ_PIECE = 256
SEQ_HALO = SSD_CHUNK
CONV_HALO = 8
CONV_SLABS = SSD_CONV_DIM // LANES
LOG2E = 1.4426950408889634
TR_ROUTE = 2048
ROUTE_CHUNK = 256
TS_SCATTER = 1024
ISSUE_UNROLL = 8
BM_EXPERT = 512
TC_COMBINE = 256
VMEM_LIMIT = 56 * 1024 * 1024
VMEM_LIMIT_PROMIX = 60 * 1024 * 1024


def _dot(a, b):
    return jnp.dot(a, b, preferred_element_type=F32)


def _hi_lo(x):
    hi = x.astype(BF16)
    lo = (x - hi.astype(F32)).astype(BF16)
    return hi, lo


def _split3(x):
    hi = x.astype(BF16)
    r = x - hi.astype(F32)
    mid = r.astype(BF16)
    lo = (r - mid.astype(F32)).astype(BF16)
    return hi, mid, lo


def _dot3(a, b):
    ah, al = _hi_lo(a)
    bh, bl = _hi_lo(b)
    return _dot(ah, bh) + _dot(ah, bl) + _dot(al, bh)


def _sigmoid(x):
    return 1.0 / (1.0 + jnp.exp(-x))


def _silu(x):
    return x * _sigmoid(x)


def _softplus(x):
    return jnp.maximum(x, 0.0) + jnp.log1p(jnp.exp(-jnp.abs(x)))


def _rmsnorm(x, g):
    return x * lax.rsqrt(jnp.mean(x * x, axis=-1, keepdims=True) + EPS) * g


def _bf16_bits(v):
    return lax.bitcast_convert_type(v.astype(BF16).astype(F32), U32) & jnp.uint32(0xFFFF0000)


def _store_token_tiles(ref, v):
    n = v.shape[0]
    half = v.shape[1] // 2
    words = _bf16_bits(v[:, half:]) | (_bf16_bits(v[:, :half]) >> 16)
    for j in range(TOK_SUB):
        ref[pl.ds(j, n, stride=TOK_SUB), :] = words[:, j * LANES:(j + 1) * LANES]


def _load_token_tiles(ref):
    n = ref.shape[0] // TOK_SUB
    words = jnp.concatenate([ref[pl.ds(j, n, stride=TOK_SUB), :] for j in range(TOK_SUB)], axis=1)
    lo = lax.bitcast_convert_type(words << 16, F32)
    hi = lax.bitcast_convert_type(words & jnp.uint32(0xFFFF0000), F32)
    return jnp.concatenate([lo, hi], axis=1)


def _token_tile(ref, row):
    return ref.at[pl.ds(pl.multiple_of(row * TOK_SUB, TOK_SUB), TOK_SUB)]


def _ada_kernel(c_ref, w_ref, b_ref, o_ref):
    o_ref[...] = _dot3(_silu(c_ref[...]), w_ref[...]) + b_ref[...]


def _ada(c, w, b):
    bsz = c.shape[0]
    n = w.shape[1]
    tn = 1024
    return pl.pallas_call(
        _ada_kernel,
        out_shape=jax.ShapeDtypeStruct((bsz, n), F32),
        grid=(n // tn,),
        in_specs=[
            pl.BlockSpec((bsz, D_MODEL), lambda j: (0, 0)),
            pl.BlockSpec((D_MODEL, tn), lambda j: (0, j)),
            pl.BlockSpec((1, tn), lambda j: (0, j)),
        ],
        out_specs=pl.BlockSpec((bsz, tn), lambda j: (0, j)),
        compiler_params=pltpu.CompilerParams(dimension_semantics=("arbitrary",)),
        name="ada",
    )(c, w, b.reshape(1, n))


N_MIX_CONSTS = 17


def _promix_kernel(tiles_per_seq, xa_ref, moda_ref, g_ref, w_ref, wdh_ref, wdl_ref, xb_ref, modb_ref, *rest):
    consts = rest[:N_MIX_CONSTS]
    xnew_ref, h2_ref, lt_ref = rest[N_MIX_CONSTS:N_MIX_CONSTS + 3]
    pbuf, dtbuf, zg, dts, xp_ext, xbc_ext, act, state, yssd = rest[N_MIX_CONSTS + 3:]
    s = pl.program_id(0)
    j = lax.rem(jnp.maximum(s - 1, 0), tiles_per_seq)
    tm = TM_MIX

    @pl.when(s == 0)
    def _():
        pbuf[...] = jnp.zeros_like(pbuf)
        dtbuf[...] = jnp.zeros_like(dtbuf)

    @pl.when(j == 0)
    def _():
        xp_ext[0:SEQ_HALO, :] = jnp.zeros((SEQ_HALO, POOL_WIDTH), BF16)
        xbc_ext[:, 0:CONV_HALO, :] = jnp.zeros((CONV_SLABS, CONV_HALO, LANES), F32)
        state[...] = jnp.zeros_like(state)

    xp_ext[SEQ_HALO:SEQ_HALO + tm, :] = pbuf[:, 0:POOL_WIDTH]
    for sl in range(CONV_SLABS):
        xbc_ext[sl, CONV_HALO:CONV_HALO + tm, :] = (
            pbuf[:, COL_XBC + sl * LANES:COL_XBC + (sl + 1) * LANES].astype(F32))
    zg[:, 0:SSD_INNER] = pbuf[:, COL_Z:COL_Z + SSD_INNER]
    zg[:, SSD_INNER:SSD_INNER + 2 * D_MODEL] = pbuf[:, COL_GATE:COL_GATE + 2 * D_MODEL]
    dts[...] = dtbuf[...]

    h = _rmsnorm(xa_ref[...], g_ref[...]) * (1.0 + moda_ref[0, 1:2, :]) + moda_ref[0, 0:1, :]
    hb, hl = _hi_lo(h)

    def project(c0):
        def piece():
            pbuf[:, c0:c0 + PROJ_PIECE] = _dot(hb, w_ref[:, c0:c0 + PROJ_PIECE]).astype(BF16)
        return piece

    def project_dt():
        dtbuf[...] = _dot(hb, wdh_ref[...]) + _dot(hb, wdl_ref[...]) + _dot(hl, wdh_ref[...])

    fill = _Filler([project(c0) for c0 in range(0, MAIN_COLS, PROJ_PIECE)] + [project_dt])
    _mix_main(fill, j, zg, dts, xb_ref, modb_ref, *consts,
              xnew_ref, h2_ref, lt_ref, xp_ext, xbc_ext, act, state, yssd)


class _Filler:
    def __init__(self, thunks):
        self.pending = list(thunks)

    def __call__(self, n):
        n = len(self.pending) if n is None else n
        for _ in range(min(n, len(self.pending))):
            self.pending.pop(0)()


def _mix_main(fill, j, zg_ref, dt_ref, x_ref, mod_ref, poolw_ref, pools_ref, convw_ref, convb_ref,
              dtb_ref, alog_ref, dskip_ref, ng_ref, wbp_ref, wbs_ref, wout_ref, n2g_ref,
              wrh_ref, wrl_ref, br_ref, band_ref, expand_ref,
              xnew_ref, h2_ref, lt_ref,
              xp_ext, xbc_ext, act, state, yssd):
    tm = TM_MIX
    q = SSD_CHUNK
    halo = SEQ_HALO


    gw = POOL_GROUP_WIDTH
    pos = j * tm + lax.broadcasted_iota(I32, (tm, gw), 0)
    yp = []
    for gi, w in enumerate(POOL_WINDOWS):
        cols = slice(gi * gw, (gi + 1) * gw)
        acc = jnp.concatenate([_dot(band_ref[gi], xp_ext[r0:r0 + halo + q, cols])
                               for r0 in range(0, tm, q)], axis=0)
        cur = xp_ext[halo:halo + tm, cols].astype(F32)
        d = acc / jnp.minimum(pos + 1, w).astype(F32) - cur
        yp.append(_dot(d.astype(BF16), poolw_ref[gi]) * pools_ref[:, cols])
        fill(1)
    bp = _dot(jnp.concatenate(yp, axis=1).astype(BF16), wbp_ref[...])

    first = CONV_HALO - (SSD_CONV - 1)
    for sl in range(CONV_SLABS):
        cols = slice(sl * LANES, (sl + 1) * LANES)
        for par in range(2):
            acc = convb_ref[:, cols] + convw_ref[0:1, cols] * xbc_ext[sl, pl.ds(first + par, tm // 2, stride=2), :]
            for k in range(1, SSD_CONV):
                acc = acc + convw_ref[k:k + 1, cols] * xbc_ext[sl, pl.ds(first + par + k, tm // 2, stride=2), :]
            act[sl, pl.ds(par, tm // 2, stride=2), :] = _silu(acc)
        if sl % 2 == 1:
            fill(1)

    xp_ext[0:halo, :] = xp_ext[tm:tm + halo, :]
    xbc_ext[:, 0:CONV_HALO, :] = xbc_ext[:, tm:tm + CONV_HALO, :]

    lane = lax.broadcasted_iota(I32, (q, q), 1)
    row = lax.broadcasted_iota(I32, (q, q), 0)
    causal = row >= lane
    triu = jnp.where(row <= lane, 1.0, 0.0).astype(BF16)
    lo_half = lane < SSD_HEAD_DIM
    a_rows = -jnp.exp(alog_ref[...])
    col_b = SSD_INNER
    col_c = SSD_INNER + SSD_GROUPS * SSD_STATE
    gcw = SSD_INNER // SSD_GROUPS
    for c in range(tm // q):
        r0 = c * q
        dtv = _softplus(dt_ref[r0:r0 + q, :] + dtb_ref[...])
        dt_t = dtv.T
        p1, p2, p3 = _split3(dt_t * a_rows)
        acs_t = _dot(p1, triu) + _dot(p2, triu) + _dot(p3, triu)
        acs = acs_t.T
        acs2 = acs * LOG2E
        acs2_t = acs_t * LOG2E
        expand = expand_ref[...]
        dt_x = _dot(jnp.concatenate(_hi_lo(dtv), axis=1), jnp.concatenate([expand, expand], axis=0))
        fx = _dot(jnp.concatenate([jnp.exp(acs).astype(BF16),
                                   jnp.exp(acs[q - 1:q, :] - acs).astype(BF16)], axis=0), expand)
        ea_x = fx[0:q]
        dec_x = fx[q:2 * q]
        xdt = jnp.concatenate([act[sl, r0:r0 + q, :] for sl in range(SSD_INNER // LANES)], axis=1) * dt_x
        xb = xdt.astype(BF16)
        xdb = (xdt * dec_x).astype(BF16)
        s_all = state[...]
        sb = s_all.astype(BF16)
        for g in range(SSD_GROUPS):
            gcols = slice(g * gcw, (g + 1) * gcw)
            bg = act[col_b // LANES + g, r0:r0 + q, :]
            cg = act[col_c // LANES + g, r0:r0 + q, :].astype(BF16)
            cbm = lax.dot_general(cg, bg.astype(BF16),
                                  (((1,), (1,)), ((), ())), preferred_element_type=F32)
            y_off = _dot(cg, sb[:, gcols]) * ea_x[:, gcols]
            ds = _dot(bg.T.astype(BF16), xdb[:, gcols])
            state[:, gcols] = s_all[:, gcols] * ea_x[q - 1:q, gcols] + ds
            y_diag = []
            for pq in range(gcw // LANES):
                pair = g * (gcw // LANES) + pq
                ms = []
                for hh in range(2):
                    h = 2 * pair + hh
                    a_l = jnp.broadcast_to(acs2[:, h:h + 1], (q, q))
                    a_s = jnp.broadcast_to(acs2_t[h:h + 1, :], (q, q))
                    ms.append((jnp.where(causal, jnp.exp2(a_l - a_s), 0.0) * cbm).astype(BF16))
                x_pair = xb[:, pair * LANES:(pair + 1) * LANES]
                zero = jnp.zeros_like(x_pair)
                y_diag.append(_dot(jnp.concatenate(ms, axis=1),
                                   jnp.concatenate([jnp.where(lo_half, x_pair, zero),
                                                    jnp.where(lo_half, zero, x_pair)], axis=0)))
                fill(1 if pq % 2 == 0 else 0)
            yssd[r0:r0 + q, gcols] = jnp.concatenate(y_diag, axis=1) + y_off

    y = yssd[...] + dskip_ref[...] * jnp.concatenate([act[sl] for sl in range(SSD_INNER // LANES)], axis=1)
    y = y * _silu(zg_ref[:, 0:SSD_INNER].astype(F32))
    fill(1)
    y = _rmsnorm(y, ng_ref[...])
    fill(1)
    bs = _dot(y.astype(BF16), wbs_ref[...])
    g_pool = _sigmoid(zg_ref[:, SSD_INNER:SSD_INNER + D_MODEL].astype(F32))
    g_ssd = _sigmoid(zg_ref[:, SSD_INNER + D_MODEL:SSD_INNER + 2 * D_MODEL].astype(F32))
    fill(1)
    u = g_pool * bp + g_ssd * bs
    x_new = x_ref[...] + mod_ref[0, 2:3, :] * _dot(u.astype(BF16), wout_ref[...])
    xnew_ref[...] = x_new
    fill(None)

    h2 = _rmsnorm(x_new, n2g_ref[...]) * (1.0 + mod_ref[0, 4:5, :]) + mod_ref[0, 3:4, :]
    _store_token_tiles(h2_ref, h2)
    hh, hl = _hi_lo(h2)
    nt = (((1,), (1,)), ((), ()))
    logits_t = (lax.dot_general(wrh_ref[...], hh, nt, preferred_element_type=F32)
                + lax.dot_general(wrl_ref[...], hh, nt, preferred_element_type=F32)
                + lax.dot_general(wrh_ref[...], hl, nt, preferred_element_type=F32))
    lt_ref[...] = logits_t + jnp.concatenate([br_ref[...]] * (tm // LANES), axis=1)


def _promix(x2, mod3, g1, w_main, wd_hi, wd_lo, consts, seq):
    assert len(consts) == N_MIX_CONSTS
    t = x2.shape[0]
    n = t // TM_MIX
    tiles = seq // TM_MIX
    nxt = lambda s: jnp.minimum(s, n - 1)
    cur = lambda s: jnp.maximum(s - 1, 0)

    def cspec(a):
        zeros = (0,) * a.ndim
        return pl.BlockSpec(a.shape, lambda s: zeros, pipeline_mode=pl.Buffered(1))

    return pl.pallas_call(
        functools.partial(_promix_kernel, tiles),
        out_shape=(jax.ShapeDtypeStruct((t, D_MODEL), F32),
                   jax.ShapeDtypeStruct((t * TOK_SUB, LANES), U32),
                   jax.ShapeDtypeStruct((2 * LANES, t), F32)),
        grid=(n + 1,),
        in_specs=[
            pl.BlockSpec((TM_MIX, D_MODEL), lambda s: (nxt(s), 0)),
            pl.BlockSpec((1, 6, D_MODEL), lambda s: (nxt(s) // tiles, 0, 0)),
            cspec(g1), cspec(w_main), cspec(wd_hi), cspec(wd_lo),
            pl.BlockSpec((TM_MIX, D_MODEL), lambda s: (cur(s), 0)),
            pl.BlockSpec((1, 6, D_MODEL), lambda s: (cur(s) // tiles, 0, 0)),
        ] + [cspec(a) for a in consts],
        out_specs=(pl.BlockSpec((TM_MIX, D_MODEL), lambda s: (cur(s), 0)),
                   pl.BlockSpec((TM_MIX * TOK_SUB, LANES), lambda s: (cur(s), 0)),
                   pl.BlockSpec((2 * LANES, TM_MIX), lambda s: (0, cur(s)))),
        scratch_shapes=[
            pltpu.VMEM((TM_MIX, MAIN_COLS), BF16),
            pltpu.VMEM((TM_MIX, LANES), F32),
            pltpu.VMEM((TM_MIX, SSD_INNER + 2 * D_MODEL), BF16),
            pltpu.VMEM((TM_MIX, LANES), F32),
            pltpu.VMEM((SEQ_HALO + TM_MIX, POOL_WIDTH), BF16),
            pltpu.VMEM((CONV_SLABS, CONV_HALO + TM_MIX, LANES), F32),
            pltpu.VMEM((CONV_SLABS, TM_MIX, LANES), F32),
            pltpu.VMEM((SSD_STATE, SSD_INNER), F32),
            pltpu.VMEM((TM_MIX, SSD_INNER), F32),
        ],
        compiler_params=pltpu.CompilerParams(dimension_semantics=("arbitrary",),
                                             vmem_limit_bytes=VMEM_LIMIT_PROMIX),
        name="promix",
    )(x2, mod3, g1, w_main, wd_hi, wd_lo, x2, mod3, *consts)


def _route_kernel(lt_ref, idx_ref, rf_ref, cnt_ref, carry):
    i = pl.program_id(0)
    tr = TR_ROUTE
    ck = ROUTE_CHUNK
    neg = -jnp.inf

    @pl.when(i == 0)
    def _():
        carry[...] = jnp.zeros_like(carry)

    row8 = lax.broadcasted_iota(I32, (8, tr), 0)
    lg = jnp.where(row8 < MOE_GROUPS, lt_ref[0:8, :], neg)
    mg = jnp.max(lg, axis=0, keepdims=True)
    gsel = jnp.min(jnp.where(lg == mg, row8, 8), axis=0, keepdims=True)
    gwt = 1.0 / jnp.sum(jnp.exp(lg - mg), axis=0, keepdims=True)
    rowe = lax.broadcasted_iota(I32, (N_EXPERTS, tr), 0)
    le = jnp.where(lax.shift_right_logical(rowe, 3) == gsel, lt_ref[LANES:LANES + N_EXPERTS, :], neg)
    m1 = jnp.max(le, axis=0, keepdims=True)
    e0 = jnp.min(jnp.where(le == m1, rowe, N_EXPERTS), axis=0, keepdims=True)
    le2 = jnp.where(rowe == e0, neg, le)
    m2 = jnp.max(le2, axis=0, keepdims=True)
    e1 = jnp.min(jnp.where(le2 == m2, rowe, N_EXPERTS), axis=0, keepdims=True)
    t21 = jnp.exp(m2 - m1)
    w0 = gwt / (1.0 + t21)
    w1 = gwt * t21 / (1.0 + t21)
    oh0 = jnp.where(rowe == e0, 1.0, 0.0)
    oh1 = jnp.where(rowe == e1, 1.0, 0.0)
    oh = oh0 + oh1

    kr = lax.broadcasted_iota(I32, (ck, ck), 0)
    kc = lax.broadcasted_iota(I32, (ck, ck), 1)
    before = jnp.where(kr < kc, 1.0, 0.0).astype(BF16)
    wpad = jnp.zeros((LANES - 2, ck), F32)
    for c0 in range(0, tr, ck):
        cs = slice(c0, c0 + ck)
        pre = _dot(oh[:, cs].astype(BF16), before) + carry[:, 0:1]
        idx_ref[2:3, cs] = jnp.sum(pre * oh0[:, cs], axis=0, keepdims=True).astype(I32)
        idx_ref[3:4, cs] = jnp.sum(pre * oh1[:, cs], axis=0, keepdims=True).astype(I32)
        carry[...] = carry[...] + jnp.sum(oh[:, cs], axis=1, keepdims=True)
        rf_ref[cs, :] = jnp.concatenate([w0[:, cs], w1[:, cs], wpad], axis=0).T
    idx_ref[0:1, :] = e0
    idx_ref[1:2, :] = e1
    idx_ref[4:8, :] = jnp.zeros((4, tr), I32)
    cnt_ref[...] = carry[...].astype(I32)


def _route(logits_t):
    t = logits_t.shape[1]
    return pl.pallas_call(
        _route_kernel,
        out_shape=(jax.ShapeDtypeStruct((8, t), I32),
                   jax.ShapeDtypeStruct((t, LANES), F32),
                   jax.ShapeDtypeStruct((N_EXPERTS, LANES), I32)),
        grid=(t // TR_ROUTE,),
        in_specs=[pl.BlockSpec((2 * LANES, TR_ROUTE), lambda i: (0, i))],
        out_specs=(pl.BlockSpec((8, TR_ROUTE), lambda i: (0, i)),
                   pl.BlockSpec((TR_ROUTE, LANES), lambda i: (i, 0)),
                   pl.BlockSpec((N_EXPERTS, LANES), lambda i: (0, 0))),
        scratch_shapes=[pltpu.VMEM((N_EXPERTS, LANES), F32)],
        compiler_params=pltpu.CompilerParams(dimension_semantics=("arbitrary",)),
        name="moe_route",
    )(logits_t)


def _scatter_kernel(ps_ref, pv_ref, pe_ref, nu_ref, idx_ref, h_ref, xb_ref, zblk, sem, zsem):
    i = pl.program_id(0)
    ts = TS_SCATTER
    bm = BM_EXPERT
    nb = xb_ref.shape[0] // (bm * TOK_SUB)

    @pl.when(i == 0)
    def _():
        zblk[...] = jnp.zeros_like(zblk)

        def blk_copy(blk):
            start = pl.multiple_of(blk * (bm * TOK_SUB), bm * TOK_SUB)
            return pltpu.make_async_copy(zblk, xb_ref.at[pl.ds(start, bm * TOK_SUB)], zsem)

        def for_pad_blocks(fn):
            def per_expert(e, carry):
                @pl.when(pv_ref[e] < pe_ref[e])
                def _():
                    fn(blk_copy(lax.div(pe_ref[e], bm) - 1))
                return carry
            lax.fori_loop(0, N_EXPERTS, per_expert, 0)

            def per_block(blk, carry):
                fn(blk_copy(blk))
                return carry
            lax.fori_loop(nu_ref[0], nb, per_block, 0)

        for_pad_blocks(lambda cp: cp.start())
        for_pad_blocks(lambda cp: cp.wait())

    def issue(j, carry):
        for u in range(ISSUE_UNROLL):
            r = j * ISSUE_UNROLL + u
            for k in range(2):
                dst = ps_ref[idx_ref[k, r]] + idx_ref[2 + k, r]
                pltpu.make_async_copy(_token_tile(h_ref, r), _token_tile(xb_ref, dst),
                                      sem.at[k]).start(priority=k)
        return carry

    lax.fori_loop(0, ts // ISSUE_UNROLL, issue, 0)
    for k in range(2):
        pltpu.make_async_copy(h_ref, xb_ref.at[pl.ds(0, ts * TOK_SUB)], sem.at[k]).wait()


def _scatter(pad_starts, pad_valid_end, pad_ends, n_used, idx, h2t, n_rows):
    t = h2t.shape[0] // TOK_SUB
    return pl.pallas_call(
        _scatter_kernel,
        out_shape=jax.ShapeDtypeStruct((n_rows * TOK_SUB, LANES), U32),
        grid_spec=pltpu.PrefetchScalarGridSpec(
            num_scalar_prefetch=4,
            grid=(t // TS_SCATTER,),
            in_specs=[
                pl.BlockSpec((8, TS_SCATTER), lambda i, *_: (0, i), memory_space=pltpu.SMEM),
                pl.BlockSpec((TS_SCATTER * TOK_SUB, LANES), lambda i, *_: (i, 0)),
            ],
            out_specs=pl.BlockSpec(memory_space=pl.ANY),
            scratch_shapes=[pltpu.VMEM((BM_EXPERT * TOK_SUB, LANES), U32),
                            pltpu.SemaphoreType.DMA((2,)),
                            pltpu.SemaphoreType.DMA(())],
        ),
        compiler_params=pltpu.CompilerParams(dimension_semantics=("arbitrary",),
                                             disable_bounds_checks=True),
        name="moe_scatter",
    )(pad_starts, pad_valid_end, pad_ends, n_used, idx, h2t)


def _expert_kernel(be_ref, nu_ref, x_ref, w13_ref, w2_ref, y_ref, w13b, w2b):
    i = pl.program_id(0)

    @pl.when((i == 0) | (be_ref[i] != be_ref[jnp.maximum(i - 1, 0)]))
    def _():
        w13b[...] = w13_ref[0].astype(BF16)
        w2b[...] = w2_ref[0].astype(BF16)

    @pl.when(i < nu_ref[0])
    def _():
        x = _load_token_tiles(x_ref).astype(BF16)
        a = _dot(x, w13b[:, 0:EXPERT_FF])
        g = _dot(x, w13b[:, EXPERT_FF:2 * EXPERT_FF])
        _store_token_tiles(y_ref, _dot((_silu(a) * g).astype(BF16), w2b[...]))

    @pl.when(i >= nu_ref[0])
    def _():
        y_ref[...] = jnp.zeros_like(y_ref)


def _experts(block_expert, n_used, xbuf, w13, w2):
    nb = xbuf.shape[0] // (BM_EXPERT * TOK_SUB)
    used = lambda i, nu: jnp.minimum(i, nu[0] - 1)
    return pl.pallas_call(
        _expert_kernel,
        out_shape=jax.ShapeDtypeStruct(xbuf.shape, U32),
        grid_spec=pltpu.PrefetchScalarGridSpec(
            num_scalar_prefetch=2,
            grid=(nb,),
            in_specs=[
                pl.BlockSpec((BM_EXPERT * TOK_SUB, LANES), lambda i, be, nu: (used(i, nu), 0)),
                pl.BlockSpec((1, D_MODEL, 2 * EXPERT_FF), lambda i, be, nu: (be[i], 0, 0)),
                pl.BlockSpec((1, EXPERT_FF, D_MODEL), lambda i, be, nu: (be[i], 0, 0)),
            ],
            out_specs=pl.BlockSpec((BM_EXPERT * TOK_SUB, LANES), lambda i, be, nu: (i, 0)),
            scratch_shapes=[pltpu.VMEM((D_MODEL, 2 * EXPERT_FF), BF16),
                            pltpu.VMEM((EXPERT_FF, D_MODEL), BF16)],
        ),
        compiler_params=pltpu.CompilerParams(dimension_semantics=("arbitrary",),
                                             vmem_limit_bytes=VMEM_LIMIT),
        name="moe_experts",
    )(block_expert, n_used, xbuf, w13, w2)


def _combine_kernel(ps_ref, idx_ref, idxn_ref, x_ref, rf_ref, mod_ref, fg_ref, yb_ref, out_ref, gbuf, sem):
    i = pl.program_id(0)
    n = pl.num_programs(0)
    tc = TC_COMBINE
    slot = lax.rem(i, 2)

    def start_gather(idx, s):
        def issue(j, carry):
            for u in range(ISSUE_UNROLL):
                r = j * ISSUE_UNROLL + u
                for k in range(2):
                    src = ps_ref[idx[k, r]] + idx[2 + k, r]
                    pltpu.make_async_copy(_token_tile(yb_ref, src), _token_tile(gbuf.at[s, k], r),
                                          sem.at[s, k]).start(priority=k)
            return carry
        lax.fori_loop(0, tc // ISSUE_UNROLL, issue, 0)

    @pl.when(i == 0)
    def _():
        start_gather(idx_ref, 0)

    @pl.when(i + 1 < n)
    def _():
        start_gather(idxn_ref, 1 - slot)

    for k in range(2):
        pltpu.make_async_copy(yb_ref.at[pl.ds(0, tc * TOK_SUB)], gbuf.at[slot, k], sem.at[slot, k]).wait()
    moe = (rf_ref[:, 0:1] * _load_token_tiles(gbuf.at[slot, 0])
           + rf_ref[:, 1:2] * _load_token_tiles(gbuf.at[slot, 1]))
    out_ref[...] = _rmsnorm(x_ref[...] + mod_ref[0, 5:6, :] * moe, fg_ref[...])


def _combine(pad_starts, idx, x_new, rf, mod3, fg, ybuf, seq):
    t = x_new.shape[0]
    n = t // TC_COMBINE
    tiles_per_seq = seq // TC_COMBINE
    return pl.pallas_call(
        _combine_kernel,
        out_shape=jax.ShapeDtypeStruct((t, D_MODEL), F32),
        grid_spec=pltpu.PrefetchScalarGridSpec(
            num_scalar_prefetch=1,
            grid=(n,),
            in_specs=[
                pl.BlockSpec((8, TC_COMBINE), lambda i, ps: (0, i), memory_space=pltpu.SMEM),
                pl.BlockSpec((8, TC_COMBINE), lambda i, ps: (0, jnp.minimum(i + 1, n - 1)),
                             memory_space=pltpu.SMEM),
                pl.BlockSpec((TC_COMBINE, D_MODEL), lambda i, ps: (i, 0)),
                pl.BlockSpec((TC_COMBINE, LANES), lambda i, ps: (i, 0)),
                pl.BlockSpec((1, 6, D_MODEL), lambda i, ps: (i // tiles_per_seq, 0, 0)),
                pl.BlockSpec((1, D_MODEL), lambda i, ps: (0, 0)),
                pl.BlockSpec(memory_space=pl.ANY),
            ],
            out_specs=pl.BlockSpec((TC_COMBINE, D_MODEL), lambda i, ps: (i, 0)),
            scratch_shapes=[pltpu.VMEM((2, 2, TC_COMBINE * TOK_SUB, LANES), U32),
                            pltpu.SemaphoreType.DMA((2, 2))],
        ),
        compiler_params=pltpu.CompilerParams(dimension_semantics=("arbitrary",),
                                             disable_bounds_checks=True),
        name="moe_combine",
    )(pad_starts, idx, idx, x_new, rf, mod3, fg, ybuf)


def _pad_lanes(a, n=LANES):
    return jnp.pad(a, [(0, 0)] * (a.ndim - 1) + [(0, n - a.shape[-1])])


def _selection_matrices():
    q = SSD_CHUNK
    rel = jnp.arange(SEQ_HALO + q)[None, :] - jnp.arange(q)[:, None] - SEQ_HALO
    band = jnp.stack([(rel <= 0) & (rel > -w) for w in POOL_WINDOWS]).astype(BF16)
    expand = (jnp.arange(LANES)[:, None] == jnp.arange(SSD_INNER)[None, :] // SSD_HEAD_DIM).astype(BF16)
    return band, expand


def _layer(x2, c, bsz, seq, ada_w, ada_b, norm1_g, w_in, pool_w, pool_scale, conv_w, conv_b, dt_bias,
           a_log, d_skip, ssd_norm_g, w_branch_pool, w_branch_ssd, w_out, norm2_g, router_group_w,
           router_group_b, router_expert_w, router_expert_b, w13, w2):
    t = bsz * seq
    mod3 = _ada(c, ada_w, ada_b).reshape(bsz, 6, D_MODEL)

    i2 = POOL_WIDTH + SSD_INNER + SSD_CONV_DIM
    i3 = i2 + SSD_HEADS
    w_main = jnp.concatenate([w_in[:, :i2], w_in[:, i3:]], axis=1).astype(BF16)
    wd = _pad_lanes(w_in[:, i2:i3])
    wd_hi = wd.astype(BF16)
    wd_lo = (wd - wd_hi.astype(F32)).astype(BF16)

    wr = jnp.concatenate([_pad_lanes(router_group_w), _pad_lanes(router_expert_w)], axis=1).T
    wr_hi = wr.astype(BF16)
    wr_lo = (wr - wr_hi.astype(F32)).astype(BF16)
    br = jnp.concatenate([_pad_lanes(router_group_b.reshape(1, -1)),
                          _pad_lanes(router_expert_b.reshape(1, -1))], axis=1)
    br = jnp.broadcast_to(br.reshape(2 * LANES, 1), (2 * LANES, LANES))
    consts = [
        pool_w.astype(BF16),
        pool_scale.reshape(1, POOL_WIDTH),
        conv_w,
        conv_b.reshape(1, SSD_CONV_DIM),
        _pad_lanes(dt_bias.reshape(1, SSD_HEADS)),
        jnp.broadcast_to(jnp.pad(a_log, (0, LANES - SSD_HEADS))[:, None], (LANES, LANES)),
        jnp.repeat(d_skip, SSD_HEAD_DIM).reshape(1, SSD_INNER),
        ssd_norm_g.reshape(1, SSD_INNER),
        w_branch_pool.astype(BF16),
        w_branch_ssd.astype(BF16),
        w_out.astype(BF16),
        norm2_g.reshape(1, D_MODEL),
        wr_hi, wr_lo, br,
        *_selection_matrices(),
    ]
    x_new, h2, logits_t = _promix(x2, mod3, norm1_g.reshape(1, D_MODEL), w_main, wd_hi, wd_lo, consts, seq)
    idx, rf, cnt = _route(logits_t)

    counts = cnt[:, 0]
    padded = ((counts + BM_EXPERT - 1) // BM_EXPERT) * BM_EXPERT
    pad_ends = jnp.cumsum(padded)
    pad_starts = (pad_ends - padded).astype(I32)
    n_rows = 2 * t + N_EXPERTS * BM_EXPERT
    nb = n_rows // BM_EXPERT
    block_row0 = jnp.arange(nb, dtype=I32) * BM_EXPERT
    block_expert = jnp.minimum(jnp.sum((pad_ends[None, :] <= block_row0[:, None]).astype(I32), axis=1),
                               N_EXPERTS - 1)
    n_used = (pad_ends[-1:] // BM_EXPERT).astype(I32)

    xbuf = _scatter(pad_starts, pad_starts + counts, pad_ends.astype(I32), n_used, idx, h2, n_rows)
    ybuf = _experts(block_expert, n_used, xbuf, w13, w2)
    return pad_starts, idx, x_new, rf, mod3, ybuf


def kernel(x, c, ada_w, ada_b, norm1_g, w_in, pool_w, pool_scale, conv_w, conv_b, dt_bias, a_log, d_skip, ssd_norm_g, w_branch_pool, w_branch_ssd, w_out, norm2_g, router_group_w, router_group_b, router_expert_w, router_expert_b, w13, w2, final_norm_g):
    bsz, seq, d = x.shape
    depth = ada_w.shape[0]
    assert depth == 1 and d == D_MODEL and seq % TM_MIX == 0
    x2 = x.reshape(bsz * seq, d)
    l = 0
    pad_starts, idx_c, x_new, rf, mod3, ybuf = _layer(
        x2, c, bsz, seq, ada_w[l], ada_b[l], norm1_g[l], w_in[l], pool_w[l], pool_scale[l], conv_w[l],
        conv_b[l], dt_bias[l], a_log[l], d_skip[l], ssd_norm_g[l], w_branch_pool[l], w_branch_ssd[l],
        w_out[l], norm2_g[l], router_group_w[l], router_group_b[l], router_expert_w[l],
        router_expert_b[l], w13[l], w2[l])
    out = _combine(pad_starts, idx_c, x_new, rf, mod3, final_norm_g.reshape(1, d), ybuf, seq)
    return out.reshape(bsz, seq, d)
```

```python
import functools

import jax
import jax.numpy as jnp
from jax import lax
from jax.experimental import pallas as pl
from jax.experimental.pallas import tpu as pltpu

F32 = jnp.float32
BF16 = jnp.bfloat16
I32 = jnp.int32
U32 = jnp.uint32

D_MODEL = 1024
POOL_WIDTH = 1024
POOL_GROUPS = 4
POOL_GROUP_WIDTH = 256
POOL_WINDOWS = (2, 4, 8, 16)
SSD_INNER = 2048
SSD_HEAD_DIM = 64
SSD_HEADS = 32
SSD_GROUPS = 4
SSD_STATE = 128
SSD_CONV = 4
SSD_CHUNK = 128
SSD_CONV_DIM = SSD_INNER + 2 * SSD_GROUPS * SSD_STATE
N_EXPERTS = 32
EXPERTS_PER_GROUP = 8
MOE_GROUPS = 4
EXPERT_FF = 512
EPS = 1e-6

LANES = 128
TOK_SUB = D_MODEL // (2 * LANES)
MAIN_COLS = POOL_WIDTH + SSD_INNER + SSD_CONV_DIM + 2 * D_MODEL
COL_Z = POOL_WIDTH
COL_XBC = COL_Z + SSD_INNER
COL_GATE = COL_XBC + SSD_CONV_DIM

TM_MIX = 256
PROJ_PIECE = 256
SEQ_HALO = SSD_CHUNK
CONV_HALO = 8
CONV_SLABS = SSD_CONV_DIM // LANES
LOG2E = 1.4426950408889634
TR_ROUTE = 2048
ROUTE_CHUNK = 256
TS_SCATTER = 1024
ISSUE_UNROLL = 8
BM_EXPERT = 512
TC_COMBINE = 256
VMEM_LIMIT = 56 * 1024 * 1024
VMEM_LIMIT_PROMIX = 60 * 1024 * 1024


def _dot(a, b):
    return jnp.dot(a, b, preferred_element_type=F32)


def _hi_lo(x):
    hi = x.astype(BF16)
    lo = (x - hi.astype(F32)).astype(BF16)
    return hi, lo


def _split3(x):
    hi = x.astype(BF16)
    r = x - hi.astype(F32)
    mid = r.astype(BF16)
    lo = (r - mid.astype(F32)).astype(BF16)
    return hi, mid, lo


def _dot3(a, b):
    ah, al = _hi_lo(a)
    bh, bl = _hi_lo(b)
    return _dot(ah, bh) + _dot(ah, bl) + _dot(al, bh)


def _sigmoid(x):
    return 1.0 / (1.0 + jnp.exp(-x))


def _silu(x):
    return x * _sigmoid(x)


def _softplus(x):
    return jnp.maximum(x, 0.0) + jnp.log1p(jnp.exp(-jnp.abs(x)))


def _rmsnorm(x, g):
    return x * lax.rsqrt(jnp.mean(x * x, axis=-1, keepdims=True) + EPS) * g


def _bf16_bits(v):
    return lax.bitcast_convert_type(v.astype(BF16).astype(F32), U32) & jnp.uint32(0xFFFF0000)


def _store_token_tiles(ref, v):
    n = v.shape[0]
    half = v.shape[1] // 2
    words = _bf16_bits(v[:, half:]) | (_bf16_bits(v[:, :half]) >> 16)
    for j in range(TOK_SUB):
        ref[pl.ds(j, n, stride=TOK_SUB), :] = words[:, j * LANES:(j + 1) * LANES]


def _load_token_tiles(ref):
    n = ref.shape[0] // TOK_SUB
    words = jnp.concatenate([ref[pl.ds(j, n, stride=TOK_SUB), :] for j in range(TOK_SUB)], axis=1)
    lo = lax.bitcast_convert_type(words << 16, F32)
    hi = lax.bitcast_convert_type(words & jnp.uint32(0xFFFF0000), F32)
    return jnp.concatenate([lo, hi], axis=1)


def _token_tile(ref, row):
    return ref.at[pl.ds(pl.multiple_of(row * TOK_SUB, TOK_SUB), TOK_SUB)]


def _ada_kernel(c_ref, w_ref, b_ref, o_ref):
    o_ref[...] = _dot3(_silu(c_ref[...]), w_ref[...]) + b_ref[...]


def _ada(c, w, b):
    bsz = c.shape[0]
    n = w.shape[1]
    tn = 1024
    return pl.pallas_call(
        _ada_kernel,
        out_shape=jax.ShapeDtypeStruct((bsz, n), F32),
        grid=(n // tn,),
        in_specs=[
            pl.BlockSpec((bsz, D_MODEL), lambda j: (0, 0)),
            pl.BlockSpec((D_MODEL, tn), lambda j: (0, j)),
            pl.BlockSpec((1, tn), lambda j: (0, j)),
        ],
        out_specs=pl.BlockSpec((bsz, tn), lambda j: (0, j)),
        compiler_params=pltpu.CompilerParams(dimension_semantics=("arbitrary",)),
        name="ada",
    )(c, w, b.reshape(1, n))


N_MIX_CONSTS = 17


def _promix_kernel(tiles_per_seq, xa_ref, moda_ref, g_ref, w_ref, wdh_ref, wdl_ref, xb_ref, modb_ref, *rest):
    consts = rest[:N_MIX_CONSTS]
    xnew_ref, h2_ref, lt_ref = rest[N_MIX_CONSTS:N_MIX_CONSTS + 3]
    pbuf, dtbuf, zg, dts, xp_ext, xbc_ext, act, state, yssd = rest[N_MIX_CONSTS + 3:]
    s = pl.program_id(0)
    j = lax.rem(jnp.maximum(s - 1, 0), tiles_per_seq)
    tm = TM_MIX

    @pl.when(s == 0)
    def _():
        pbuf[...] = jnp.zeros_like(pbuf)
        dtbuf[...] = jnp.zeros_like(dtbuf)

    @pl.when(j == 0)
    def _():
        xp_ext[0:SEQ_HALO, :] = jnp.zeros((SEQ_HALO, POOL_WIDTH), BF16)
        xbc_ext[:, 0:CONV_HALO, :] = jnp.zeros((CONV_SLABS, CONV_HALO, LANES), F32)
        state[...] = jnp.zeros_like(state)

    xp_ext[SEQ_HALO:SEQ_HALO + tm, :] = pbuf[:, 0:POOL_WIDTH]
    for sl in range(CONV_SLABS):
        xbc_ext[sl, CONV_HALO:CONV_HALO + tm, :] = (
            pbuf[:, COL_XBC + sl * LANES:COL_XBC + (sl + 1) * LANES].astype(F32))
    zg[:, 0:SSD_INNER] = pbuf[:, COL_Z:COL_Z + SSD_INNER]
    zg[:, SSD_INNER:SSD_INNER + 2 * D_MODEL] = pbuf[:, COL_GATE:COL_GATE + 2 * D_MODEL]
    dts[...] = dtbuf[...]

    h = _rmsnorm(xa_ref[...], g_ref[...]) * (1.0 + moda_ref[0, 1:2, :]) + moda_ref[0, 0:1, :]
    hb, hl = _hi_lo(h)

    def project(c0):
        def piece():
            pbuf[:, c0:c0 + PROJ_PIECE] = _dot(hb, w_ref[:, c0:c0 + PROJ_PIECE]).astype(BF16)
        return piece

    def project_dt():
        dtbuf[...] = _dot(hb, wdh_ref[...]) + _dot(hb, wdl_ref[...]) + _dot(hl, wdh_ref[...])

    fill = _Filler([project(c0) for c0 in range(0, MAIN_COLS, PROJ_PIECE)] + [project_dt])
    _mix_main(fill, j, zg, dts, xb_ref, modb_ref, *consts,
              xnew_ref, h2_ref, lt_ref, xp_ext, xbc_ext, act, state, yssd)


class _Filler:
    def __init__(self, thunks):
        self.pending = list(thunks)

    def __call__(self, n):
        n = len(self.pending) if n is None else n
        for _ in range(min(n, len(self.pending))):
            self.pending.pop(0)()


def _mix_main(fill, j, zg_ref, dt_ref, x_ref, mod_ref, poolw_ref, pools_ref, convw_ref, convb_ref,
              dtb_ref, alog_ref, dskip_ref, ng_ref, wbp_ref, wbs_ref, wout_ref, n2g_ref,
              wrh_ref, wrl_ref, br_ref, band_ref, expand_ref,
              xnew_ref, h2_ref, lt_ref,
              xp_ext, xbc_ext, act, state, yssd):
    tm = TM_MIX
    q = SSD_CHUNK
    halo = SEQ_HALO


    gw = POOL_GROUP_WIDTH
    pos = j * tm + lax.broadcasted_iota(I32, (tm, gw), 0)
    yp = []
    for gi, w in enumerate(POOL_WINDOWS):
        cols = slice(gi * gw, (gi + 1) * gw)
        acc = jnp.concatenate([_dot(band_ref[gi], xp_ext[r0:r0 + halo + q, cols])
                               for r0 in range(0, tm, q)], axis=0)
        cur = xp_ext[halo:halo + tm, cols].astype(F32)
        d = acc / jnp.minimum(pos + 1, w).astype(F32) - cur
        yp.append(_dot(d.astype(BF16), poolw_ref[gi]) * pools_ref[:, cols])
        fill(1)
    bp = _dot(jnp.concatenate(yp, axis=1).astype(BF16), wbp_ref[...])

    first = CONV_HALO - (SSD_CONV - 1)
    for sl in range(CONV_SLABS):
        cols = slice(sl * LANES, (sl + 1) * LANES)
        for par in range(2):
            acc = convb_ref[:, cols] + convw_ref[0:1, cols] * xbc_ext[sl, pl.ds(first + par, tm // 2, stride=2), :]
            for k in range(1, SSD_CONV):
                acc = acc + convw_ref[k:k + 1, cols] * xbc_ext[sl, pl.ds(first + par + k, tm // 2, stride=2), :]
            act[sl, pl.ds(par, tm // 2, stride=2), :] = _silu(acc)
        if sl % 2 == 1:
            fill(1)

    xp_ext[0:halo, :] = xp_ext[tm:tm + halo, :]
    xbc_ext[:, 0:CONV_HALO, :] = xbc_ext[:, tm:tm + CONV_HALO, :]

    lane = lax.broadcasted_iota(I32, (q, q), 1)
    row = lax.broadcasted_iota(I32, (q, q), 0)
    causal = row >= lane
    triu = jnp.where(row <= lane, 1.0, 0.0).astype(BF16)
    lo_half = lane < SSD_HEAD_DIM
    a_rows = -jnp.exp(alog_ref[...])
    col_b = SSD_INNER
    col_c = SSD_INNER + SSD_GROUPS * SSD_STATE
    gcw = SSD_INNER // SSD_GROUPS
    for c in range(tm // q):
        r0 = c * q
        dtv = _softplus(dt_ref[r0:r0 + q, :] + dtb_ref[...])
        dt_t = dtv.T
        p1, p2, p3 = _split3(dt_t * a_rows)
        acs_t = _dot(p1, triu) + _dot(p2, triu) + _dot(p3, triu)
        acs = acs_t.T
        acs2 = acs * LOG2E
        acs2_t = acs_t * LOG2E
        expand = expand_ref[...]
        dt_x = _dot(jnp.concatenate(_hi_lo(dtv), axis=1), jnp.concatenate([expand, expand], axis=0))
        fx = _dot(jnp.concatenate([jnp.exp(acs).astype(BF16),
                                   jnp.exp(acs[q - 1:q, :] - acs).astype(BF16)], axis=0), expand)
        ea_x = fx[0:q]
        dec_x = fx[q:2 * q]
        xdt = jnp.concatenate([act[sl, r0:r0 + q, :] for sl in range(SSD_INNER // LANES)], axis=1) * dt_x
        xb = xdt.astype(BF16)
        xdb = (xdt * dec_x).astype(BF16)
        s_all = state[...]
        sb = s_all.astype(BF16)
        for g in range(SSD_GROUPS):
            gcols = slice(g * gcw, (g + 1) * gcw)
            bg = act[col_b // LANES + g, r0:r0 + q, :]
            cg = act[col_c // LANES + g, r0:r0 + q, :].astype(BF16)
            cbm = lax.dot_general(cg, bg.astype(BF16),
                                  (((1,), (1,)), ((), ())), preferred_element_type=F32)
            y_off = _dot(cg, sb[:, gcols]) * ea_x[:, gcols]
            ds = _dot(bg.T.astype(BF16), xdb[:, gcols])
            state[:, gcols] = s_all[:, gcols] * ea_x[q - 1:q, gcols] + ds
            y_diag = []
            for pq in range(gcw // LANES):
                pair = g * (gcw // LANES) + pq
                ms = []
                for hh in range(2):
                    h = 2 * pair + hh
                    a_l = jnp.broadcast_to(acs2[:, h:h + 1], (q, q))
                    a_s = jnp.broadcast_to(acs2_t[h:h + 1, :], (q, q))
                    ms.append((jnp.where(causal, jnp.exp2(a_l - a_s), 0.0) * cbm).astype(BF16))
                x_pair = xb[:, pair * LANES:(pair + 1) * LANES]
                zero = jnp.zeros_like(x_pair)
                y_diag.append(_dot(jnp.concatenate(ms, axis=1),
                                   jnp.concatenate([jnp.where(lo_half, x_pair, zero),
                                                    jnp.where(lo_half, zero, x_pair)], axis=0)))
                fill(1 if pq % 2 == 0 else 0)
            yssd[r0:r0 + q, gcols] = jnp.concatenate(y_diag, axis=1) + y_off

    y = yssd[...] + dskip_ref[...] * jnp.concatenate([act[sl] for sl in range(SSD_INNER // LANES)], axis=1)
    y = y * _silu(zg_ref[:, 0:SSD_INNER].astype(F32))
    fill(1)
    y = _rmsnorm(y, ng_ref[...])
    fill(1)
    bs = _dot(y.astype(BF16), wbs_ref[...])
    g_pool = _sigmoid(zg_ref[:, SSD_INNER:SSD_INNER + D_MODEL].astype(F32))
    g_ssd = _sigmoid(zg_ref[:, SSD_INNER + D_MODEL:SSD_INNER + 2 * D_MODEL].astype(F32))
    fill(1)
    u = g_pool * bp + g_ssd * bs
    x_new = x_ref[...] + mod_ref[0, 2:3, :] * _dot(u.astype(BF16), wout_ref[...])
    xnew_ref[...] = x_new
    fill(None)

    h2 = _rmsnorm(x_new, n2g_ref[...]) * (1.0 + mod_ref[0, 4:5, :]) + mod_ref[0, 3:4, :]
    _store_token_tiles(h2_ref, h2)
    hh, hl = _hi_lo(h2)
    nt = (((1,), (1,)), ((), ()))
    logits_t = (lax.dot_general(wrh_ref[...], hh, nt, preferred_element_type=F32)
                + lax.dot_general(wrl_ref[...], hh, nt, preferred_element_type=F32)
                + lax.dot_general(wrh_ref[...], hl, nt, preferred_element_type=F32))
    lt_ref[...] = logits_t + jnp.concatenate([br_ref[...]] * (tm // LANES), axis=1)


def _promix(x2, mod3, g1, w_main, wd_hi, wd_lo, consts, seq):
    assert len(consts) == N_MIX_CONSTS
    t = x2.shape[0]
    n = t // TM_MIX
    tiles = seq // TM_MIX
    nxt = lambda s: jnp.minimum(s, n - 1)
    cur = lambda s: jnp.maximum(s - 1, 0)

    def cspec(a):
        zeros = (0,) * a.ndim
        return pl.BlockSpec(a.shape, lambda s: zeros, pipeline_mode=pl.Buffered(1))

    return pl.pallas_call(
        functools.partial(_promix_kernel, tiles),
        out_shape=(jax.ShapeDtypeStruct((t, D_MODEL), F32),
                   jax.ShapeDtypeStruct((t * TOK_SUB, LANES), U32),
                   jax.ShapeDtypeStruct((2 * LANES, t), F32)),
        grid=(n + 1,),
        in_specs=[
            pl.BlockSpec((TM_MIX, D_MODEL), lambda s: (nxt(s), 0)),
            pl.BlockSpec((1, 6, D_MODEL), lambda s: (nxt(s) // tiles, 0, 0)),
            cspec(g1), cspec(w_main), cspec(wd_hi), cspec(wd_lo),
            pl.BlockSpec((TM_MIX, D_MODEL), lambda s: (cur(s), 0)),
            pl.BlockSpec((1, 6, D_MODEL), lambda s: (cur(s) // tiles, 0, 0)),
        ] + [cspec(a) for a in consts],
        out_specs=(pl.BlockSpec((TM_MIX, D_MODEL), lambda s: (cur(s), 0)),
                   pl.BlockSpec((TM_MIX * TOK_SUB, LANES), lambda s: (cur(s), 0)),
                   pl.BlockSpec((2 * LANES, TM_MIX), lambda s: (0, cur(s)))),
        scratch_shapes=[
            pltpu.VMEM((TM_MIX, MAIN_COLS), BF16),
            pltpu.VMEM((TM_MIX, LANES), F32),
            pltpu.VMEM((TM_MIX, SSD_INNER + 2 * D_MODEL), BF16),
            pltpu.VMEM((TM_MIX, LANES), F32),
            pltpu.VMEM((SEQ_HALO + TM_MIX, POOL_WIDTH), BF16),
            pltpu.VMEM((CONV_SLABS, CONV_HALO + TM_MIX, LANES), F32),
            pltpu.VMEM((CONV_SLABS, TM_MIX, LANES), F32),
            pltpu.VMEM((SSD_STATE, SSD_INNER), F32),
            pltpu.VMEM((TM_MIX, SSD_INNER), F32),
        ],
        compiler_params=pltpu.CompilerParams(dimension_semantics=("arbitrary",),
                                             vmem_limit_bytes=VMEM_LIMIT_PROMIX),
        name="promix",
    )(x2, mod3, g1, w_main, wd_hi, wd_lo, x2, mod3, *consts)


def _route_kernel(lt_ref, idx_ref, rf_ref, cnt_ref, carry):
    i = pl.program_id(0)
    tr = TR_ROUTE
    ck = ROUTE_CHUNK
    neg = -jnp.inf

    @pl.when(i == 0)
    def _():
        carry[...] = jnp.zeros_like(carry)

    row8 = lax.broadcasted_iota(I32, (8, tr), 0)
    lg = jnp.where(row8 < MOE_GROUPS, lt_ref[0:8, :], neg)
    mg = jnp.max(lg, axis=0, keepdims=True)
    gsel = jnp.min(jnp.where(lg == mg, row8, 8), axis=0, keepdims=True)
    gwt = 1.0 / jnp.sum(jnp.exp(lg - mg), axis=0, keepdims=True)
    rowe = lax.broadcasted_iota(I32, (N_EXPERTS, tr), 0)
    le = jnp.where(lax.shift_right_logical(rowe, 3) == gsel, lt_ref[LANES:LANES + N_EXPERTS, :], neg)
    m1 = jnp.max(le, axis=0, keepdims=True)
    e0 = jnp.min(jnp.where(le == m1, rowe, N_EXPERTS), axis=0, keepdims=True)
    le2 = jnp.where(rowe == e0, neg, le)
    m2 = jnp.max(le2, axis=0, keepdims=True)
    e1 = jnp.min(jnp.where(le2 == m2, rowe, N_EXPERTS), axis=0, keepdims=True)
    t21 = jnp.exp(m2 - m1)
    w0 = gwt / (1.0 + t21)
    w1 = gwt * t21 / (1.0 + t21)
    oh0 = jnp.where(rowe == e0, 1.0, 0.0)
    oh1 = jnp.where(rowe == e1, 1.0, 0.0)
    oh = oh0 + oh1

    kr = lax.broadcasted_iota(I32, (ck, ck), 0)
    kc = lax.broadcasted_iota(I32, (ck, ck), 1)
    before = jnp.where(kr < kc, 1.0, 0.0).astype(BF16)
    wpad = jnp.zeros((LANES - 2, ck), F32)
    for c0 in range(0, tr, ck):
        cs = slice(c0, c0 + ck)
        pre = _dot(oh[:, cs].astype(BF16), before) + carry[:, 0:1]
        idx_ref[2:3, cs] = jnp.sum(pre * oh0[:, cs], axis=0, keepdims=True).astype(I32)
        idx_ref[3:4, cs] = jnp.sum(pre * oh1[:, cs], axis=0, keepdims=True).astype(I32)
        carry[...] = carry[...] + jnp.sum(oh[:, cs], axis=1, keepdims=True)
        rf_ref[cs, :] = jnp.concatenate([w0[:, cs], w1[:, cs], wpad], axis=0).T
    idx_ref[0:1, :] = e0
    idx_ref[1:2, :] = e1
    idx_ref[4:8, :] = jnp.zeros((4, tr), I32)
    cnt_ref[...] = carry[...].astype(I32)


def _route(logits_t):
    t = logits_t.shape[1]
    return pl.pallas_call(
        _route_kernel,
        out_shape=(jax.ShapeDtypeStruct((8, t), I32),
                   jax.ShapeDtypeStruct((t, LANES), F32),
                   jax.ShapeDtypeStruct((N_EXPERTS, LANES), I32)),
        grid=(t // TR_ROUTE,),
        in_specs=[pl.BlockSpec((2 * LANES, TR_ROUTE), lambda i: (0, i))],
        out_specs=(pl.BlockSpec((8, TR_ROUTE), lambda i: (0, i)),
                   pl.BlockSpec((TR_ROUTE, LANES), lambda i: (i, 0)),
                   pl.BlockSpec((N_EXPERTS, LANES), lambda i: (0, 0))),
        scratch_shapes=[pltpu.VMEM((N_EXPERTS, LANES), F32)],
        compiler_params=pltpu.CompilerParams(dimension_semantics=("arbitrary",)),
        name="moe_route",
    )(logits_t)


def _dest_kernel(ps_ref, idx_ref, dest_ref):
    dest = [idx_ref[2:3, :], idx_ref[3:4, :]]
    for e in range(N_EXPERTS):
        for k in range(2):
            dest[k] = dest[k] + jnp.where(idx_ref[k:k + 1, :] == e, ps_ref[e], 0)
    dest_ref[0:1, :] = dest[0]
    dest_ref[1:2, :] = dest[1]
    dest_ref[2:8, :] = jnp.zeros((6, dest_ref.shape[1]), I32)


def _dest(pad_starts, idx):
    t = idx.shape[1]
    return pl.pallas_call(
        _dest_kernel,
        out_shape=jax.ShapeDtypeStruct((8, t), I32),
        grid_spec=pltpu.PrefetchScalarGridSpec(
            num_scalar_prefetch=1,
            grid=(t // TR_ROUTE,),
            in_specs=[pl.BlockSpec((8, TR_ROUTE), lambda i, ps: (0, i))],
            out_specs=pl.BlockSpec((8, TR_ROUTE), lambda i, ps: (0, i)),
        ),
        compiler_params=pltpu.CompilerParams(dimension_semantics=("arbitrary",)),
        name="moe_dest",
    )(pad_starts, idx)


def _scatter_kernel(pv_ref, pe_ref, nu_ref, dest_ref, h_ref, xb_ref, zblk, sem, zsem):
    i = pl.program_id(0)
    ts = TS_SCATTER
    bm = BM_EXPERT
    nb = xb_ref.shape[0] // (bm * TOK_SUB)

    @pl.when(i == 0)
    def _():
        zblk[...] = jnp.zeros_like(zblk)

        def blk_copy(blk):
            start = pl.multiple_of(blk * (bm * TOK_SUB), bm * TOK_SUB)
            return pltpu.make_async_copy(zblk, xb_ref.at[pl.ds(start, bm * TOK_SUB)], zsem)

        def for_pad_blocks(fn):
            def per_expert(e, carry):
                @pl.when(pv_ref[e] < pe_ref[e])
                def _():
                    fn(blk_copy(lax.div(pe_ref[e], bm) - 1))
                return carry
            lax.fori_loop(0, N_EXPERTS, per_expert, 0)

            def per_block(blk, carry):
                fn(blk_copy(blk))
                return carry
            lax.fori_loop(nu_ref[0], nb, per_block, 0)

        for_pad_blocks(lambda cp: cp.start())
        for_pad_blocks(lambda cp: cp.wait())

    def issue(j, carry):
        for u in range(ISSUE_UNROLL):
            r = j * ISSUE_UNROLL + u
            for k in range(2):
                pltpu.make_async_copy(_token_tile(h_ref, r), _token_tile(xb_ref, dest_ref[k, r]),
                                      sem.at[k]).start(priority=k)
        return carry

    lax.fori_loop(0, ts // ISSUE_UNROLL, issue, 0)
    for k in range(2):
        pltpu.make_async_copy(h_ref, xb_ref.at[pl.ds(0, ts * TOK_SUB)], sem.at[k]).wait()


def _scatter(pad_valid_end, pad_ends, n_used, dest, h2t, n_rows):
    t = h2t.shape[0] // TOK_SUB
    return pl.pallas_call(
        _scatter_kernel,
        out_shape=jax.ShapeDtypeStruct((n_rows * TOK_SUB, LANES), U32),
        grid_spec=pltpu.PrefetchScalarGridSpec(
            num_scalar_prefetch=3,
            grid=(t // TS_SCATTER,),
            in_specs=[
                pl.BlockSpec((8, TS_SCATTER), lambda i, *_: (0, i), memory_space=pltpu.SMEM),
                pl.BlockSpec((TS_SCATTER * TOK_SUB, LANES), lambda i, *_: (i, 0)),
            ],
            out_specs=pl.BlockSpec(memory_space=pl.ANY),
            scratch_shapes=[pltpu.VMEM((BM_EXPERT * TOK_SUB, LANES), U32),
                            pltpu.SemaphoreType.DMA((2,)),
                            pltpu.SemaphoreType.DMA(())],
        ),
        compiler_params=pltpu.CompilerParams(dimension_semantics=("arbitrary",),
                                             disable_bounds_checks=True),
        name="moe_scatter",
    )(pad_valid_end, pad_ends, n_used, dest, h2t)


def _expert_kernel(be_ref, nu_ref, x_ref, w13_ref, w2_ref, y_ref, w13b, w2b):
    i = pl.program_id(0)

    @pl.when((i == 0) | (be_ref[i] != be_ref[jnp.maximum(i - 1, 0)]))
    def _():
        w13b[...] = w13_ref[0].astype(BF16)
        w2b[...] = w2_ref[0].astype(BF16)

    @pl.when(i < nu_ref[0])
    def _():
        x = _load_token_tiles(x_ref).astype(BF16)
        a = _dot(x, w13b[:, 0:EXPERT_FF])
        g = _dot(x, w13b[:, EXPERT_FF:2 * EXPERT_FF])
        _store_token_tiles(y_ref, _dot((_silu(a) * g).astype(BF16), w2b[...]))

    @pl.when(i >= nu_ref[0])
    def _():
        y_ref[...] = jnp.zeros_like(y_ref)


def _experts(block_expert, n_used, xbuf, w13, w2):
    nb = xbuf.shape[0] // (BM_EXPERT * TOK_SUB)
    used = lambda i, nu: jnp.minimum(i, nu[0] - 1)
    return pl.pallas_call(
        _expert_kernel,
        out_shape=jax.ShapeDtypeStruct(xbuf.shape, U32),
        grid_spec=pltpu.PrefetchScalarGridSpec(
            num_scalar_prefetch=2,
            grid=(nb,),
            in_specs=[
                pl.BlockSpec((BM_EXPERT * TOK_SUB, LANES), lambda i, be, nu: (used(i, nu), 0)),
                pl.BlockSpec((1, D_MODEL, 2 * EXPERT_FF), lambda i, be, nu: (be[i], 0, 0)),
                pl.BlockSpec((1, EXPERT_FF, D_MODEL), lambda i, be, nu: (be[i], 0, 0)),
            ],
            out_specs=pl.BlockSpec((BM_EXPERT * TOK_SUB, LANES), lambda i, be, nu: (i, 0)),
            scratch_shapes=[pltpu.VMEM((D_MODEL, 2 * EXPERT_FF), BF16),
                            pltpu.VMEM((EXPERT_FF, D_MODEL), BF16)],
        ),
        compiler_params=pltpu.CompilerParams(dimension_semantics=("arbitrary",),
                                             vmem_limit_bytes=VMEM_LIMIT),
        name="moe_experts",
    )(block_expert, n_used, xbuf, w13, w2)


def _combine_kernel(dest_ref, destn_ref, x_ref, rf_ref, mod_ref, fg_ref, yb_ref, out_ref, gbuf, sem):
    i = pl.program_id(0)
    n = pl.num_programs(0)
    tc = TC_COMBINE
    slot = lax.rem(i, 2)

    def start_gather(dest, s):
        def issue(j, carry):
            for u in range(ISSUE_UNROLL):
                r = j * ISSUE_UNROLL + u
                for k in range(2):
                    pltpu.make_async_copy(_token_tile(yb_ref, dest[k, r]), _token_tile(gbuf.at[s, k], r),
                                          sem.at[s, k]).start(priority=k)
            return carry
        lax.fori_loop(0, tc // ISSUE_UNROLL, issue, 0)

    @pl.when(i == 0)
    def _():
        start_gather(dest_ref, 0)

    @pl.when(i + 1 < n)
    def _():
        start_gather(destn_ref, 1 - slot)

    for k in range(2):
        pltpu.make_async_copy(yb_ref.at[pl.ds(0, tc * TOK_SUB)], gbuf.at[slot, k], sem.at[slot, k]).wait()
    moe = (rf_ref[:, 0:1] * _load_token_tiles(gbuf.at[slot, 0])
           + rf_ref[:, 1:2] * _load_token_tiles(gbuf.at[slot, 1]))
    out_ref[...] = _rmsnorm(x_ref[...] + mod_ref[0, 5:6, :] * moe, fg_ref[...])


def _combine(dest, x_new, rf, mod3, fg, ybuf, seq):
    t = x_new.shape[0]
    n = t // TC_COMBINE
    tiles_per_seq = seq // TC_COMBINE
    return pl.pallas_call(
        _combine_kernel,
        out_shape=jax.ShapeDtypeStruct((t, D_MODEL), F32),
        grid=(n,),
        in_specs=[
            pl.BlockSpec((8, TC_COMBINE), lambda i: (0, i), memory_space=pltpu.SMEM),
            pl.BlockSpec((8, TC_COMBINE), lambda i: (0, jnp.minimum(i + 1, n - 1)), memory_space=pltpu.SMEM),
            pl.BlockSpec((TC_COMBINE, D_MODEL), lambda i: (i, 0)),
            pl.BlockSpec((TC_COMBINE, LANES), lambda i: (i, 0)),
            pl.BlockSpec((1, 6, D_MODEL), lambda i: (i // tiles_per_seq, 0, 0)),
            pl.BlockSpec((1, D_MODEL), lambda i: (0, 0)),
            pl.BlockSpec(memory_space=pl.ANY),
        ],
        out_specs=pl.BlockSpec((TC_COMBINE, D_MODEL), lambda i: (i, 0)),
        scratch_shapes=[pltpu.VMEM((2, 2, TC_COMBINE * TOK_SUB, LANES), U32),
                        pltpu.SemaphoreType.DMA((2, 2))],
        compiler_params=pltpu.CompilerParams(dimension_semantics=("arbitrary",),
                                             disable_bounds_checks=True),
        name="moe_combine",
    )(dest, dest, x_new, rf, mod3, fg, ybuf)


def _pad_lanes(a, n=LANES):
    return jnp.pad(a, [(0, 0)] * (a.ndim - 1) + [(0, n - a.shape[-1])])


def _selection_matrices():
    q = SSD_CHUNK
    rel = jnp.arange(SEQ_HALO + q)[None, :] - jnp.arange(q)[:, None] - SEQ_HALO
    band = jnp.stack([(rel <= 0) & (rel > -w) for w in POOL_WINDOWS]).astype(BF16)
    expand = (jnp.arange(LANES)[:, None] == jnp.arange(SSD_INNER)[None, :] // SSD_HEAD_DIM).astype(BF16)
    return band, expand


def _layer(x2, c, bsz, seq, ada_w, ada_b, norm1_g, w_in, pool_w, pool_scale, conv_w, conv_b, dt_bias,
           a_log, d_skip, ssd_norm_g, w_branch_pool, w_branch_ssd, w_out, norm2_g, router_group_w,
           router_group_b, router_expert_w, router_expert_b, w13, w2):
    t = bsz * seq
    mod3 = _ada(c, ada_w, ada_b).reshape(bsz, 6, D_MODEL)

    i2 = POOL_WIDTH + SSD_INNER + SSD_CONV_DIM
    i3 = i2 + SSD_HEADS
    w_main = jnp.concatenate([w_in[:, :i2], w_in[:, i3:]], axis=1).astype(BF16)
    wd = _pad_lanes(w_in[:, i2:i3])
    wd_hi = wd.astype(BF16)
    wd_lo = (wd - wd_hi.astype(F32)).astype(BF16)

    wr = jnp.concatenate([_pad_lanes(router_group_w), _pad_lanes(router_expert_w)], axis=1).T
    wr_hi = wr.astype(BF16)
    wr_lo = (wr - wr_hi.astype(F32)).astype(BF16)
    br = jnp.concatenate([_pad_lanes(router_group_b.reshape(1, -1)),
                          _pad_lanes(router_expert_b.reshape(1, -1))], axis=1)
    br = jnp.broadcast_to(br.reshape(2 * LANES, 1), (2 * LANES, LANES))
    consts = [
        pool_w.astype(BF16),
        pool_scale.reshape(1, POOL_WIDTH),
        conv_w,
        conv_b.reshape(1, SSD_CONV_DIM),
        _pad_lanes(dt_bias.reshape(1, SSD_HEADS)),
        jnp.broadcast_to(jnp.pad(a_log, (0, LANES - SSD_HEADS))[:, None], (LANES, LANES)),
        jnp.repeat(d_skip, SSD_HEAD_DIM).reshape(1, SSD_INNER),
        ssd_norm_g.reshape(1, SSD_INNER),
        w_branch_pool.astype(BF16),
        w_branch_ssd.astype(BF16),
        w_out.astype(BF16),
        norm2_g.reshape(1, D_MODEL),
        wr_hi, wr_lo, br,
        *_selection_matrices(),
    ]
    x_new, h2, logits_t = _promix(x2, mod3, norm1_g.reshape(1, D_MODEL), w_main, wd_hi, wd_lo, consts, seq)
    idx, rf, cnt = _route(logits_t)

    counts = cnt[:, 0]
    padded = ((counts + BM_EXPERT - 1) // BM_EXPERT) * BM_EXPERT
    pad_ends = jnp.cumsum(padded)
    pad_starts = (pad_ends - padded).astype(I32)
    n_rows = 2 * t + N_EXPERTS * BM_EXPERT
    nb = n_rows // BM_EXPERT
    block_row0 = jnp.arange(nb, dtype=I32) * BM_EXPERT
    block_expert = jnp.minimum(jnp.sum((pad_ends[None, :] <= block_row0[:, None]).astype(I32), axis=1),
                               N_EXPERTS - 1)
    n_used = (pad_ends[-1:] // BM_EXPERT).astype(I32)

    dest = _dest(pad_starts, idx)
    xbuf = _scatter(pad_starts + counts, pad_ends.astype(I32), n_used, dest, h2, n_rows)
    ybuf = _experts(block_expert, n_used, xbuf, w13, w2)
    return dest, x_new, rf, mod3, ybuf


def kernel(x, c, ada_w, ada_b, norm1_g, w_in, pool_w, pool_scale, conv_w, conv_b, dt_bias, a_log, d_skip, ssd_norm_g, w_branch_pool, w_branch_ssd, w_out, norm2_g, router_group_w, router_group_b, router_expert_w, router_expert_b, w13, w2, final_norm_g):
    bsz, seq, d = x.shape
    depth = ada_w.shape[0]
    assert depth == 1 and d == D_MODEL and seq % TM_MIX == 0
    x2 = x.reshape(bsz * seq, d)
    l = 0
    dest, x_new, rf, mod3, ybuf = _layer(
        x2, c, bsz, seq, ada_w[l], ada_b[l], norm1_g[l], w_in[l], pool_w[l], pool_scale[l], conv_w[l],
        conv_b[l], dt_bias[l], a_log[l], d_skip[l], ssd_norm_g[l], w_branch_pool[l], w_branch_ssd[l],
        w_out[l], norm2_g[l], router_group_w[l], router_group_b[l], router_expert_w[l],
        router_expert_b[l], w13[l], w2[l])
    out = _combine(dest, x_new, rf, mod3, final_norm_g.reshape(1, d), ybuf, seq)
    return out.reshape(bsz, seq, d)
```

```python
import functools

import jax
import jax.numpy as jnp
from jax import lax
from jax.experimental import pallas as pl
from jax.experimental.pallas import tpu as pltpu

F32 = jnp.float32
BF16 = jnp.bfloat16
I32 = jnp.int32
U32 = jnp.uint32

D_MODEL = 1024
POOL_WIDTH = 1024
POOL_GROUPS = 4
POOL_GROUP_WIDTH = 256
POOL_WINDOWS = (2, 4, 8, 16)
SSD_INNER = 2048
SSD_HEAD_DIM = 64
SSD_HEADS = 32
SSD_GROUPS = 4
SSD_STATE = 128
SSD_CONV = 4
SSD_CHUNK = 128
SSD_CONV_DIM = SSD_INNER + 2 * SSD_GROUPS * SSD_STATE
N_EXPERTS = 32
EXPERTS_PER_GROUP = 8
MOE_GROUPS = 4
EXPERT_FF = 512
EPS = 1e-6

LANES = 128
TOK_SUB = D_MODEL // (2 * LANES)
MAIN_COLS = POOL_WIDTH + SSD_INNER + SSD_CONV_DIM + 2 * D_MODEL
COL_Z = POOL_WIDTH
COL_XBC = COL_Z + SSD_INNER
COL_GATE = COL_XBC + SSD_CONV_DIM

TM_MIX = 256
PROJ_PIECE = 256
SEQ_HALO = SSD_CHUNK
CONV_HALO = 8
CONV_SLABS = SSD_CONV_DIM // LANES
LOG2E = 1.4426950408889634
TR_ROUTE = 2048
ROUTE_CHUNK = 256
TS_SCATTER = 1024
ISSUE_UNROLL = 8
BM_EXPERT = 512
TC_COMBINE = 256
VMEM_LIMIT = 56 * 1024 * 1024
VMEM_LIMIT_PROMIX = 60 * 1024 * 1024


def _dot(a, b):
    return jnp.dot(a, b, preferred_element_type=F32)


def _hi_lo(x):
    hi = x.astype(BF16)
    lo = (x - hi.astype(F32)).astype(BF16)
    return hi, lo


def _split3(x):
    hi = x.astype(BF16)
    r = x - hi.astype(F32)
    mid = r.astype(BF16)
    lo = (r - mid.astype(F32)).astype(BF16)
    return hi, mid, lo


def _dot3(a, b):
    ah, al = _hi_lo(a)
    bh, bl = _hi_lo(b)
    return _dot(ah, bh) + _dot(ah, bl) + _dot(al, bh)


def _sigmoid(x):
    return 1.0 / (1.0 + jnp.exp(-x))


def _silu(x):
    return x * _sigmoid(x)


def _softplus(x):
    return jnp.maximum(x, 0.0) + jnp.log1p(jnp.exp(-jnp.abs(x)))


def _rmsnorm(x, g):
    return x * lax.rsqrt(jnp.mean(x * x, axis=-1, keepdims=True) + EPS) * g


def _bf16_bits(v):
    return lax.bitcast_convert_type(v.astype(BF16).astype(F32), U32) & jnp.uint32(0xFFFF0000)


def _store_token_tiles(ref, v):
    n = v.shape[0]
    half = v.shape[1] // 2
    words = _bf16_bits(v[:, half:]) | (_bf16_bits(v[:, :half]) >> 16)
    for j in range(TOK_SUB):
        ref[pl.ds(j, n, stride=TOK_SUB), :] = words[:, j * LANES:(j + 1) * LANES]


def _load_token_tiles(ref):
    n = ref.shape[0] // TOK_SUB
    words = jnp.concatenate([ref[pl.ds(j, n, stride=TOK_SUB), :] for j in range(TOK_SUB)], axis=1)
    lo = lax.bitcast_convert_type(words << 16, F32)
    hi = lax.bitcast_convert_type(words & jnp.uint32(0xFFFF0000), F32)
    return jnp.concatenate([lo, hi], axis=1)


def _token_tile(ref, row):
    return ref.at[pl.ds(pl.multiple_of(row * TOK_SUB, TOK_SUB), TOK_SUB)]


def _token_tile_at(ref, first_row):
    return ref.at[pl.ds(pl.multiple_of(first_row, TOK_SUB), TOK_SUB)]


def _ada_kernel(c_ref, w_ref, b_ref, o_ref):
    o_ref[...] = _dot3(_silu(c_ref[...]), w_ref[...]) + b_ref[...]


def _ada(c, w, b):
    bsz = c.shape[0]
    n = w.shape[1]
    tn = 1024
    return pl.pallas_call(
        _ada_kernel,
        out_shape=jax.ShapeDtypeStruct((bsz, n), F32),
        grid=(n // tn,),
        in_specs=[
            pl.BlockSpec((bsz, D_MODEL), lambda j: (0, 0)),
            pl.BlockSpec((D_MODEL, tn), lambda j: (0, j)),
            pl.BlockSpec((1, tn), lambda j: (0, j)),
        ],
        out_specs=pl.BlockSpec((bsz, tn), lambda j: (0, j)),
        compiler_params=pltpu.CompilerParams(dimension_semantics=("arbitrary",)),
        name="ada",
    )(c, w, b.reshape(1, n))


N_MIX_CONSTS = 17


def _promix_kernel(tiles_per_seq, xa_ref, moda_ref, g_ref, w_ref, wdh_ref, wdl_ref, xb_ref, modb_ref, *rest):
    consts = rest[:N_MIX_CONSTS]
    xnew_ref, h2_ref, lt_ref = rest[N_MIX_CONSTS:N_MIX_CONSTS + 3]
    pbuf, dtbuf, zg, dts, xp_ext, xbc_ext, act, state, yssd = rest[N_MIX_CONSTS + 3:]
    s = pl.program_id(0)
    j = lax.rem(jnp.maximum(s - 1, 0), tiles_per_seq)
    tm = TM_MIX

    @pl.when(s == 0)
    def _():
        pbuf[...] = jnp.zeros_like(pbuf)
        dtbuf[...] = jnp.zeros_like(dtbuf)

    @pl.when(j == 0)
    def _():
        xp_ext[0:SEQ_HALO, :] = jnp.zeros((SEQ_HALO, POOL_WIDTH), BF16)
        xbc_ext[:, 0:CONV_HALO, :] = jnp.zeros((CONV_SLABS, CONV_HALO, LANES), F32)
        state[...] = jnp.zeros_like(state)

    xp_ext[SEQ_HALO:SEQ_HALO + tm, :] = pbuf[:, 0:POOL_WIDTH]
    for sl in range(CONV_SLABS):
        xbc_ext[sl, CONV_HALO:CONV_HALO + tm, :] = (
            pbuf[:, COL_XBC + sl * LANES:COL_XBC + (sl + 1) * LANES].astype(F32))
    zg[:, 0:SSD_INNER] = pbuf[:, COL_Z:COL_Z + SSD_INNER]
    zg[:, SSD_INNER:SSD_INNER + 2 * D_MODEL] = pbuf[:, COL_GATE:COL_GATE + 2 * D_MODEL]
    dts[...] = dtbuf[...]

    h = _rmsnorm(xa_ref[...], g_ref[...]) * (1.0 + moda_ref[0, 1:2, :]) + moda_ref[0, 0:1, :]
    hb, hl = _hi_lo(h)

    def project(c0):
        def piece():
            pbuf[:, c0:c0 + PROJ_PIECE] = _dot(hb, w_ref[:, c0:c0 + PROJ_PIECE]).astype(BF16)
        return piece

    def project_dt():
        dtbuf[...] = _dot(hb, wdh_ref[...]) + _dot(hb, wdl_ref[...]) + _dot(hl, wdh_ref[...])

    fill = _Filler([project(c0) for c0 in range(0, MAIN_COLS, PROJ_PIECE)] + [project_dt])
    _mix_main(fill, j, zg, dts, xb_ref, modb_ref, *consts,
              xnew_ref, h2_ref, lt_ref, xp_ext, xbc_ext, act, state, yssd)


class _Filler:
    def __init__(self, thunks):
        self.pending = list(thunks)

    def __call__(self, n):
        n = len(self.pending) if n is None else n
        for _ in range(min(n, len(self.pending))):
            self.pending.pop(0)()


def _mix_main(fill, j, zg_ref, dt_ref, x_ref, mod_ref, poolw_ref, pools_ref, convw_ref, convb_ref,
              dtb_ref, alog_ref, dskip_ref, ng_ref, wbp_ref, wbs_ref, wout_ref, n2g_ref,
              wrh_ref, wrl_ref, br_ref, band_ref, expand_ref,
              xnew_ref, h2_ref, lt_ref,
              xp_ext, xbc_ext, act, state, yssd):
    tm = TM_MIX
    q = SSD_CHUNK
    halo = SEQ_HALO


    gw = POOL_GROUP_WIDTH
    pos = j * tm + lax.broadcasted_iota(I32, (tm, gw), 0)
    yp = []
    for gi, w in enumerate(POOL_WINDOWS):
        cols = slice(gi * gw, (gi + 1) * gw)
        acc = jnp.concatenate([_dot(band_ref[gi], xp_ext[r0:r0 + halo + q, cols])
                               for r0 in range(0, tm, q)], axis=0)
        cur = xp_ext[halo:halo + tm, cols].astype(F32)
        d = acc / jnp.minimum(pos + 1, w).astype(F32) - cur
        yp.append(_dot(d.astype(BF16), poolw_ref[gi]) * pools_ref[:, cols])
        fill(1)
    bp = _dot(jnp.concatenate(yp, axis=1).astype(BF16), wbp_ref[...])

    first = CONV_HALO - (SSD_CONV - 1)
    for sl in range(CONV_SLABS):
        cols = slice(sl * LANES, (sl + 1) * LANES)
        for par in range(2):
            acc = convb_ref[:, cols] + convw_ref[0:1, cols] * xbc_ext[sl, pl.ds(first + par, tm // 2, stride=2), :]
            for k in range(1, SSD_CONV):
                acc = acc + convw_ref[k:k + 1, cols] * xbc_ext[sl, pl.ds(first + par + k, tm // 2, stride=2), :]
            act[sl, pl.ds(par, tm // 2, stride=2), :] = _silu(acc)
        if sl % 2 == 1:
            fill(1)

    xp_ext[0:halo, :] = xp_ext[tm:tm + halo, :]
    xbc_ext[:, 0:CONV_HALO, :] = xbc_ext[:, tm:tm + CONV_HALO, :]

    lane = lax.broadcasted_iota(I32, (q, q), 1)
    row = lax.broadcasted_iota(I32, (q, q), 0)
    causal = row >= lane
    triu = jnp.where(row <= lane, 1.0, 0.0).astype(BF16)
    lo_half = lane < SSD_HEAD_DIM
    a_rows = -jnp.exp(alog_ref[...])
    col_b = SSD_INNER
    col_c = SSD_INNER + SSD_GROUPS * SSD_STATE
    gcw = SSD_INNER // SSD_GROUPS
    for c in range(tm // q):
        r0 = c * q
        dtv = _softplus(dt_ref[r0:r0 + q, :] + dtb_ref[...])
        dt_t = dtv.T
        p1, p2, p3 = _split3(dt_t * a_rows)
        acs_t = _dot(p1, triu) + _dot(p2, triu) + _dot(p3, triu)
        acs = acs_t.T
        acs2 = acs * LOG2E
        acs2_t = acs_t * LOG2E
        expand = expand_ref[...]
        dt_x = _dot(jnp.concatenate(_hi_lo(dtv), axis=1), jnp.concatenate([expand, expand], axis=0))
        fx = _dot(jnp.concatenate([jnp.exp(acs).astype(BF16),
                                   jnp.exp(acs[q - 1:q, :] - acs).astype(BF16)], axis=0), expand)
        ea_x = fx[0:q]
        dec_x = fx[q:2 * q]
        xdt = jnp.concatenate([act[sl, r0:r0 + q, :] for sl in range(SSD_INNER // LANES)], axis=1) * dt_x
        xb = xdt.astype(BF16)
        xdb = (xdt * dec_x).astype(BF16)
        s_all = state[...]
        sb = s_all.astype(BF16)
        for g in range(SSD_GROUPS):
            gcols = slice(g * gcw, (g + 1) * gcw)
            bg = act[col_b // LANES + g, r0:r0 + q, :]
            cg = act[col_c // LANES + g, r0:r0 + q, :].astype(BF16)
            cbm = lax.dot_general(cg, bg.astype(BF16),
                                  (((1,), (1,)), ((), ())), preferred_element_type=F32)
            y_off = _dot(cg, sb[:, gcols]) * ea_x[:, gcols]
            ds = _dot(bg.T.astype(BF16), xdb[:, gcols])
            state[:, gcols] = s_all[:, gcols] * ea_x[q - 1:q, gcols] + ds
            y_diag = []
            for pq in range(gcw // LANES):
                pair = g * (gcw // LANES) + pq
                ms = []
                for hh in range(2):
                    h = 2 * pair + hh
                    a_l = jnp.broadcast_to(acs2[:, h:h + 1], (q, q))
                    a_s = jnp.broadcast_to(acs2_t[h:h + 1, :], (q, q))
                    ms.append((jnp.where(causal, jnp.exp2(a_l - a_s), 0.0) * cbm).astype(BF16))
                x_pair = xb[:, pair * LANES:(pair + 1) * LANES]
                zero = jnp.zeros_like(x_pair)
                y_diag.append(_dot(jnp.concatenate(ms, axis=1),
                                   jnp.concatenate([jnp.where(lo_half, x_pair, zero),
                                                    jnp.where(lo_half, zero, x_pair)], axis=0)))
                fill(1 if pq % 2 == 0 else 0)
            yssd[r0:r0 + q, gcols] = jnp.concatenate(y_diag, axis=1) + y_off

    y = yssd[...] + dskip_ref[...] * jnp.concatenate([act[sl] for sl in range(SSD_INNER // LANES)], axis=1)
    y = y * _silu(zg_ref[:, 0:SSD_INNER].astype(F32))
    fill(1)
    y = _rmsnorm(y, ng_ref[...])
    fill(1)
    bs = _dot(y.astype(BF16), wbs_ref[...])
    g_pool = _sigmoid(zg_ref[:, SSD_INNER:SSD_INNER + D_MODEL].astype(F32))
    g_ssd = _sigmoid(zg_ref[:, SSD_INNER + D_MODEL:SSD_INNER + 2 * D_MODEL].astype(F32))
    fill(1)
    u = g_pool * bp + g_ssd * bs
    x_new = x_ref[...] + mod_ref[0, 2:3, :] * _dot(u.astype(BF16), wout_ref[...])
    xnew_ref[...] = x_new
    fill(None)

    h2 = _rmsnorm(x_new, n2g_ref[...]) * (1.0 + mod_ref[0, 4:5, :]) + mod_ref[0, 3:4, :]
    _store_token_tiles(h2_ref, h2)
    hh, hl = _hi_lo(h2)
    nt = (((1,), (1,)), ((), ()))
    logits_t = (lax.dot_general(wrh_ref[...], hh, nt, preferred_element_type=F32)
                + lax.dot_general(wrl_ref[...], hh, nt, preferred_element_type=F32)
                + lax.dot_general(wrh_ref[...], hl, nt, preferred_element_type=F32))
    lt_ref[...] = logits_t + jnp.concatenate([br_ref[...]] * (tm // LANES), axis=1)


def _promix(x2, mod3, g1, w_main, wd_hi, wd_lo, consts, seq):
    assert len(consts) == N_MIX_CONSTS
    t = x2.shape[0]
    n = t // TM_MIX
    tiles = seq // TM_MIX
    nxt = lambda s: jnp.minimum(s, n - 1)
    cur = lambda s: jnp.maximum(s - 1, 0)

    def cspec(a):
        zeros = (0,) * a.ndim
        return pl.BlockSpec(a.shape, lambda s: zeros, pipeline_mode=pl.Buffered(1))

    return pl.pallas_call(
        functools.partial(_promix_kernel, tiles),
        out_shape=(jax.ShapeDtypeStruct((t, D_MODEL), F32),
                   jax.ShapeDtypeStruct((t * TOK_SUB, LANES), U32),
                   jax.ShapeDtypeStruct((2 * LANES, t), F32)),
        grid=(n + 1,),
        in_specs=[
            pl.BlockSpec((TM_MIX, D_MODEL), lambda s: (nxt(s), 0)),
            pl.BlockSpec((1, 6, D_MODEL), lambda s: (nxt(s) // tiles, 0, 0)),
            cspec(g1), cspec(w_main), cspec(wd_hi), cspec(wd_lo),
            pl.BlockSpec((TM_MIX, D_MODEL), lambda s: (cur(s), 0)),
            pl.BlockSpec((1, 6, D_MODEL), lambda s: (cur(s) // tiles, 0, 0)),
        ] + [cspec(a) for a in consts],
        out_specs=(pl.BlockSpec((TM_MIX, D_MODEL), lambda s: (cur(s), 0)),
                   pl.BlockSpec((TM_MIX * TOK_SUB, LANES), lambda s: (cur(s), 0)),
                   pl.BlockSpec((2 * LANES, TM_MIX), lambda s: (0, cur(s)))),
        scratch_shapes=[
            pltpu.VMEM((TM_MIX, MAIN_COLS), BF16),
            pltpu.VMEM((TM_MIX, LANES), F32),
            pltpu.VMEM((TM_MIX, SSD_INNER + 2 * D_MODEL), BF16),
            pltpu.VMEM((TM_MIX, LANES), F32),
            pltpu.VMEM((SEQ_HALO + TM_MIX, POOL_WIDTH), BF16),
            pltpu.VMEM((CONV_SLABS, CONV_HALO + TM_MIX, LANES), F32),
            pltpu.VMEM((CONV_SLABS, TM_MIX, LANES), F32),
            pltpu.VMEM((SSD_STATE, SSD_INNER), F32),
            pltpu.VMEM((TM_MIX, SSD_INNER), F32),
        ],
        compiler_params=pltpu.CompilerParams(dimension_semantics=("arbitrary",),
                                             vmem_limit_bytes=VMEM_LIMIT_PROMIX),
        name="promix",
    )(x2, mod3, g1, w_main, wd_hi, wd_lo, x2, mod3, *consts)


def _route_kernel(lt_ref, idx_ref, rf_ref, cnt_ref, carry):
    i = pl.program_id(0)
    tr = TR_ROUTE
    ck = ROUTE_CHUNK
    neg = -jnp.inf

    @pl.when(i == 0)
    def _():
        carry[...] = jnp.zeros_like(carry)

    row8 = lax.broadcasted_iota(I32, (8, tr), 0)
    lg = jnp.where(row8 < MOE_GROUPS, lt_ref[0:8, :], neg)
    mg = jnp.max(lg, axis=0, keepdims=True)
    gsel = jnp.min(jnp.where(lg == mg, row8, 8), axis=0, keepdims=True)
    gwt = 1.0 / jnp.sum(jnp.exp(lg - mg), axis=0, keepdims=True)
    rowe = lax.broadcasted_iota(I32, (N_EXPERTS, tr), 0)
    le = jnp.where(lax.shift_right_logical(rowe, 3) == gsel, lt_ref[LANES:LANES + N_EXPERTS, :], neg)
    m1 = jnp.max(le, axis=0, keepdims=True)
    e0 = jnp.min(jnp.where(le == m1, rowe, N_EXPERTS), axis=0, keepdims=True)
    le2 = jnp.where(rowe == e0, neg, le)
    m2 = jnp.max(le2, axis=0, keepdims=True)
    e1 = jnp.min(jnp.where(le2 == m2, rowe, N_EXPERTS), axis=0, keepdims=True)
    t21 = jnp.exp(m2 - m1)
    w0 = gwt / (1.0 + t21)
    w1 = gwt * t21 / (1.0 + t21)
    oh0 = jnp.where(rowe == e0, 1.0, 0.0)
    oh1 = jnp.where(rowe == e1, 1.0, 0.0)
    oh = oh0 + oh1

    kr = lax.broadcasted_iota(I32, (ck, ck), 0)
    kc = lax.broadcasted_iota(I32, (ck, ck), 1)
    before = jnp.where(kr < kc, 1.0, 0.0).astype(BF16)
    wpad = jnp.zeros((LANES - 2, ck), F32)
    for c0 in range(0, tr, ck):
        cs = slice(c0, c0 + ck)
        pre = _dot(oh[:, cs].astype(BF16), before) + carry[:, 0:1]
        idx_ref[2:3, cs] = jnp.sum(pre * oh0[:, cs], axis=0, keepdims=True).astype(I32)
        idx_ref[3:4, cs] = jnp.sum(pre * oh1[:, cs], axis=0, keepdims=True).astype(I32)
        carry[...] = carry[...] + jnp.sum(oh[:, cs], axis=1, keepdims=True)
        rf_ref[cs, :] = jnp.concatenate([w0[:, cs], w1[:, cs], wpad], axis=0).T
    idx_ref[0:1, :] = e0
    idx_ref[1:2, :] = e1
    idx_ref[4:8, :] = jnp.zeros((4, tr), I32)
    cnt_ref[...] = carry[...].astype(I32)


def _route(logits_t):
    t = logits_t.shape[1]
    return pl.pallas_call(
        _route_kernel,
        out_shape=(jax.ShapeDtypeStruct((8, t), I32),
                   jax.ShapeDtypeStruct((t, LANES), F32),
                   jax.ShapeDtypeStruct((N_EXPERTS, LANES), I32)),
        grid=(t // TR_ROUTE,),
        in_specs=[pl.BlockSpec((2 * LANES, TR_ROUTE), lambda i: (0, i))],
        out_specs=(pl.BlockSpec((8, TR_ROUTE), lambda i: (0, i)),
                   pl.BlockSpec((TR_ROUTE, LANES), lambda i: (i, 0)),
                   pl.BlockSpec((N_EXPERTS, LANES), lambda i: (0, 0))),
        scratch_shapes=[pltpu.VMEM((N_EXPERTS, LANES), F32)],
        compiler_params=pltpu.CompilerParams(dimension_semantics=("arbitrary",)),
        name="moe_route",
    )(logits_t)


def _dest_kernel(ps_ref, idx_ref, dest_ref):
    dest = [idx_ref[2:3, :], idx_ref[3:4, :]]
    for e in range(N_EXPERTS):
        for k in range(2):
            dest[k] = dest[k] + jnp.where(idx_ref[k:k + 1, :] == e, ps_ref[e], 0)
    dest_ref[0:1, :] = dest[0] * TOK_SUB
    dest_ref[1:2, :] = dest[1] * TOK_SUB
    dest_ref[2:8, :] = jnp.zeros((6, dest_ref.shape[1]), I32)


def _dest(pad_starts, idx):
    t = idx.shape[1]
    return pl.pallas_call(
        _dest_kernel,
        out_shape=jax.ShapeDtypeStruct((8, t), I32),
        grid_spec=pltpu.PrefetchScalarGridSpec(
            num_scalar_prefetch=1,
            grid=(t // TR_ROUTE,),
            in_specs=[pl.BlockSpec((8, TR_ROUTE), lambda i, ps: (0, i))],
            out_specs=pl.BlockSpec((8, TR_ROUTE), lambda i, ps: (0, i)),
        ),
        compiler_params=pltpu.CompilerParams(dimension_semantics=("arbitrary",)),
        name="moe_dest",
    )(pad_starts, idx)


def _scatter_kernel(pv_ref, pe_ref, nu_ref, d0_ref, d1_ref, h_ref, xb_ref, zblk, sem, zsem):
    i = pl.program_id(0)
    ts = TS_SCATTER
    bm = BM_EXPERT
    nb = xb_ref.shape[0] // (bm * TOK_SUB)

    @pl.when(i == 0)
    def _():
        zblk[...] = jnp.zeros_like(zblk)

        def blk_copy(blk):
            start = pl.multiple_of(blk * (bm * TOK_SUB), bm * TOK_SUB)
            return pltpu.make_async_copy(zblk, xb_ref.at[pl.ds(start, bm * TOK_SUB)], zsem)

        def for_pad_blocks(fn):
            def per_expert(e, carry):
                @pl.when(pv_ref[e] < pe_ref[e])
                def _():
                    fn(blk_copy(lax.div(pe_ref[e], bm) - 1))
                return carry
            lax.fori_loop(0, N_EXPERTS, per_expert, 0)

            def per_block(blk, carry):
                fn(blk_copy(blk))
                return carry
            lax.fori_loop(nu_ref[0], nb, per_block, 0)

        for_pad_blocks(lambda cp: cp.start())
        for_pad_blocks(lambda cp: cp.wait())

    def issue(j, carry):
        for u in range(ISSUE_UNROLL):
            r = j * ISSUE_UNROLL + u
            for k in range(2):
                pltpu.make_async_copy(_token_tile(h_ref, r), _token_tile_at(xb_ref, (d0_ref, d1_ref)[k][i * ts + r]),
                                      sem.at[k]).start(priority=k)
        return carry

    lax.fori_loop(0, ts // ISSUE_UNROLL, issue, 0)
    for k in range(2):
        pltpu.make_async_copy(h_ref, xb_ref.at[pl.ds(0, ts * TOK_SUB)], sem.at[k]).wait()


def _scatter(pad_valid_end, pad_ends, n_used, dest0, dest1, h2t, n_rows):
    t = h2t.shape[0] // TOK_SUB
    return pl.pallas_call(
        _scatter_kernel,
        out_shape=jax.ShapeDtypeStruct((n_rows * TOK_SUB, LANES), U32),
        grid_spec=pltpu.PrefetchScalarGridSpec(
            num_scalar_prefetch=5,
            grid=(t // TS_SCATTER,),
            in_specs=[pl.BlockSpec((TS_SCATTER * TOK_SUB, LANES), lambda i, *_: (i, 0))],
            out_specs=pl.BlockSpec(memory_space=pl.ANY),
            scratch_shapes=[pltpu.VMEM((BM_EXPERT * TOK_SUB, LANES), U32),
                            pltpu.SemaphoreType.DMA((2,)),
                            pltpu.SemaphoreType.DMA(())],
        ),
        compiler_params=pltpu.CompilerParams(dimension_semantics=("arbitrary",),
                                             disable_bounds_checks=True),
        name="moe_scatter",
    )(pad_valid_end, pad_ends, n_used, dest0, dest1, h2t)


def _expert_kernel(be_ref, nu_ref, x_ref, w13_ref, w2_ref, y_ref, w13b, w2b):
    i = pl.program_id(0)

    @pl.when((i == 0) | (be_ref[i] != be_ref[jnp.maximum(i - 1, 0)]))
    def _():
        w13b[...] = w13_ref[0].astype(BF16)
        w2b[...] = w2_ref[0].astype(BF16)

    @pl.when(i < nu_ref[0])
    def _():
        x = _load_token_tiles(x_ref).astype(BF16)
        a = _dot(x, w13b[:, 0:EXPERT_FF])
        g = _dot(x, w13b[:, EXPERT_FF:2 * EXPERT_FF])
        _store_token_tiles(y_ref, _dot((_silu(a) * g).astype(BF16), w2b[...]))

    @pl.when(i >= nu_ref[0])
    def _():
        y_ref[...] = jnp.zeros_like(y_ref)


def _experts(block_expert, n_used, xbuf, w13, w2):
    nb = xbuf.shape[0] // (BM_EXPERT * TOK_SUB)
    used = lambda i, nu: jnp.minimum(i, nu[0] - 1)
    return pl.pallas_call(
        _expert_kernel,
        out_shape=jax.ShapeDtypeStruct(xbuf.shape, U32),
        grid_spec=pltpu.PrefetchScalarGridSpec(
            num_scalar_prefetch=2,
            grid=(nb,),
            in_specs=[
                pl.BlockSpec((BM_EXPERT * TOK_SUB, LANES), lambda i, be, nu: (used(i, nu), 0)),
                pl.BlockSpec((1, D_MODEL, 2 * EXPERT_FF), lambda i, be, nu: (be[i], 0, 0)),
                pl.BlockSpec((1, EXPERT_FF, D_MODEL), lambda i, be, nu: (be[i], 0, 0)),
            ],
            out_specs=pl.BlockSpec((BM_EXPERT * TOK_SUB, LANES), lambda i, be, nu: (i, 0)),
            scratch_shapes=[pltpu.VMEM((D_MODEL, 2 * EXPERT_FF), BF16),
                            pltpu.VMEM((EXPERT_FF, D_MODEL), BF16)],
        ),
        compiler_params=pltpu.CompilerParams(dimension_semantics=("arbitrary",),
                                             vmem_limit_bytes=VMEM_LIMIT),
        name="moe_experts",
    )(block_expert, n_used, xbuf, w13, w2)


def _combine_kernel(d0_ref, d1_ref, x_ref, rf_ref, mod_ref, fg_ref, yb_ref, out_ref, gbuf, sem):
    i = pl.program_id(0)
    n = pl.num_programs(0)
    tc = TC_COMBINE
    slot = lax.rem(i, 2)

    def start_gather(tile, s):
        def issue(j, carry):
            for u in range(ISSUE_UNROLL):
                r = j * ISSUE_UNROLL + u
                for k in range(2):
                    src = (d0_ref, d1_ref)[k][tile * tc + r]
                    pltpu.make_async_copy(_token_tile_at(yb_ref, src), _token_tile(gbuf.at[s, k], r),
                                          sem.at[s, k]).start(priority=k)
            return carry
        lax.fori_loop(0, tc // ISSUE_UNROLL, issue, 0)

    @pl.when(i == 0)
    def _():
        start_gather(0, 0)

    @pl.when(i + 1 < n)
    def _():
        start_gather(i + 1, 1 - slot)

    for k in range(2):
        pltpu.make_async_copy(yb_ref.at[pl.ds(0, tc * TOK_SUB)], gbuf.at[slot, k], sem.at[slot, k]).wait()
    moe = (rf_ref[:, 0:1] * _load_token_tiles(gbuf.at[slot, 0])
           + rf_ref[:, 1:2] * _load_token_tiles(gbuf.at[slot, 1]))
    out_ref[...] = _rmsnorm(x_ref[...] + mod_ref[0, 5:6, :] * moe, fg_ref[...])


def _combine(dest0, dest1, x_new, rf, mod3, fg, ybuf, seq):
    t = x_new.shape[0]
    n = t // TC_COMBINE
    tiles_per_seq = seq // TC_COMBINE
    return pl.pallas_call(
        _combine_kernel,
        out_shape=jax.ShapeDtypeStruct((t, D_MODEL), F32),
        grid_spec=pltpu.PrefetchScalarGridSpec(
            num_scalar_prefetch=2,
            grid=(n,),
            in_specs=[
                pl.BlockSpec((TC_COMBINE, D_MODEL), lambda i, *_: (i, 0)),
                pl.BlockSpec((TC_COMBINE, LANES), lambda i, *_: (i, 0)),
                pl.BlockSpec((1, 6, D_MODEL), lambda i, *_: (i // tiles_per_seq, 0, 0)),
                pl.BlockSpec((1, D_MODEL), lambda i, *_: (0, 0)),
                pl.BlockSpec(memory_space=pl.ANY),
            ],
            out_specs=pl.BlockSpec((TC_COMBINE, D_MODEL), lambda i, *_: (i, 0)),
            scratch_shapes=[pltpu.VMEM((2, 2, TC_COMBINE * TOK_SUB, LANES), U32),
                            pltpu.SemaphoreType.DMA((2, 2))],
        ),
        compiler_params=pltpu.CompilerParams(dimension_semantics=("arbitrary",),
                                             disable_bounds_checks=True),
        name="moe_combine",
    )(dest0, dest1, x_new, rf, mod3, fg, ybuf)


def _pad_lanes(a, n=LANES):
    return jnp.pad(a, [(0, 0)] * (a.ndim - 1) + [(0, n - a.shape[-1])])


def _selection_matrices():
    q = SSD_CHUNK
    rel = jnp.arange(SEQ_HALO + q)[None, :] - jnp.arange(q)[:, None] - SEQ_HALO
    band = jnp.stack([(rel <= 0) & (rel > -w) for w in POOL_WINDOWS]).astype(BF16)
    expand = (jnp.arange(LANES)[:, None] == jnp.arange(SSD_INNER)[None, :] // SSD_HEAD_DIM).astype(BF16)
    return band, expand


def _layer(x2, c, bsz, seq, ada_w, ada_b, norm1_g, w_in, pool_w, pool_scale, conv_w, conv_b, dt_bias,
           a_log, d_skip, ssd_norm_g, w_branch_pool, w_branch_ssd, w_out, norm2_g, router_group_w,
           router_group_b, router_expert_w, router_expert_b, w13, w2):
    t = bsz * seq
    mod3 = _ada(c, ada_w, ada_b).reshape(bsz, 6, D_MODEL)

    i2 = POOL_WIDTH + SSD_INNER + SSD_CONV_DIM
    i3 = i2 + SSD_HEADS
    w_main = jnp.concatenate([w_in[:, :i2], w_in[:, i3:]], axis=1).astype(BF16)
    wd = _pad_lanes(w_in[:, i2:i3])
    wd_hi = wd.astype(BF16)
    wd_lo = (wd - wd_hi.astype(F32)).astype(BF16)

    wr = jnp.concatenate([_pad_lanes(router_group_w), _pad_lanes(router_expert_w)], axis=1).T
    wr_hi = wr.astype(BF16)
    wr_lo = (wr - wr_hi.astype(F32)).astype(BF16)
    br = jnp.concatenate([_pad_lanes(router_group_b.reshape(1, -1)),
                          _pad_lanes(router_expert_b.reshape(1, -1))], axis=1)
    br = jnp.broadcast_to(br.reshape(2 * LANES, 1), (2 * LANES, LANES))
    consts = [
        pool_w.astype(BF16),
        pool_scale.reshape(1, POOL_WIDTH),
        conv_w,
        conv_b.reshape(1, SSD_CONV_DIM),
        _pad_lanes(dt_bias.reshape(1, SSD_HEADS)),
        jnp.broadcast_to(jnp.pad(a_log, (0, LANES - SSD_HEADS))[:, None], (LANES, LANES)),
        jnp.repeat(d_skip, SSD_HEAD_DIM).reshape(1, SSD_INNER),
        ssd_norm_g.reshape(1, SSD_INNER),
        w_branch_pool.astype(BF16),
        w_branch_ssd.astype(BF16),
        w_out.astype(BF16),
        norm2_g.reshape(1, D_MODEL),
        wr_hi, wr_lo, br,
        *_selection_matrices(),
    ]
    x_new, h2, logits_t = _promix(x2, mod3, norm1_g.reshape(1, D_MODEL), w_main, wd_hi, wd_lo, consts, seq)
    idx, rf, cnt = _route(logits_t)

    counts = cnt[:, 0]
    padded = ((counts + BM_EXPERT - 1) // BM_EXPERT) * BM_EXPERT
    pad_ends = jnp.cumsum(padded)
    pad_starts = (pad_ends - padded).astype(I32)
    n_rows = 2 * t + N_EXPERTS * BM_EXPERT
    nb = n_rows // BM_EXPERT
    block_row0 = jnp.arange(nb, dtype=I32) * BM_EXPERT
    block_expert = jnp.minimum(jnp.sum((pad_ends[None, :] <= block_row0[:, None]).astype(I32), axis=1),
                               N_EXPERTS - 1)
    n_used = (pad_ends[-1:] // BM_EXPERT).astype(I32)

    dest = _dest(pad_starts, idx)
    dest0, dest1 = dest[0], dest[1]
    xbuf = _scatter(pad_starts + counts, pad_ends.astype(I32), n_used, dest0, dest1, h2, n_rows)
    ybuf = _experts(block_expert, n_used, xbuf, w13, w2)
    return dest0, dest1, x_new, rf, mod3, ybuf


def kernel(x, c, ada_w, ada_b, norm1_g, w_in, pool_w, pool_scale, conv_w, conv_b, dt_bias, a_log, d_skip, ssd_norm_g, w_branch_pool, w_branch_ssd, w_out, norm2_g, router_group_w, router_group_b, router_expert_w, router_expert_b, w13, w2, final_norm_g):
    bsz, seq, d = x.shape
    depth = ada_w.shape[0]
    assert depth == 1 and d == D_MODEL and seq % TM_MIX == 0
    x2 = x.reshape(bsz * seq, d)
    l = 0
    dest0, dest1, x_new, rf, mod3, ybuf = _layer(
        x2, c, bsz, seq, ada_w[l], ada_b[l], norm1_g[l], w_in[l], pool_w[l], pool_scale[l], conv_w[l],
        conv_b[l], dt_bias[l], a_log[l], d_skip[l], ssd_norm_g[l], w_branch_pool[l], w_branch_ssd[l],
        w_out[l], norm2_g[l], router_group_w[l], router_group_b[l], router_expert_w[l],
        router_expert_b[l], w13[l], w2[l])
    out = _combine(dest0, dest1, x_new, rf, mod3, final_norm_g.reshape(1, d), ybuf, seq)
    return out.reshape(bsz, seq, d)
```

```python
import functools

import jax
import jax.numpy as jnp
from jax import lax
from jax.experimental import pallas as pl
from jax.experimental.pallas import tpu as pltpu

F32 = jnp.float32
BF16 = jnp.bfloat16
I32 = jnp.int32
U32 = jnp.uint32

D_MODEL = 1024
POOL_WIDTH = 1024
POOL_GROUPS = 4
POOL_GROUP_WIDTH = 256
POOL_WINDOWS = (2, 4, 8, 16)
SSD_INNER = 2048
SSD_HEAD_DIM = 64
SSD_HEADS = 32
SSD_GROUPS = 4
SSD_STATE = 128
SSD_CONV = 4
SSD_CHUNK = 128
SSD_CONV_DIM = SSD_INNER + 2 * SSD_GROUPS * SSD_STATE
N_EXPERTS = 32
EXPERTS_PER_GROUP = 8
MOE_GROUPS = 4
EXPERT_FF = 512
EPS = 1e-6

LANES = 128
TOK_SUB = D_MODEL // (2 * LANES)
MAIN_COLS = POOL_WIDTH + SSD_INNER + SSD_CONV_DIM + 2 * D_MODEL
COL_Z = POOL_WIDTH
COL_XBC = COL_Z + SSD_INNER
COL_GATE = COL_XBC + SSD_CONV_DIM

TM_MIX = 256
PROJ_PIECE = 256
SEQ_HALO = SSD_CHUNK
CONV_HALO = 8
CONV_SLABS = SSD_CONV_DIM // LANES
LOG2E = 1.4426950408889634
TR_ROUTE = 2048
ROUTE_CHUNK = 256
TS_SCATTER = 1024
ISSUE_UNROLL = 16
BM_EXPERT = 512
TC_COMBINE = 256
VMEM_LIMIT = 56 * 1024 * 1024
VMEM_LIMIT_PROMIX = 60 * 1024 * 1024


def _dot(a, b):
    return jnp.dot(a, b, preferred_element_type=F32)


def _hi_lo(x):
    hi = x.astype(BF16)
    lo = (x - hi.astype(F32)).astype(BF16)
    return hi, lo


def _split3(x):
    hi = x.astype(BF16)
    r = x - hi.astype(F32)
    mid = r.astype(BF16)
    lo = (r - mid.astype(F32)).astype(BF16)
    return hi, mid, lo


def _dot3(a, b):
    ah, al = _hi_lo(a)
    bh, bl = _hi_lo(b)
    return _dot(ah, bh) + _dot(ah, bl) + _dot(al, bh)


def _sigmoid(x):
    return 1.0 / (1.0 + jnp.exp(-x))


def _silu(x):
    return x * _sigmoid(x)


def _softplus(x):
    return jnp.maximum(x, 0.0) + jnp.log1p(jnp.exp(-jnp.abs(x)))


def _rmsnorm(x, g):
    return x * lax.rsqrt(jnp.mean(x * x, axis=-1, keepdims=True) + EPS) * g


def _bf16_bits(v):
    return lax.bitcast_convert_type(v.astype(BF16).astype(F32), U32) & jnp.uint32(0xFFFF0000)


def _store_token_tiles(ref, v):
    n = v.shape[0]
    half = v.shape[1] // 2
    words = _bf16_bits(v[:, half:]) | (_bf16_bits(v[:, :half]) >> 16)
    for j in range(TOK_SUB):
        ref[pl.ds(j, n, stride=TOK_SUB), :] = words[:, j * LANES:(j + 1) * LANES]


def _load_token_tiles(ref):
    n = ref.shape[0] // TOK_SUB
    words = jnp.concatenate([ref[pl.ds(j, n, stride=TOK_SUB), :] for j in range(TOK_SUB)], axis=1)
    lo = lax.bitcast_convert_type(words << 16, F32)
    hi = lax.bitcast_convert_type(words & jnp.uint32(0xFFFF0000), F32)
    return jnp.concatenate([lo, hi], axis=1)


def _token_tile(ref, row):
    return ref.at[pl.ds(pl.multiple_of(row * TOK_SUB, TOK_SUB), TOK_SUB)]


def _token_tile_at(ref, first_row):
    return ref.at[pl.ds(pl.multiple_of(first_row, TOK_SUB), TOK_SUB)]


def _ada_kernel(c_ref, w_ref, b_ref, o_ref):
    o_ref[...] = _dot3(_silu(c_ref[...]), w_ref[...]) + b_ref[...]


def _ada(c, w, b):
    bsz = c.shape[0]
    n = w.shape[1]
    tn = 1024
    return pl.pallas_call(
        _ada_kernel,
        out_shape=jax.ShapeDtypeStruct((bsz, n), F32),
        grid=(n // tn,),
        in_specs=[
            pl.BlockSpec((bsz, D_MODEL), lambda j: (0, 0)),
            pl.BlockSpec((D_MODEL, tn), lambda j: (0, j)),
            pl.BlockSpec((1, tn), lambda j: (0, j)),
        ],
        out_specs=pl.BlockSpec((bsz, tn), lambda j: (0, j)),
        compiler_params=pltpu.CompilerParams(dimension_semantics=("arbitrary",)),
        name="ada",
    )(c, w, b.reshape(1, n))


N_MIX_CONSTS = 17


def _promix_kernel(tiles_per_seq, xa_ref, moda_ref, g_ref, w_ref, wdh_ref, wdl_ref, xb_ref, modb_ref, *rest):
    consts = rest[:N_MIX_CONSTS]
    xnew_ref, h2_ref, lt_ref = rest[N_MIX_CONSTS:N_MIX_CONSTS + 3]
    pbuf, dtbuf, zg, dts, xp_ext, xbc_ext, act, state, yssd = rest[N_MIX_CONSTS + 3:]
    s = pl.program_id(0)
    j = lax.rem(jnp.maximum(s - 1, 0), tiles_per_seq)
    tm = TM_MIX

    @pl.when(s == 0)
    def _():
        pbuf[...] = jnp.zeros_like(pbuf)
        dtbuf[...] = jnp.zeros_like(dtbuf)

    @pl.when(j == 0)
    def _():
        xp_ext[0:SEQ_HALO, :] = jnp.zeros((SEQ_HALO, POOL_WIDTH), BF16)
        xbc_ext[:, 0:CONV_HALO, :] = jnp.zeros((CONV_SLABS, CONV_HALO, LANES), F32)
        state[...] = jnp.zeros_like(state)

    xp_ext[SEQ_HALO:SEQ_HALO + tm, :] = pbuf[:, 0:POOL_WIDTH]
    for sl in range(CONV_SLABS):
        xbc_ext[sl, CONV_HALO:CONV_HALO + tm, :] = (
            pbuf[:, COL_XBC + sl * LANES:COL_XBC + (sl + 1) * LANES].astype(F32))
    zg[:, 0:SSD_INNER] = pbuf[:, COL_Z:COL_Z + SSD_INNER]
    zg[:, SSD_INNER:SSD_INNER + 2 * D_MODEL] = pbuf[:, COL_GATE:COL_GATE + 2 * D_MODEL]
    dts[...] = dtbuf[...]

    h = _rmsnorm(xa_ref[...], g_ref[...]) * (1.0 + moda_ref[0, 1:2, :]) + moda_ref[0, 0:1, :]
    hb, hl = _hi_lo(h)

    def project(c0):
        def piece():
            pbuf[:, c0:c0 + PROJ_PIECE] = _dot(hb, w_ref[:, c0:c0 + PROJ_PIECE]).astype(BF16)
        return piece

    def project_dt():
        dtbuf[...] = _dot(hb, wdh_ref[...]) + _dot(hb, wdl_ref[...]) + _dot(hl, wdh_ref[...])

    fill = _Filler([project(c0) for c0 in range(0, MAIN_COLS, PROJ_PIECE)] + [project_dt])
    _mix_main(fill, j, zg, dts, xb_ref, modb_ref, *consts,
              xnew_ref, h2_ref, lt_ref, xp_ext, xbc_ext, act, state, yssd)


class _Filler:
    def __init__(self, thunks):
        self.pending = list(thunks)

    def __call__(self, n):
        n = len(self.pending) if n is None else n
        for _ in range(min(n, len(self.pending))):
            self.pending.pop(0)()


def _mix_main(fill, j, zg_ref, dt_ref, x_ref, mod_ref, poolw_ref, pools_ref, convw_ref, convb_ref,
              dtb_ref, alog_ref, dskip_ref, ng_ref, wbp_ref, wbs_ref, wout_ref, n2g_ref,
              wrh_ref, wrl_ref, br_ref, band_ref, expand_ref,
              xnew_ref, h2_ref, lt_ref,
              xp_ext, xbc_ext, act, state, yssd):
    tm = TM_MIX
    q = SSD_CHUNK
    halo = SEQ_HALO


    gw = POOL_GROUP_WIDTH
    pos = j * tm + lax.broadcasted_iota(I32, (tm, gw), 0)
    yp = []
    for gi, w in enumerate(POOL_WINDOWS):
        cols = slice(gi * gw, (gi + 1) * gw)
        acc = jnp.concatenate([_dot(band_ref[gi], xp_ext[r0:r0 + halo + q, cols])
                               for r0 in range(0, tm, q)], axis=0)
        cur = xp_ext[halo:halo + tm, cols].astype(F32)
        d = acc / jnp.minimum(pos + 1, w).astype(F32) - cur
        yp.append(_dot(d.astype(BF16), poolw_ref[gi]) * pools_ref[:, cols])
        fill(1)
    bp = _dot(jnp.concatenate(yp, axis=1).astype(BF16), wbp_ref[...])

    first = CONV_HALO - (SSD_CONV - 1)
    for sl in range(CONV_SLABS):
        cols = slice(sl * LANES, (sl + 1) * LANES)
        for par in range(2):
            acc = convb_ref[:, cols] + convw_ref[0:1, cols] * xbc_ext[sl, pl.ds(first + par, tm // 2, stride=2), :]
            for k in range(1, SSD_CONV):
                acc = acc + convw_ref[k:k + 1, cols] * xbc_ext[sl, pl.ds(first + par + k, tm // 2, stride=2), :]
            act[sl, pl.ds(par, tm // 2, stride=2), :] = _silu(acc)
        if sl % 4 == 3:
            fill(1)

    xp_ext[0:halo, :] = xp_ext[tm:tm + halo, :]
    xbc_ext[:, 0:CONV_HALO, :] = xbc_ext[:, tm:tm + CONV_HALO, :]

    lane = lax.broadcasted_iota(I32, (q, q), 1)
    row = lax.broadcasted_iota(I32, (q, q), 0)
    causal = row >= lane
    triu = jnp.where(row <= lane, 1.0, 0.0).astype(BF16)
    lo_half = lane < SSD_HEAD_DIM
    a_rows = -jnp.exp(alog_ref[...])
    col_b = SSD_INNER
    col_c = SSD_INNER + SSD_GROUPS * SSD_STATE
    gcw = SSD_INNER // SSD_GROUPS
    for c in range(tm // q):
        r0 = c * q
        dtv = _softplus(dt_ref[r0:r0 + q, :] + dtb_ref[...])
        dt_t = dtv.T
        p1, p2, p3 = _split3(dt_t * a_rows)
        acs_t = _dot(p1, triu) + _dot(p2, triu) + _dot(p3, triu)
        acs = acs_t.T
        acs2 = acs * LOG2E
        acs2_t = acs_t * LOG2E
        expand = expand_ref[...]
        dt_x = _dot(jnp.concatenate(_hi_lo(dtv), axis=1), jnp.concatenate([expand, expand], axis=0))
        fx = _dot(jnp.concatenate([jnp.exp(acs).astype(BF16),
                                   jnp.exp(acs[q - 1:q, :] - acs).astype(BF16)], axis=0), expand)
        ea_x = fx[0:q]
        dec_x = fx[q:2 * q]
        xdt = jnp.concatenate([act[sl, r0:r0 + q, :] for sl in range(SSD_INNER // LANES)], axis=1) * dt_x
        xb = xdt.astype(BF16)
        xdb = (xdt * dec_x).astype(BF16)
        s_all = state[...]
        sb = s_all.astype(BF16)
        for g in range(SSD_GROUPS):
            gcols = slice(g * gcw, (g + 1) * gcw)
            bg = act[col_b // LANES + g, r0:r0 + q, :]
            cg = act[col_c // LANES + g, r0:r0 + q, :].astype(BF16)
            cbm = lax.dot_general(cg, bg.astype(BF16),
                                  (((1,), (1,)), ((), ())), preferred_element_type=F32)
            y_off = _dot(cg, sb[:, gcols]) * ea_x[:, gcols]
            ds = _dot(bg.T.astype(BF16), xdb[:, gcols])
            state[:, gcols] = s_all[:, gcols] * ea_x[q - 1:q, gcols] + ds
            y_diag = []
            for pq in range(gcw // LANES):
                pair = g * (gcw // LANES) + pq
                ms = []
                for hh in range(2):
                    h = 2 * pair + hh
                    a_l = jnp.broadcast_to(acs2[:, h:h + 1], (q, q))
                    a_s = jnp.broadcast_to(acs2_t[h:h + 1, :], (q, q))
                    ms.append((jnp.where(causal, jnp.exp2(a_l - a_s), 0.0) * cbm).astype(BF16))
                x_pair = xb[:, pair * LANES:(pair + 1) * LANES]
                zero = jnp.zeros_like(x_pair)
                y_diag.append(_dot(jnp.concatenate(ms, axis=1),
                                   jnp.concatenate([jnp.where(lo_half, x_pair, zero),
                                                    jnp.where(lo_half, zero, x_pair)], axis=0)))
                fill(1 if pq % 2 == 0 else 0)
            yssd[r0:r0 + q, gcols] = jnp.concatenate(y_diag, axis=1) + y_off

    y = yssd[...] + dskip_ref[...] * jnp.concatenate([act[sl] for sl in range(SSD_INNER // LANES)], axis=1)
    y = y * _silu(zg_ref[:, 0:SSD_INNER].astype(F32))
    fill(1)
    y = _rmsnorm(y, ng_ref[...])
    fill(1)
    bs = _dot(y.astype(BF16), wbs_ref[...])
    g_pool = _sigmoid(zg_ref[:, SSD_INNER:SSD_INNER + D_MODEL].astype(F32))
    g_ssd = _sigmoid(zg_ref[:, SSD_INNER + D_MODEL:SSD_INNER + 2 * D_MODEL].astype(F32))
    fill(1)
    u = g_pool * bp + g_ssd * bs
    x_new = x_ref[...] + mod_ref[0, 2:3, :] * _dot(u.astype(BF16), wout_ref[...])
    xnew_ref[...] = x_new
    fill(None)

    h2 = _rmsnorm(x_new, n2g_ref[...]) * (1.0 + mod_ref[0, 4:5, :]) + mod_ref[0, 3:4, :]
    _store_token_tiles(h2_ref, h2)
    hh, hl = _hi_lo(h2)
    nt = (((1,), (1,)), ((), ()))
    logits_t = (lax.dot_general(wrh_ref[...], hh, nt, preferred_element_type=F32)
                + lax.dot_general(wrl_ref[...], hh, nt, preferred_element_type=F32)
                + lax.dot_general(wrh_ref[...], hl, nt, preferred_element_type=F32))
    lt_ref[...] = logits_t + jnp.concatenate([br_ref[...]] * (tm // LANES), axis=1)


def _promix(x2, mod3, g1, w_main, wd_hi, wd_lo, consts, seq):
    assert len(consts) == N_MIX_CONSTS
    t = x2.shape[0]
    n = t // TM_MIX
    tiles = seq // TM_MIX
    nxt = lambda s: jnp.minimum(s, n - 1)
    cur = lambda s: jnp.maximum(s - 1, 0)

    def cspec(a):
        zeros = (0,) * a.ndim
        return pl.BlockSpec(a.shape, lambda s: zeros, pipeline_mode=pl.Buffered(1))

    return pl.pallas_call(
        functools.partial(_promix_kernel, tiles),
        out_shape=(jax.ShapeDtypeStruct((t, D_MODEL), F32),
                   jax.ShapeDtypeStruct((t * TOK_SUB, LANES), U32),
                   jax.ShapeDtypeStruct((2 * LANES, t), F32)),
        grid=(n + 1,),
        in_specs=[
            pl.BlockSpec((TM_MIX, D_MODEL), lambda s: (nxt(s), 0)),
            pl.BlockSpec((1, 6, D_MODEL), lambda s: (nxt(s) // tiles, 0, 0)),
            cspec(g1), cspec(w_main), cspec(wd_hi), cspec(wd_lo),
            pl.BlockSpec((TM_MIX, D_MODEL), lambda s: (cur(s), 0)),
            pl.BlockSpec((1, 6, D_MODEL), lambda s: (cur(s) // tiles, 0, 0)),
        ] + [cspec(a) for a in consts],
        out_specs=(pl.BlockSpec((TM_MIX, D_MODEL), lambda s: (cur(s), 0)),
                   pl.BlockSpec((TM_MIX * TOK_SUB, LANES), lambda s: (cur(s), 0)),
                   pl.BlockSpec((2 * LANES, TM_MIX), lambda s: (0, cur(s)))),
        scratch_shapes=[
            pltpu.VMEM((TM_MIX, MAIN_COLS), BF16),
            pltpu.VMEM((TM_MIX, LANES), F32),
            pltpu.VMEM((TM_MIX, SSD_INNER + 2 * D_MODEL), BF16),
            pltpu.VMEM((TM_MIX, LANES), F32),
            pltpu.VMEM((SEQ_HALO + TM_MIX, POOL_WIDTH), BF16),
            pltpu.VMEM((CONV_SLABS, CONV_HALO + TM_MIX, LANES), F32),
            pltpu.VMEM((CONV_SLABS, TM_MIX, LANES), F32),
            pltpu.VMEM((SSD_STATE, SSD_INNER), F32),
            pltpu.VMEM((TM_MIX, SSD_INNER), F32),
        ],
        compiler_params=pltpu.CompilerParams(dimension_semantics=("arbitrary",),
                                             vmem_limit_bytes=VMEM_LIMIT_PROMIX),
        name="promix",
    )(x2, mod3, g1, w_main, wd_hi, wd_lo, x2, mod3, *consts)


def _route_kernel(lt_ref, idx_ref, rf_ref, cnt_ref, carry):
    i = pl.program_id(0)
    tr = TR_ROUTE
    ck = ROUTE_CHUNK
    neg = -jnp.inf

    @pl.when(i == 0)
    def _():
        carry[...] = jnp.zeros_like(carry)

    row8 = lax.broadcasted_iota(I32, (8, tr), 0)
    lg = jnp.where(row8 < MOE_GROUPS, lt_ref[0:8, :], neg)
    mg = jnp.max(lg, axis=0, keepdims=True)
    gsel = jnp.min(jnp.where(lg == mg, row8, 8), axis=0, keepdims=True)
    gwt = 1.0 / jnp.sum(jnp.exp(lg - mg), axis=0, keepdims=True)
    rowe = lax.broadcasted_iota(I32, (N_EXPERTS, tr), 0)
    le = jnp.where(lax.shift_right_logical(rowe, 3) == gsel, lt_ref[LANES:LANES + N_EXPERTS, :], neg)
    m1 = jnp.max(le, axis=0, keepdims=True)
    e0 = jnp.min(jnp.where(le == m1, rowe, N_EXPERTS), axis=0, keepdims=True)
    le2 = jnp.where(rowe == e0, neg, le)
    m2 = jnp.max(le2, axis=0, keepdims=True)
    e1 = jnp.min(jnp.where(le2 == m2, rowe, N_EXPERTS), axis=0, keepdims=True)
    t21 = jnp.exp(m2 - m1)
    w0 = gwt / (1.0 + t21)
    w1 = gwt * t21 / (1.0 + t21)
    oh0 = jnp.where(rowe == e0, 1.0, 0.0)
    oh1 = jnp.where(rowe == e1, 1.0, 0.0)
    oh = oh0 + oh1

    kr = lax.broadcasted_iota(I32, (ck, ck), 0)
    kc = lax.broadcasted_iota(I32, (ck, ck), 1)
    before = jnp.where(kr < kc, 1.0, 0.0).astype(BF16)
    wpad = jnp.zeros((LANES - 2, ck), F32)
    for c0 in range(0, tr, ck):
        cs = slice(c0, c0 + ck)
        pre = _dot(oh[:, cs].astype(BF16), before) + carry[:, 0:1]
        idx_ref[2:3, cs] = jnp.sum(pre * oh0[:, cs], axis=0, keepdims=True).astype(I32)
        idx_ref[3:4, cs] = jnp.sum(pre * oh1[:, cs], axis=0, keepdims=True).astype(I32)
        carry[...] = carry[...] + jnp.sum(oh[:, cs], axis=1, keepdims=True)
        rf_ref[cs, :] = jnp.concatenate([w0[:, cs], w1[:, cs], wpad], axis=0).T
    idx_ref[0:1, :] = e0
    idx_ref[1:2, :] = e1
    idx_ref[4:8, :] = jnp.zeros((4, tr), I32)
    cnt_ref[...] = carry[...].astype(I32)


def _route(logits_t):
    t = logits_t.shape[1]
    return pl.pallas_call(
        _route_kernel,
        out_shape=(jax.ShapeDtypeStruct((8, t), I32),
                   jax.ShapeDtypeStruct((t, LANES), F32),
                   jax.ShapeDtypeStruct((N_EXPERTS, LANES), I32)),
        grid=(t // TR_ROUTE,),
        in_specs=[pl.BlockSpec((2 * LANES, TR_ROUTE), lambda i: (0, i))],
        out_specs=(pl.BlockSpec((8, TR_ROUTE), lambda i: (0, i)),
                   pl.BlockSpec((TR_ROUTE, LANES), lambda i: (i, 0)),
                   pl.BlockSpec((N_EXPERTS, LANES), lambda i: (0, 0))),
        scratch_shapes=[pltpu.VMEM((N_EXPERTS, LANES), F32)],
        compiler_params=pltpu.CompilerParams(dimension_semantics=("arbitrary",)),
        name="moe_route",
    )(logits_t)


def _dest_kernel(ps_ref, idx_ref, dest_ref):
    dest = [idx_ref[2:3, :], idx_ref[3:4, :]]
    for e in range(N_EXPERTS):
        for k in range(2):
            dest[k] = dest[k] + jnp.where(idx_ref[k:k + 1, :] == e, ps_ref[e], 0)
    dest_ref[0:1, :] = dest[0] * TOK_SUB
    dest_ref[1:2, :] = dest[1] * TOK_SUB
    dest_ref[2:8, :] = jnp.zeros((6, dest_ref.shape[1]), I32)


def _dest(pad_starts, idx):
    t = idx.shape[1]
    return pl.pallas_call(
        _dest_kernel,
        out_shape=jax.ShapeDtypeStruct((8, t), I32),
        grid_spec=pltpu.PrefetchScalarGridSpec(
            num_scalar_prefetch=1,
            grid=(t // TR_ROUTE,),
            in_specs=[pl.BlockSpec((8, TR_ROUTE), lambda i, ps: (0, i))],
            out_specs=pl.BlockSpec((8, TR_ROUTE), lambda i, ps: (0, i)),
        ),
        compiler_params=pltpu.CompilerParams(dimension_semantics=("arbitrary",)),
        name="moe_dest",
    )(pad_starts, idx)


def _scatter_kernel(pv_ref, pe_ref, nu_ref, d0_ref, d1_ref, h_ref, xb_ref, zblk, sem, zsem):
    i = pl.program_id(0)
    ts = TS_SCATTER
    bm = BM_EXPERT
    nb = xb_ref.shape[0] // (bm * TOK_SUB)

    @pl.when(i == 0)
    def _():
        zblk[...] = jnp.zeros_like(zblk)

        def blk_copy(blk):
            start = pl.multiple_of(blk * (bm * TOK_SUB), bm * TOK_SUB)
            return pltpu.make_async_copy(zblk, xb_ref.at[pl.ds(start, bm * TOK_SUB)], zsem)

        def for_pad_blocks(fn):
            def per_expert(e, carry):
                @pl.when(pv_ref[e] < pe_ref[e])
                def _():
                    fn(blk_copy(lax.div(pe_ref[e], bm) - 1))
                return carry
            lax.fori_loop(0, N_EXPERTS, per_expert, 0)

            def per_block(blk, carry):
                fn(blk_copy(blk))
                return carry
            lax.fori_loop(nu_ref[0], nb, per_block, 0)

        for_pad_blocks(lambda cp: cp.start())
        for_pad_blocks(lambda cp: cp.wait())

    def issue(j, carry):
        for u in range(ISSUE_UNROLL):
            r = j * ISSUE_UNROLL + u
            for k in range(2):
                pltpu.make_async_copy(_token_tile(h_ref, r), _token_tile_at(xb_ref, (d0_ref, d1_ref)[k][i * ts + r]),
                                      sem.at[k]).start(priority=k)
        return carry

    lax.fori_loop(0, ts // ISSUE_UNROLL, issue, 0)
    for k in range(2):
        pltpu.make_async_copy(h_ref, xb_ref.at[pl.ds(0, ts * TOK_SUB)], sem.at[k]).wait()


def _scatter(pad_valid_end, pad_ends, n_used, dest0, dest1, h2t, n_rows):
    t = h2t.shape[0] // TOK_SUB
    return pl.pallas_call(
        _scatter_kernel,
        out_shape=jax.ShapeDtypeStruct((n_rows * TOK_SUB, LANES), U32),
        grid_spec=pltpu.PrefetchScalarGridSpec(
            num_scalar_prefetch=5,
            grid=(t // TS_SCATTER,),
            in_specs=[pl.BlockSpec((TS_SCATTER * TOK_SUB, LANES), lambda i, *_: (i, 0))],
            out_specs=pl.BlockSpec(memory_space=pl.ANY),
            scratch_shapes=[pltpu.VMEM((BM_EXPERT * TOK_SUB, LANES), U32),
                            pltpu.SemaphoreType.DMA((2,)),
                            pltpu.SemaphoreType.DMA(())],
        ),
        compiler_params=pltpu.CompilerParams(dimension_semantics=("arbitrary",),
                                             disable_bounds_checks=True),
        name="moe_scatter",
    )(pad_valid_end, pad_ends, n_used, dest0, dest1, h2t)


def _expert_kernel(be_ref, nu_ref, x_ref, w13_ref, w2_ref, y_ref, w13b, w2b):
    i = pl.program_id(0)

    @pl.when((i == 0) | (be_ref[i] != be_ref[jnp.maximum(i - 1, 0)]))
    def _():
        w13b[...] = w13_ref[0].astype(BF16)
        w2b[...] = w2_ref[0].astype(BF16)

    @pl.when(i < nu_ref[0])
    def _():
        x = _load_token_tiles(x_ref).astype(BF16)
        a = _dot(x, w13b[:, 0:EXPERT_FF])
        g = _dot(x, w13b[:, EXPERT_FF:2 * EXPERT_FF])
        _store_token_tiles(y_ref, _dot((_silu(a) * g).astype(BF16), w2b[...]))

    @pl.when(i >= nu_ref[0])
    def _():
        y_ref[...] = jnp.zeros_like(y_ref)


def _experts(block_expert, n_used, xbuf, w13, w2):
    nb = xbuf.shape[0] // (BM_EXPERT * TOK_SUB)
    used = lambda i, nu: jnp.minimum(i, nu[0] - 1)
    return pl.pallas_call(
        _expert_kernel,
        out_shape=jax.ShapeDtypeStruct(xbuf.shape, U32),
        grid_spec=pltpu.PrefetchScalarGridSpec(
            num_scalar_prefetch=2,
            grid=(nb,),
            in_specs=[
                pl.BlockSpec((BM_EXPERT * TOK_SUB, LANES), lambda i, be, nu: (used(i, nu), 0)),
                pl.BlockSpec((1, D_MODEL, 2 * EXPERT_FF), lambda i, be, nu: (be[i], 0, 0)),
                pl.BlockSpec((1, EXPERT_FF, D_MODEL), lambda i, be, nu: (be[i], 0, 0)),
            ],
            out_specs=pl.BlockSpec((BM_EXPERT * TOK_SUB, LANES), lambda i, be, nu: (i, 0)),
            scratch_shapes=[pltpu.VMEM((D_MODEL, 2 * EXPERT_FF), BF16),
                            pltpu.VMEM((EXPERT_FF, D_MODEL), BF16)],
        ),
        compiler_params=pltpu.CompilerParams(dimension_semantics=("arbitrary",),
                                             vmem_limit_bytes=VMEM_LIMIT),
        name="moe_experts",
    )(block_expert, n_used, xbuf, w13, w2)


def _combine_kernel(d0_ref, d1_ref, x_ref, rf_ref, mod_ref, fg_ref, yb_ref, out_ref, gbuf, sem):
    i = pl.program_id(0)
    n = pl.num_programs(0)
    tc = TC_COMBINE
    slot = lax.rem(i, 2)

    def start_gather(tile, s):
        def issue(j, carry):
            for u in range(ISSUE_UNROLL):
                r = j * ISSUE_UNROLL + u
                for k in range(2):
                    src = (d0_ref, d1_ref)[k][tile * tc + r]
                    pltpu.make_async_copy(_token_tile_at(yb_ref, src), _token_tile(gbuf.at[s, k], r),
                                          sem.at[s, k]).start(priority=k)
            return carry
        lax.fori_loop(0, tc // ISSUE_UNROLL, issue, 0)

    @pl.when(i == 0)
    def _():
        start_gather(0, 0)

    @pl.when(i + 1 < n)
    def _():
        start_gather(i + 1, 1 - slot)

    for k in range(2):
        pltpu.make_async_copy(yb_ref.at[pl.ds(0, tc * TOK_SUB)], gbuf.at[slot, k], sem.at[slot, k]).wait()
    moe = (rf_ref[:, 0:1] * _load_token_tiles(gbuf.at[slot, 0])
           + rf_ref[:, 1:2] * _load_token_tiles(gbuf.at[slot, 1]))
    out_ref[...] = _rmsnorm(x_ref[...] + mod_ref[0, 5:6, :] * moe, fg_ref[...])


def _combine(dest0, dest1, x_new, rf, mod3, fg, ybuf, seq):
    t = x_new.shape[0]
    n = t // TC_COMBINE
    tiles_per_seq = seq // TC_COMBINE
    return pl.pallas_call(
        _combine_kernel,
        out_shape=jax.ShapeDtypeStruct((t, D_MODEL), F32),
        grid_spec=pltpu.PrefetchScalarGridSpec(
            num_scalar_prefetch=2,
            grid=(n,),
            in_specs=[
                pl.BlockSpec((TC_COMBINE, D_MODEL), lambda i, *_: (i, 0)),
                pl.BlockSpec((TC_COMBINE, LANES), lambda i, *_: (i, 0)),
                pl.BlockSpec((1, 6, D_MODEL), lambda i, *_: (i // tiles_per_seq, 0, 0)),
                pl.BlockSpec((1, D_MODEL), lambda i, *_: (0, 0)),
                pl.BlockSpec(memory_space=pl.ANY),
            ],
            out_specs=pl.BlockSpec((TC_COMBINE, D_MODEL), lambda i, *_: (i, 0)),
            scratch_shapes=[pltpu.VMEM((2, 2, TC_COMBINE * TOK_SUB, LANES), U32),
                            pltpu.SemaphoreType.DMA((2, 2))],
        ),
        compiler_params=pltpu.CompilerParams(dimension_semantics=("arbitrary",),
                                             disable_bounds_checks=True),
        name="moe_combine",
    )(dest0, dest1, x_new, rf, mod3, fg, ybuf)


def _pad_lanes(a, n=LANES):
    return jnp.pad(a, [(0, 0)] * (a.ndim - 1) + [(0, n - a.shape[-1])])


def _selection_matrices():
    q = SSD_CHUNK
    rel = jnp.arange(SEQ_HALO + q)[None, :] - jnp.arange(q)[:, None] - SEQ_HALO
    band = jnp.stack([(rel <= 0) & (rel > -w) for w in POOL_WINDOWS]).astype(BF16)
    expand = (jnp.arange(LANES)[:, None] == jnp.arange(SSD_INNER)[None, :] // SSD_HEAD_DIM).astype(BF16)
    return band, expand


def _layer(x2, c, bsz, seq, ada_w, ada_b, norm1_g, w_in, pool_w, pool_scale, conv_w, conv_b, dt_bias,
           a_log, d_skip, ssd_norm_g, w_branch_pool, w_branch_ssd, w_out, norm2_g, router_group_w,
           router_group_b, router_expert_w, router_expert_b, w13, w2):
    t = bsz * seq
    mod3 = _ada(c, ada_w, ada_b).reshape(bsz, 6, D_MODEL)

    i2 = POOL_WIDTH + SSD_INNER + SSD_CONV_DIM
    i3 = i2 + SSD_HEADS
    w_main = jnp.concatenate([w_in[:, :i2], w_in[:, i3:]], axis=1).astype(BF16)
    wd = _pad_lanes(w_in[:, i2:i3])
    wd_hi = wd.astype(BF16)
    wd_lo = (wd - wd_hi.astype(F32)).astype(BF16)

    wr = jnp.concatenate([_pad_lanes(router_group_w), _pad_lanes(router_expert_w)], axis=1).T
    wr_hi = wr.astype(BF16)
    wr_lo = (wr - wr_hi.astype(F32)).astype(BF16)
    br = jnp.concatenate([_pad_lanes(router_group_b.reshape(1, -1)),
                          _pad_lanes(router_expert_b.reshape(1, -1))], axis=1)
    br = jnp.broadcast_to(br.reshape(2 * LANES, 1), (2 * LANES, LANES))
    consts = [
        pool_w.astype(BF16),
        pool_scale.reshape(1, POOL_WIDTH),
        conv_w,
        conv_b.reshape(1, SSD_CONV_DIM),
        _pad_lanes(dt_bias.reshape(1, SSD_HEADS)),
        jnp.broadcast_to(jnp.pad(a_log, (0, LANES - SSD_HEADS))[:, None], (LANES, LANES)),
        jnp.repeat(d_skip, SSD_HEAD_DIM).reshape(1, SSD_INNER),
        ssd_norm_g.reshape(1, SSD_INNER),
        w_branch_pool.astype(BF16),
        w_branch_ssd.astype(BF16),
        w_out.astype(BF16),
        norm2_g.reshape(1, D_MODEL),
        wr_hi, wr_lo, br,
        *_selection_matrices(),
    ]
    x_new, h2, logits_t = _promix(x2, mod3, norm1_g.reshape(1, D_MODEL), w_main, wd_hi, wd_lo, consts, seq)
    idx, rf, cnt = _route(logits_t)

    counts = cnt[:, 0]
    padded = ((counts + BM_EXPERT - 1) // BM_EXPERT) * BM_EXPERT
    pad_ends = jnp.cumsum(padded)
    pad_starts = (pad_ends - padded).astype(I32)
    n_rows = 2 * t + N_EXPERTS * BM_EXPERT
    nb = n_rows // BM_EXPERT
    block_row0 = jnp.arange(nb, dtype=I32) * BM_EXPERT
    block_expert = jnp.minimum(jnp.sum((pad_ends[None, :] <= block_row0[:, None]).astype(I32), axis=1),
                               N_EXPERTS - 1)
    n_used = (pad_ends[-1:] // BM_EXPERT).astype(I32)

    dest = _dest(pad_starts, idx)
    dest0, dest1 = dest[0], dest[1]
    xbuf = _scatter(pad_starts + counts, pad_ends.astype(I32), n_used, dest0, dest1, h2, n_rows)
    ybuf = _experts(block_expert, n_used, xbuf, w13, w2)
    return dest0, dest1, x_new, rf, mod3, ybuf


def kernel(x, c, ada_w, ada_b, norm1_g, w_in, pool_w, pool_scale, conv_w, conv_b, dt_bias, a_log, d_skip, ssd_norm_g, w_branch_pool, w_branch_ssd, w_out, norm2_g, router_group_w, router_group_b, router_expert_w, router_expert_b, w13, w2, final_norm_g):
    bsz, seq, d = x.shape
    depth = ada_w.shape[0]
    assert depth == 1 and d == D_MODEL and seq % TM_MIX == 0
    x2 = x.reshape(bsz * seq, d)
    l = 0
    dest0, dest1, x_new, rf, mod3, ybuf = _layer(
        x2, c, bsz, seq, ada_w[l], ada_b[l], norm1_g[l], w_in[l], pool_w[l], pool_scale[l], conv_w[l],
        conv_b[l], dt_bias[l], a_log[l], d_skip[l], ssd_norm_g[l], w_branch_pool[l], w_branch_ssd[l],
        w_out[l], norm2_g[l], router_group_w[l], router_group_b[l], router_expert_w[l],
        router_expert_b[l], w13[l], w2[l])
    out = _combine(dest0, dest1, x_new, rf, mod3, final_norm_g.reshape(1, d), ybuf, seq)
    return out.reshape(bsz, seq, d)
```

```python
import functools

import jax
import jax.numpy as jnp
from jax import lax
from jax.experimental import pallas as pl
from jax.experimental.pallas import tpu as pltpu

F32 = jnp.float32
BF16 = jnp.bfloat16
I32 = jnp.int32
U32 = jnp.uint32

D_MODEL = 1024
POOL_WIDTH = 1024
POOL_GROUPS = 4
POOL_GROUP_WIDTH = 256
POOL_WINDOWS = (2, 4, 8, 16)
SSD_INNER = 2048
SSD_HEAD_DIM = 64
SSD_HEADS = 32
SSD_GROUPS = 4
SSD_STATE = 128
SSD_CONV = 4
SSD_CHUNK = 128
SSD_CONV_DIM = SSD_INNER + 2 * SSD_GROUPS * SSD_STATE
N_EXPERTS = 32
EXPERTS_PER_GROUP = 8
MOE_GROUPS = 4
EXPERT_FF = 512
EPS = 1e-6

LANES = 128
TOK_SUB = D_MODEL // (2 * LANES)
MAIN_COLS = POOL_WIDTH + SSD_INNER + SSD_CONV_DIM + 2 * D_MODEL
COL_Z = POOL_WIDTH
COL_XBC = COL_Z + SSD_INNER
COL_GATE = COL_XBC + SSD_CONV_DIM

TM_MIX = 256
PROJ_PIECE = 256
SEQ_HALO = SSD_CHUNK
CONV_HALO = 8
CONV_SLABS = SSD_CONV_DIM // LANES
LOG2E = 1.4426950408889634
TR_ROUTE = 2048
ROUTE_CHUNK = 256
TS_SCATTER = 2048
ISSUE_UNROLL = 16
BM_EXPERT = 512
TC_COMBINE = 256
VMEM_LIMIT = 56 * 1024 * 1024
VMEM_LIMIT_PROMIX = 60 * 1024 * 1024


def _dot(a, b):
    return jnp.dot(a, b, preferred_element_type=F32)


def _hi_lo(x):
    hi = x.astype(BF16)
    lo = (x - hi.astype(F32)).astype(BF16)
    return hi, lo


def _split3(x):
    hi = x.astype(BF16)
    r = x - hi.astype(F32)
    mid = r.astype(BF16)
    lo = (r - mid.astype(F32)).astype(BF16)
    return hi, mid, lo


def _dot3(a, b):
    ah, al = _hi_lo(a)
    bh, bl = _hi_lo(b)
    return _dot(ah, bh) + _dot(ah, bl) + _dot(al, bh)


def _sigmoid(x):
    return 1.0 / (1.0 + jnp.exp(-x))


def _silu(x):
    return x * _sigmoid(x)


def _softplus(x):
    return jnp.maximum(x, 0.0) + jnp.log1p(jnp.exp(-jnp.abs(x)))


def _rmsnorm(x, g):
    return x * lax.rsqrt(jnp.mean(x * x, axis=-1, keepdims=True) + EPS) * g


def _bf16_bits(v):
    return lax.bitcast_convert_type(v.astype(BF16).astype(F32), U32) & jnp.uint32(0xFFFF0000)


def _store_token_tiles(ref, v):
    n = v.shape[0]
    half = v.shape[1] // 2
    words = _bf16_bits(v[:, half:]) | (_bf16_bits(v[:, :half]) >> 16)
    for j in range(TOK_SUB):
        ref[pl.ds(j, n, stride=TOK_SUB), :] = words[:, j * LANES:(j + 1) * LANES]


def _load_token_tiles(ref):
    n = ref.shape[0] // TOK_SUB
    words = jnp.concatenate([ref[pl.ds(j, n, stride=TOK_SUB), :] for j in range(TOK_SUB)], axis=1)
    lo = lax.bitcast_convert_type(words << 16, F32)
    hi = lax.bitcast_convert_type(words & jnp.uint32(0xFFFF0000), F32)
    return jnp.concatenate([lo, hi], axis=1)


def _token_tile(ref, row):
    return ref.at[pl.ds(pl.multiple_of(row * TOK_SUB, TOK_SUB), TOK_SUB)]


def _token_tile_at(ref, first_row):
    return ref.at[pl.ds(pl.multiple_of(first_row, TOK_SUB), TOK_SUB)]


def _ada_kernel(c_ref, w_ref, b_ref, o_ref):
    o_ref[...] = _dot3(_silu(c_ref[...]), w_ref[...]) + b_ref[...]


def _ada(c, w, b):
    bsz = c.shape[0]
    n = w.shape[1]
    tn = 1024
    return pl.pallas_call(
        _ada_kernel,
        out_shape=jax.ShapeDtypeStruct((bsz, n), F32),
        grid=(n // tn,),
        in_specs=[
            pl.BlockSpec((bsz, D_MODEL), lambda j: (0, 0)),
            pl.BlockSpec((D_MODEL, tn), lambda j: (0, j)),
            pl.BlockSpec((1, tn), lambda j: (0, j)),
        ],
        out_specs=pl.BlockSpec((bsz, tn), lambda j: (0, j)),
        compiler_params=pltpu.CompilerParams(dimension_semantics=("arbitrary",)),
        name="ada",
    )(c, w, b.reshape(1, n))


N_MIX_CONSTS = 17


def _promix_kernel(tiles_per_seq, xa_ref, moda_ref, g_ref, w_ref, wdh_ref, wdl_ref, xb_ref, modb_ref, *rest):
    consts = rest[:N_MIX_CONSTS]
    xnew_ref, h2_ref, lt_ref = rest[N_MIX_CONSTS:N_MIX_CONSTS + 3]
    pbuf, dtbuf, zg, dts, xp_ext, xbc_ext, act, state, yssd = rest[N_MIX_CONSTS + 3:]
    s = pl.program_id(0)
    j = lax.rem(jnp.maximum(s - 1, 0), tiles_per_seq)
    tm = TM_MIX

    @pl.when(s == 0)
    def _():
        pbuf[...] = jnp.zeros_like(pbuf)
        dtbuf[...] = jnp.zeros_like(dtbuf)

    @pl.when(j == 0)
    def _():
        xp_ext[0:SEQ_HALO, :] = jnp.zeros((SEQ_HALO, POOL_WIDTH), BF16)
        xbc_ext[:, 0:CONV_HALO, :] = jnp.zeros((CONV_SLABS, CONV_HALO, LANES), F32)
        state[...] = jnp.zeros_like(state)

    xp_ext[SEQ_HALO:SEQ_HALO + tm, :] = pbuf[:, 0:POOL_WIDTH]
    for sl in range(CONV_SLABS):
        xbc_ext[sl, CONV_HALO:CONV_HALO + tm, :] = (
            pbuf[:, COL_XBC + sl * LANES:COL_XBC + (sl + 1) * LANES].astype(F32))
    zg[:, 0:SSD_INNER] = pbuf[:, COL_Z:COL_Z + SSD_INNER]
    zg[:, SSD_INNER:SSD_INNER + 2 * D_MODEL] = pbuf[:, COL_GATE:COL_GATE + 2 * D_MODEL]
    dts[...] = dtbuf[...]

    h = _rmsnorm(xa_ref[...], g_ref[...]) * (1.0 + moda_ref[0, 1:2, :]) + moda_ref[0, 0:1, :]
    hb, hl = _hi_lo(h)

    def project(c0):
        def piece():
            pbuf[:, c0:c0 + PROJ_PIECE] = _dot(hb, w_ref[:, c0:c0 + PROJ_PIECE]).astype(BF16)
        return piece

    def project_dt():
        dtbuf[...] = _dot(hb, wdh_ref[...]) + _dot(hb, wdl_ref[...]) + _dot(hl, wdh_ref[...])

    fill = _Filler([project(c0) for c0 in range(0, MAIN_COLS, PROJ_PIECE)] + [project_dt])
    _mix_main(fill, j, zg, dts, xb_ref, modb_ref, *consts,
              xnew_ref, h2_ref, lt_ref, xp_ext, xbc_ext, act, state, yssd)


class _Filler:
    def __init__(self, thunks):
        self.pending = list(thunks)

    def __call__(self, n):
        n = len(self.pending) if n is None else n
        for _ in range(min(n, len(self.pending))):
            self.pending.pop(0)()


def _mix_main(fill, j, zg_ref, dt_ref, x_ref, mod_ref, poolw_ref, pools_ref, convw_ref, convb_ref,
              dtb_ref, alog_ref, dskip_ref, ng_ref, wbp_ref, wbs_ref, wout_ref, n2g_ref,
              wrh_ref, wrl_ref, br_ref, band_ref, expand_ref,
              xnew_ref, h2_ref, lt_ref,
              xp_ext, xbc_ext, act, state, yssd):
    tm = TM_MIX
    q = SSD_CHUNK
    halo = SEQ_HALO


    gw = POOL_GROUP_WIDTH
    pos = j * tm + lax.broadcasted_iota(I32, (tm, gw), 0)
    yp = []
    for gi, w in enumerate(POOL_WINDOWS):
        cols = slice(gi * gw, (gi + 1) * gw)
        acc = jnp.concatenate([_dot(band_ref[gi], xp_ext[r0:r0 + halo + q, cols])
                               for r0 in range(0, tm, q)], axis=0)
        cur = xp_ext[halo:halo + tm, cols].astype(F32)
        d = acc / jnp.minimum(pos + 1, w).astype(F32) - cur
        yp.append(_dot(d.astype(BF16), poolw_ref[gi]) * pools_ref[:, cols])
        fill(1)
    bp = _dot(jnp.concatenate(yp, axis=1).astype(BF16), wbp_ref[...])

    first = CONV_HALO - (SSD_CONV - 1)
    for sl in range(CONV_SLABS):
        cols = slice(sl * LANES, (sl + 1) * LANES)
        for par in range(2):
            acc = convb_ref[:, cols] + convw_ref[0:1, cols] * xbc_ext[sl, pl.ds(first + par, tm // 2, stride=2), :]
            for k in range(1, SSD_CONV):
                acc = acc + convw_ref[k:k + 1, cols] * xbc_ext[sl, pl.ds(first + par + k, tm // 2, stride=2), :]
            act[sl, pl.ds(par, tm // 2, stride=2), :] = _silu(acc)
        if sl % 4 == 3:
            fill(1)

    xp_ext[0:halo, :] = xp_ext[tm:tm + halo, :]
    xbc_ext[:, 0:CONV_HALO, :] = xbc_ext[:, tm:tm + CONV_HALO, :]

    lane = lax.broadcasted_iota(I32, (q, q), 1)
    row = lax.broadcasted_iota(I32, (q, q), 0)
    causal = row >= lane
    triu = jnp.where(row <= lane, 1.0, 0.0).astype(BF16)
    lo_half = lane < SSD_HEAD_DIM
    a_rows = -jnp.exp(alog_ref[...])
    col_b = SSD_INNER
    col_c = SSD_INNER + SSD_GROUPS * SSD_STATE
    gcw = SSD_INNER // SSD_GROUPS
    for c in range(tm // q):
        r0 = c * q
        dtv = _softplus(dt_ref[r0:r0 + q, :] + dtb_ref[...])
        dt_t = dtv.T
        p1, p2, p3 = _split3(dt_t * a_rows)
        acs_t = _dot(p1, triu) + _dot(p2, triu) + _dot(p3, triu)
        acs = acs_t.T
        acs2 = acs * LOG2E
        acs2_t = acs_t * LOG2E
        expand = expand_ref[...]
        dt_x = _dot(jnp.concatenate(_hi_lo(dtv), axis=1), jnp.concatenate([expand, expand], axis=0))
        fx = _dot(jnp.concatenate([jnp.exp(acs).astype(BF16),
                                   jnp.exp(acs[q - 1:q, :] - acs).astype(BF16)], axis=0), expand)
        ea_x = fx[0:q]
        dec_x = fx[q:2 * q]
        xdt = jnp.concatenate([act[sl, r0:r0 + q, :] for sl in range(SSD_INNER // LANES)], axis=1) * dt_x
        xb = xdt.astype(BF16)
        xdb = (xdt * dec_x).astype(BF16)
        s_all = state[...]
        sb = s_all.astype(BF16)
        for g in range(SSD_GROUPS):
            gcols = slice(g * gcw, (g + 1) * gcw)
            bg = act[col_b // LANES + g, r0:r0 + q, :]
            cg = act[col_c // LANES + g, r0:r0 + q, :].astype(BF16)
            cbm = lax.dot_general(cg, bg.astype(BF16),
                                  (((1,), (1,)), ((), ())), preferred_element_type=F32)
            y_off = _dot(cg, sb[:, gcols]) * ea_x[:, gcols]
            ds = _dot(bg.T.astype(BF16), xdb[:, gcols])
            state[:, gcols] = s_all[:, gcols] * ea_x[q - 1:q, gcols] + ds
            y_diag = []
            for pq in range(gcw // LANES):
                pair = g * (gcw // LANES) + pq
                ms = []
                for hh in range(2):
                    h = 2 * pair + hh
                    a_l = jnp.broadcast_to(acs2[:, h:h + 1], (q, q))
                    a_s = jnp.broadcast_to(acs2_t[h:h + 1, :], (q, q))
                    ms.append((jnp.where(causal, jnp.exp2(a_l - a_s), 0.0) * cbm).astype(BF16))
                x_pair = xb[:, pair * LANES:(pair + 1) * LANES]
                zero = jnp.zeros_like(x_pair)
                y_diag.append(_dot(jnp.concatenate(ms, axis=1),
                                   jnp.concatenate([jnp.where(lo_half, x_pair, zero),
                                                    jnp.where(lo_half, zero, x_pair)], axis=0)))
                fill(1 if pq % 2 == 0 else 0)
            yssd[r0:r0 + q, gcols] = jnp.concatenate(y_diag, axis=1) + y_off

    y = yssd[...] + dskip_ref[...] * jnp.concatenate([act[sl] for sl in range(SSD_INNER // LANES)], axis=1)
    y = y * _silu(zg_ref[:, 0:SSD_INNER].astype(F32))
    fill(1)
    y = _rmsnorm(y, ng_ref[...])
    fill(1)
    bs = _dot(y.astype(BF16), wbs_ref[...])
    g_pool = _sigmoid(zg_ref[:, SSD_INNER:SSD_INNER + D_MODEL].astype(F32))
    g_ssd = _sigmoid(zg_ref[:, SSD_INNER + D_MODEL:SSD_INNER + 2 * D_MODEL].astype(F32))
    fill(1)
    u = g_pool * bp + g_ssd * bs
    x_new = x_ref[...] + mod_ref[0, 2:3, :] * _dot(u.astype(BF16), wout_ref[...])
    xnew_ref[...] = x_new
    fill(None)

    h2 = _rmsnorm(x_new, n2g_ref[...]) * (1.0 + mod_ref[0, 4:5, :]) + mod_ref[0, 3:4, :]
    _store_token_tiles(h2_ref, h2)
    hh, hl = _hi_lo(h2)
    nt = (((1,), (1,)), ((), ()))
    logits_t = (lax.dot_general(wrh_ref[...], hh, nt, preferred_element_type=F32)
                + lax.dot_general(wrl_ref[...], hh, nt, preferred_element_type=F32)
                + lax.dot_general(wrh_ref[...], hl, nt, preferred_element_type=F32))
    lt_ref[...] = logits_t + jnp.concatenate([br_ref[...]] * (tm // LANES), axis=1)


def _promix(x2, mod3, g1, w_main, wd_hi, wd_lo, consts, seq):
    assert len(consts) == N_MIX_CONSTS
    t = x2.shape[0]
    n = t // TM_MIX
    tiles = seq // TM_MIX
    nxt = lambda s: jnp.minimum(s, n - 1)
    cur = lambda s: jnp.maximum(s - 1, 0)

    def cspec(a):
        zeros = (0,) * a.ndim
        return pl.BlockSpec(a.shape, lambda s: zeros, pipeline_mode=pl.Buffered(1))

    return pl.pallas_call(
        functools.partial(_promix_kernel, tiles),
        out_shape=(jax.ShapeDtypeStruct((t, D_MODEL), F32),
                   jax.ShapeDtypeStruct((t * TOK_SUB, LANES), U32),
                   jax.ShapeDtypeStruct((2 * LANES, t), F32)),
        grid=(n + 1,),
        in_specs=[
            pl.BlockSpec((TM_MIX, D_MODEL), lambda s: (nxt(s), 0)),
            pl.BlockSpec((1, 6, D_MODEL), lambda s: (nxt(s) // tiles, 0, 0)),
            cspec(g1), cspec(w_main), cspec(wd_hi), cspec(wd_lo),
            pl.BlockSpec((TM_MIX, D_MODEL), lambda s: (cur(s), 0)),
            pl.BlockSpec((1, 6, D_MODEL), lambda s: (cur(s) // tiles, 0, 0)),
        ] + [cspec(a) for a in consts],
        out_specs=(pl.BlockSpec((TM_MIX, D_MODEL), lambda s: (cur(s), 0)),
                   pl.BlockSpec((TM_MIX * TOK_SUB, LANES), lambda s: (cur(s), 0)),
                   pl.BlockSpec((2 * LANES, TM_MIX), lambda s: (0, cur(s)))),
        scratch_shapes=[
            pltpu.VMEM((TM_MIX, MAIN_COLS), BF16),
            pltpu.VMEM((TM_MIX, LANES), F32),
            pltpu.VMEM((TM_MIX, SSD_INNER + 2 * D_MODEL), BF16),
            pltpu.VMEM((TM_MIX, LANES), F32),
            pltpu.VMEM((SEQ_HALO + TM_MIX, POOL_WIDTH), BF16),
            pltpu.VMEM((CONV_SLABS, CONV_HALO + TM_MIX, LANES), F32),
            pltpu.VMEM((CONV_SLABS, TM_MIX, LANES), F32),
            pltpu.VMEM((SSD_STATE, SSD_INNER), F32),
            pltpu.VMEM((TM_MIX, SSD_INNER), F32),
        ],
        compiler_params=pltpu.CompilerParams(dimension_semantics=("arbitrary",),
                                             vmem_limit_bytes=VMEM_LIMIT_PROMIX),
        name="promix",
    )(x2, mod3, g1, w_main, wd_hi, wd_lo, x2, mod3, *consts)


def _route_kernel(lt_ref, idx_ref, rf_ref, cnt_ref, carry):
    i = pl.program_id(0)
    tr = TR_ROUTE
    ck = ROUTE_CHUNK
    neg = -jnp.inf

    @pl.when(i == 0)
    def _():
        carry[...] = jnp.zeros_like(carry)

    row8 = lax.broadcasted_iota(I32, (8, tr), 0)
    lg = jnp.where(row8 < MOE_GROUPS, lt_ref[0:8, :], neg)
    mg = jnp.max(lg, axis=0, keepdims=True)
    gsel = jnp.min(jnp.where(lg == mg, row8, 8), axis=0, keepdims=True)
    gwt = 1.0 / jnp.sum(jnp.exp(lg - mg), axis=0, keepdims=True)
    rowe = lax.broadcasted_iota(I32, (N_EXPERTS, tr), 0)
    le = jnp.where(lax.shift_right_logical(rowe, 3) == gsel, lt_ref[LANES:LANES + N_EXPERTS, :], neg)
    m1 = jnp.max(le, axis=0, keepdims=True)
    e0 = jnp.min(jnp.where(le == m1, rowe, N_EXPERTS), axis=0, keepdims=True)
    le2 = jnp.where(rowe == e0, neg, le)
    m2 = jnp.max(le2, axis=0, keepdims=True)
    e1 = jnp.min(jnp.where(le2 == m2, rowe, N_EXPERTS), axis=0, keepdims=True)
    t21 = jnp.exp(m2 - m1)
    w0 = gwt / (1.0 + t21)
    w1 = gwt * t21 / (1.0 + t21)
    oh0 = jnp.where(rowe == e0, 1.0, 0.0)
    oh1 = jnp.where(rowe == e1, 1.0, 0.0)
    oh = oh0 + oh1

    kr = lax.broadcasted_iota(I32, (ck, ck), 0)
    kc = lax.broadcasted_iota(I32, (ck, ck), 1)
    before = jnp.where(kr < kc, 1.0, 0.0).astype(BF16)
    wpad = jnp.zeros((LANES - 2, ck), F32)
    for c0 in range(0, tr, ck):
        cs = slice(c0, c0 + ck)
        pre = _dot(oh[:, cs].astype(BF16), before) + carry[:, 0:1]
        idx_ref[2:3, cs] = jnp.sum(pre * oh0[:, cs], axis=0, keepdims=True).astype(I32)
        idx_ref[3:4, cs] = jnp.sum(pre * oh1[:, cs], axis=0, keepdims=True).astype(I32)
        carry[...] = carry[...] + jnp.sum(oh[:, cs], axis=1, keepdims=True)
        rf_ref[cs, :] = jnp.concatenate([w0[:, cs], w1[:, cs], wpad], axis=0).T
    idx_ref[0:1, :] = e0
    idx_ref[1:2, :] = e1
    idx_ref[4:8, :] = jnp.zeros((4, tr), I32)
    cnt_ref[...] = carry[...].astype(I32)


def _route(logits_t):
    t = logits_t.shape[1]
    return pl.pallas_call(
        _route_kernel,
        out_shape=(jax.ShapeDtypeStruct((8, t), I32),
                   jax.ShapeDtypeStruct((t, LANES), F32),
                   jax.ShapeDtypeStruct((N_EXPERTS, LANES), I32)),
        grid=(t // TR_ROUTE,),
        in_specs=[pl.BlockSpec((2 * LANES, TR_ROUTE), lambda i: (0, i))],
        out_specs=(pl.BlockSpec((8, TR_ROUTE), lambda i: (0, i)),
                   pl.BlockSpec((TR_ROUTE, LANES), lambda i: (i, 0)),
                   pl.BlockSpec((N_EXPERTS, LANES), lambda i: (0, 0))),
        scratch_shapes=[pltpu.VMEM((N_EXPERTS, LANES), F32)],
        compiler_params=pltpu.CompilerParams(dimension_semantics=("arbitrary",)),
        name="moe_route",
    )(logits_t)


def _dest_kernel(ps_ref, idx_ref, dest_ref):
    dest = [idx_ref[2:3, :], idx_ref[3:4, :]]
    for e in range(N_EXPERTS):
        for k in range(2):
            dest[k] = dest[k] + jnp.where(idx_ref[k:k + 1, :] == e, ps_ref[e], 0)
    dest_ref[0:1, :] = dest[0] * TOK_SUB
    dest_ref[1:2, :] = dest[1] * TOK_SUB
    dest_ref[2:8, :] = jnp.zeros((6, dest_ref.shape[1]), I32)


def _dest(pad_starts, idx):
    t = idx.shape[1]
    return pl.pallas_call(
        _dest_kernel,
        out_shape=jax.ShapeDtypeStruct((8, t), I32),
        grid_spec=pltpu.PrefetchScalarGridSpec(
            num_scalar_prefetch=1,
            grid=(t // TR_ROUTE,),
            in_specs=[pl.BlockSpec((8, TR_ROUTE), lambda i, ps: (0, i))],
            out_specs=pl.BlockSpec((8, TR_ROUTE), lambda i, ps: (0, i)),
        ),
        compiler_params=pltpu.CompilerParams(dimension_semantics=("arbitrary",)),
        name="moe_dest",
    )(pad_starts, idx)


def _scatter_kernel(pv_ref, pe_ref, nu_ref, d0_ref, d1_ref, h_ref, xb_ref, zblk, sem, zsem):
    i = pl.program_id(0)
    ts = TS_SCATTER
    bm = BM_EXPERT
    nb = xb_ref.shape[0] // (bm * TOK_SUB)

    @pl.when(i == 0)
    def _():
        zblk[...] = jnp.zeros_like(zblk)

        def blk_copy(blk):
            start = pl.multiple_of(blk * (bm * TOK_SUB), bm * TOK_SUB)
            return pltpu.make_async_copy(zblk, xb_ref.at[pl.ds(start, bm * TOK_SUB)], zsem)

        def for_pad_blocks(fn):
            def per_expert(e, carry):
                @pl.when(pv_ref[e] < pe_ref[e])
                def _():
                    fn(blk_copy(lax.div(pe_ref[e], bm) - 1))
                return carry
            lax.fori_loop(0, N_EXPERTS, per_expert, 0)

            def per_block(blk, carry):
                fn(blk_copy(blk))
                return carry
            lax.fori_loop(nu_ref[0], nb, per_block, 0)

        for_pad_blocks(lambda cp: cp.start())
        for_pad_blocks(lambda cp: cp.wait())

    def issue(j, carry):
        for u in range(ISSUE_UNROLL):
            r = j * ISSUE_UNROLL + u
            for k in range(2):
                pltpu.make_async_copy(_token_tile(h_ref, r), _token_tile_at(xb_ref, (d0_ref, d1_ref)[k][i * ts + r]),
                                      sem.at[k]).start(priority=k)
        return carry

    lax.fori_loop(0, ts // ISSUE_UNROLL, issue, 0)
    for k in range(2):
        pltpu.make_async_copy(h_ref, xb_ref.at[pl.ds(0, ts * TOK_SUB)], sem.at[k]).wait()


def _scatter(pad_valid_end, pad_ends, n_used, dest0, dest1, h2t, n_rows):
    t = h2t.shape[0] // TOK_SUB
    return pl.pallas_call(
        _scatter_kernel,
        out_shape=jax.ShapeDtypeStruct((n_rows * TOK_SUB, LANES), U32),
        grid_spec=pltpu.PrefetchScalarGridSpec(
            num_scalar_prefetch=5,
            grid=(t // TS_SCATTER,),
            in_specs=[pl.BlockSpec((TS_SCATTER * TOK_SUB, LANES), lambda i, *_: (i, 0))],
            out_specs=pl.BlockSpec(memory_space=pl.ANY),
            scratch_shapes=[pltpu.VMEM((BM_EXPERT * TOK_SUB, LANES), U32),
                            pltpu.SemaphoreType.DMA((2,)),
                            pltpu.SemaphoreType.DMA(())],
        ),
        compiler_params=pltpu.CompilerParams(dimension_semantics=("arbitrary",),
                                             disable_bounds_checks=True),
        name="moe_scatter",
    )(pad_valid_end, pad_ends, n_used, dest0, dest1, h2t)


def _expert_kernel(be_ref, nu_ref, x_ref, w13_ref, w2_ref, y_ref, w13b, w2b):
    i = pl.program_id(0)

    @pl.when((i == 0) | (be_ref[i] != be_ref[jnp.maximum(i - 1, 0)]))
    def _():
        w13b[...] = w13_ref[0].astype(BF16)
        w2b[...] = w2_ref[0].astype(BF16)

    @pl.when(i < nu_ref[0])
    def _():
        x = _load_token_tiles(x_ref).astype(BF16)
        a = _dot(x, w13b[:, 0:EXPERT_FF])
        g = _dot(x, w13b[:, EXPERT_FF:2 * EXPERT_FF])
        _store_token_tiles(y_ref, _dot((_silu(a) * g).astype(BF16), w2b[...]))

    @pl.when(i >= nu_ref[0])
    def _():
        y_ref[...] = jnp.zeros_like(y_ref)


def _experts(block_expert, n_used, xbuf, w13, w2):
    nb = xbuf.shape[0] // (BM_EXPERT * TOK_SUB)
    used = lambda i, nu: jnp.minimum(i, nu[0] - 1)
    return pl.pallas_call(
        _expert_kernel,
        out_shape=jax.ShapeDtypeStruct(xbuf.shape, U32),
        grid_spec=pltpu.PrefetchScalarGridSpec(
            num_scalar_prefetch=2,
            grid=(nb,),
            in_specs=[
                pl.BlockSpec((BM_EXPERT * TOK_SUB, LANES), lambda i, be, nu: (used(i, nu), 0)),
                pl.BlockSpec((1, D_MODEL, 2 * EXPERT_FF), lambda i, be, nu: (be[i], 0, 0)),
                pl.BlockSpec((1, EXPERT_FF, D_MODEL), lambda i, be, nu: (be[i], 0, 0)),
            ],
            out_specs=pl.BlockSpec((BM_EXPERT * TOK_SUB, LANES), lambda i, be, nu: (i, 0)),
            scratch_shapes=[pltpu.VMEM((D_MODEL, 2 * EXPERT_FF), BF16),
                            pltpu.VMEM((EXPERT_FF, D_MODEL), BF16)],
        ),
        compiler_params=pltpu.CompilerParams(dimension_semantics=("arbitrary",),
                                             vmem_limit_bytes=VMEM_LIMIT),
        name="moe_experts",
    )(block_expert, n_used, xbuf, w13, w2)


def _combine_kernel(d0_ref, d1_ref, x_ref, rf_ref, mod_ref, fg_ref, yb_ref, out_ref, gbuf, sem):
    i = pl.program_id(0)
    n = pl.num_programs(0)
    tc = TC_COMBINE
    slot = lax.rem(i, 2)

    def start_gather(tile, s):
        def issue(j, carry):
            for u in range(ISSUE_UNROLL):
                r = j * ISSUE_UNROLL + u
                for k in range(2):
                    src = (d0_ref, d1_ref)[k][tile * tc + r]
                    pltpu.make_async_copy(_token_tile_at(yb_ref, src), _token_tile(gbuf.at[s, k], r),
                                          sem.at[s, k]).start(priority=k)
            return carry
        lax.fori_loop(0, tc // ISSUE_UNROLL, issue, 0)

    @pl.when(i == 0)
    def _():
        start_gather(0, 0)

    @pl.when(i + 1 < n)
    def _():
        start_gather(i + 1, 1 - slot)

    for k in range(2):
        pltpu.make_async_copy(yb_ref.at[pl.ds(0, tc * TOK_SUB)], gbuf.at[slot, k], sem.at[slot, k]).wait()
    moe = (rf_ref[:, 0:1] * _load_token_tiles(gbuf.at[slot, 0])
           + rf_ref[:, 1:2] * _load_token_tiles(gbuf.at[slot, 1]))
    out_ref[...] = _rmsnorm(x_ref[...] + mod_ref[0, 5:6, :] * moe, fg_ref[...])


def _combine(dest0, dest1, x_new, rf, mod3, fg, ybuf, seq):
    t = x_new.shape[0]
    n = t // TC_COMBINE
    tiles_per_seq = seq // TC_COMBINE
    return pl.pallas_call(
        _combine_kernel,
        out_shape=jax.ShapeDtypeStruct((t, D_MODEL), F32),
        grid_spec=pltpu.PrefetchScalarGridSpec(
            num_scalar_prefetch=2,
            grid=(n,),
            in_specs=[
                pl.BlockSpec((TC_COMBINE, D_MODEL), lambda i, *_: (i, 0)),
                pl.BlockSpec((TC_COMBINE, LANES), lambda i, *_: (i, 0)),
                pl.BlockSpec((1, 6, D_MODEL), lambda i, *_: (i // tiles_per_seq, 0, 0)),
                pl.BlockSpec((1, D_MODEL), lambda i, *_: (0, 0)),
                pl.BlockSpec(memory_space=pl.ANY),
            ],
            out_specs=pl.BlockSpec((TC_COMBINE, D_MODEL), lambda i, *_: (i, 0)),
            scratch_shapes=[pltpu.VMEM((2, 2, TC_COMBINE * TOK_SUB, LANES), U32),
                            pltpu.SemaphoreType.DMA((2, 2))],
        ),
        compiler_params=pltpu.CompilerParams(dimension_semantics=("arbitrary",),
                                             disable_bounds_checks=True),
        name="moe_combine",
    )(dest0, dest1, x_new, rf, mod3, fg, ybuf)


def _pad_lanes(a, n=LANES):
    return jnp.pad(a, [(0, 0)] * (a.ndim - 1) + [(0, n - a.shape[-1])])


def _selection_matrices():
    q = SSD_CHUNK
    rel = jnp.arange(SEQ_HALO + q)[None, :] - jnp.arange(q)[:, None] - SEQ_HALO
    band = jnp.stack([(rel <= 0) & (rel > -w) for w in POOL_WINDOWS]).astype(BF16)
    expand = (jnp.arange(LANES)[:, None] == jnp.arange(SSD_INNER)[None, :] // SSD_HEAD_DIM).astype(BF16)
    return band, expand


def _layer(x2, c, bsz, seq, ada_w, ada_b, norm1_g, w_in, pool_w, pool_scale, conv_w, conv_b, dt_bias,
           a_log, d_skip, ssd_norm_g, w_branch_pool, w_branch_ssd, w_out, norm2_g, router_group_w,
           router_group_b, router_expert_w, router_expert_b, w13, w2):
    t = bsz * seq
    mod3 = _ada(c, ada_w, ada_b).reshape(bsz, 6, D_MODEL)

    i2 = POOL_WIDTH + SSD_INNER + SSD_CONV_DIM
    i3 = i2 + SSD_HEADS
    w_main = jnp.concatenate([w_in[:, :i2], w_in[:, i3:]], axis=1).astype(BF16)
    wd = _pad_lanes(w_in[:, i2:i3])
    wd_hi = wd.astype(BF16)
    wd_lo = (wd - wd_hi.astype(F32)).astype(BF16)

    wr = jnp.concatenate([_pad_lanes(router_group_w), _pad_lanes(router_expert_w)], axis=1).T
    wr_hi = wr.astype(BF16)
    wr_lo = (wr - wr_hi.astype(F32)).astype(BF16)
    br = jnp.concatenate([_pad_lanes(router_group_b.reshape(1, -1)),
                          _pad_lanes(router_expert_b.reshape(1, -1))], axis=1)
    br = jnp.broadcast_to(br.reshape(2 * LANES, 1), (2 * LANES, LANES))
    consts = [
        pool_w.astype(BF16),
        pool_scale.reshape(1, POOL_WIDTH),
        conv_w,
        conv_b.reshape(1, SSD_CONV_DIM),
        _pad_lanes(dt_bias.reshape(1, SSD_HEADS)),
        jnp.broadcast_to(jnp.pad(a_log, (0, LANES - SSD_HEADS))[:, None], (LANES, LANES)),
        jnp.repeat(d_skip, SSD_HEAD_DIM).reshape(1, SSD_INNER),
        ssd_norm_g.reshape(1, SSD_INNER),
        w_branch_pool.astype(BF16),
        w_branch_ssd.astype(BF16),
        w_out.astype(BF16),
        norm2_g.reshape(1, D_MODEL),
        wr_hi, wr_lo, br,
        *_selection_matrices(),
    ]
    x_new, h2, logits_t = _promix(x2, mod3, norm1_g.reshape(1, D_MODEL), w_main, wd_hi, wd_lo, consts, seq)
    idx, rf, cnt = _route(logits_t)

    counts = cnt[:, 0]
    padded = ((counts + BM_EXPERT - 1) // BM_EXPERT) * BM_EXPERT
    pad_ends = jnp.cumsum(padded)
    pad_starts = (pad_ends - padded).astype(I32)
    n_rows = 2 * t + N_EXPERTS * BM_EXPERT
    nb = n_rows // BM_EXPERT
    block_row0 = jnp.arange(nb, dtype=I32) * BM_EXPERT
    block_expert = jnp.minimum(jnp.sum((pad_ends[None, :] <= block_row0[:, None]).astype(I32), axis=1),
                               N_EXPERTS - 1)
    n_used = (pad_ends[-1:] // BM_EXPERT).astype(I32)

    dest = _dest(pad_starts, idx)
    dest0, dest1 = dest[0], dest[1]
    xbuf = _scatter(pad_starts + counts, pad_ends.astype(I32), n_used, dest0, dest1, h2, n_rows)
    ybuf = _experts(block_expert, n_used, xbuf, w13, w2)
    return dest0, dest1, x_new, rf, mod3, ybuf


def kernel(x, c, ada_w, ada_b, norm1_g, w_in, pool_w, pool_scale, conv_w, conv_b, dt_bias, a_log, d_skip, ssd_norm_g, w_branch_pool, w_branch_ssd, w_out, norm2_g, router_group_w, router_group_b, router_expert_w, router_expert_b, w13, w2, final_norm_g):
    bsz, seq, d = x.shape
    depth = ada_w.shape[0]
    assert depth == 1 and d == D_MODEL and seq % TM_MIX == 0
    x2 = x.reshape(bsz * seq, d)
    l = 0
    dest0, dest1, x_new, rf, mod3, ybuf = _layer(
        x2, c, bsz, seq, ada_w[l], ada_b[l], norm1_g[l], w_in[l], pool_w[l], pool_scale[l], conv_w[l],
        conv_b[l], dt_bias[l], a_log[l], d_skip[l], ssd_norm_g[l], w_branch_pool[l], w_branch_ssd[l],
        w_out[l], norm2_g[l], router_group_w[l], router_group_b[l], router_expert_w[l],
        router_expert_b[l], w13[l], w2[l])
    out = _combine(dest0, dest1, x_new, rf, mod3, final_norm_g.reshape(1, d), ybuf, seq)
    return out.reshape(bsz, seq, d)
```

```python
import functools

import jax
import jax.numpy as jnp
from jax import lax
from jax.experimental import pallas as pl
from jax.experimental.pallas import tpu as pltpu

F32 = jnp.float32
BF16 = jnp.bfloat16
I32 = jnp.int32
U32 = jnp.uint32

D_MODEL = 1024
POOL_WIDTH = 1024
POOL_GROUPS = 4
POOL_GROUP_WIDTH = 256
POOL_WINDOWS = (2, 4, 8, 16)
SSD_INNER = 2048
SSD_HEAD_DIM = 64
SSD_HEADS = 32
SSD_GROUPS = 4
SSD_STATE = 128
SSD_CONV = 4
SSD_CHUNK = 128
SSD_CONV_DIM = SSD_INNER + 2 * SSD_GROUPS * SSD_STATE
N_EXPERTS = 32
EXPERTS_PER_GROUP = 8
MOE_GROUPS = 4
EXPERT_FF = 512
EPS = 1e-6

LANES = 128
TOK_SUB = D_MODEL // (2 * LANES)
MAIN_COLS = POOL_WIDTH + SSD_INNER + SSD_CONV_DIM + 2 * D_MODEL
COL_Z = POOL_WIDTH
COL_XBC = COL_Z + SSD_INNER
COL_GATE = COL_XBC + SSD_CONV_DIM

TM_MIX = 256
PROJ_PIECE = 256
SEQ_HALO = SSD_CHUNK
CONV_HALO = 8
CONV_SLABS = SSD_CONV_DIM // LANES
LOG2E = 1.4426950408889634
TR_ROUTE = 2048
ROUTE_CHUNK = 256
TS_SCATTER = 2048
ISSUE_UNROLL = 16
BM_EXPERT = 512
TC_COMBINE = 256
COMBINE_SLOTS = 3
VMEM_LIMIT = 56 * 1024 * 1024
VMEM_LIMIT_PROMIX = 60 * 1024 * 1024


def _dot(a, b):
    return jnp.dot(a, b, preferred_element_type=F32)


def _hi_lo(x):
    hi = x.astype(BF16)
    lo = (x - hi.astype(F32)).astype(BF16)
    return hi, lo


def _split3(x):
    hi = x.astype(BF16)
    r = x - hi.astype(F32)
    mid = r.astype(BF16)
    lo = (r - mid.astype(F32)).astype(BF16)
    return hi, mid, lo


def _dot3(a, b):
    ah, al = _hi_lo(a)
    bh, bl = _hi_lo(b)
    return _dot(ah, bh) + _dot(ah, bl) + _dot(al, bh)


def _sigmoid(x):
    return 1.0 / (1.0 + jnp.exp(-x))


def _silu(x):
    return x * _sigmoid(x)


def _softplus(x):
    return jnp.maximum(x, 0.0) + jnp.log1p(jnp.exp(-jnp.abs(x)))


def _rmsnorm(x, g):
    return x * lax.rsqrt(jnp.mean(x * x, axis=-1, keepdims=True) + EPS) * g


def _bf16_bits(v):
    return lax.bitcast_convert_type(v.astype(BF16).astype(F32), U32) & jnp.uint32(0xFFFF0000)


def _store_token_tiles(ref, v):
    n = v.shape[0]
    half = v.shape[1] // 2
    words = _bf16_bits(v[:, half:]) | (_bf16_bits(v[:, :half]) >> 16)
    for j in range(TOK_SUB):
        ref[pl.ds(j, n, stride=TOK_SUB), :] = words[:, j * LANES:(j + 1) * LANES]


def _load_token_tiles(ref):
    n = ref.shape[0] // TOK_SUB
    words = jnp.concatenate([ref[pl.ds(j, n, stride=TOK_SUB), :] for j in range(TOK_SUB)], axis=1)
    lo = lax.bitcast_convert_type(words << 16, F32)
    hi = lax.bitcast_convert_type(words & jnp.uint32(0xFFFF0000), F32)
    return jnp.concatenate([lo, hi], axis=1)


def _token_tile(ref, row):
    return ref.at[pl.ds(pl.multiple_of(row * TOK_SUB, TOK_SUB), TOK_SUB)]


def _token_tile_at(ref, first_row):
    return ref.at[pl.ds(pl.multiple_of(first_row, TOK_SUB), TOK_SUB)]


def _ada_kernel(c_ref, w_ref, b_ref, o_ref):
    o_ref[...] = _dot3(_silu(c_ref[...]), w_ref[...]) + b_ref[...]


def _ada(c, w, b):
    bsz = c.shape[0]
    n = w.shape[1]
    tn = 1024
    return pl.pallas_call(
        _ada_kernel,
        out_shape=jax.ShapeDtypeStruct((bsz, n), F32),
        grid=(n // tn,),
        in_specs=[
            pl.BlockSpec((bsz, D_MODEL), lambda j: (0, 0)),
            pl.BlockSpec((D_MODEL, tn), lambda j: (0, j)),
            pl.BlockSpec((1, tn), lambda j: (0, j)),
        ],
        out_specs=pl.BlockSpec((bsz, tn), lambda j: (0, j)),
        compiler_params=pltpu.CompilerParams(dimension_semantics=("arbitrary",)),
        name="ada",
    )(c, w, b.reshape(1, n))


N_MIX_CONSTS = 17


def _promix_kernel(tiles_per_seq, xa_ref, moda_ref, g_ref, w_ref, wdh_ref, wdl_ref, xb_ref, modb_ref, *rest):
    consts = rest[:N_MIX_CONSTS]
    xnew_ref, h2_ref, lt_ref = rest[N_MIX_CONSTS:N_MIX_CONSTS + 3]
    pbuf, dtbuf, zg, dts, xp_ext, xbc_ext, act, state, yssd = rest[N_MIX_CONSTS + 3:]
    s = pl.program_id(0)
    j = lax.rem(jnp.maximum(s - 1, 0), tiles_per_seq)
    tm = TM_MIX

    @pl.when(s == 0)
    def _():
        pbuf[...] = jnp.zeros_like(pbuf)
        dtbuf[...] = jnp.zeros_like(dtbuf)

    @pl.when(j == 0)
    def _():
        xp_ext[0:SEQ_HALO, :] = jnp.zeros((SEQ_HALO, POOL_WIDTH), BF16)
        xbc_ext[:, 0:CONV_HALO, :] = jnp.zeros((CONV_SLABS, CONV_HALO, LANES), F32)
        state[...] = jnp.zeros_like(state)

    xp_ext[SEQ_HALO:SEQ_HALO + tm, :] = pbuf[:, 0:POOL_WIDTH]
    for sl in range(CONV_SLABS):
        xbc_ext[sl, CONV_HALO:CONV_HALO + tm, :] = (
            pbuf[:, COL_XBC + sl * LANES:COL_XBC + (sl + 1) * LANES].astype(F32))
    zg[:, 0:SSD_INNER] = pbuf[:, COL_Z:COL_Z + SSD_INNER]
    zg[:, SSD_INNER:SSD_INNER + 2 * D_MODEL] = pbuf[:, COL_GATE:COL_GATE + 2 * D_MODEL]
    dts[...] = dtbuf[...]

    h = _rmsnorm(xa_ref[...], g_ref[...]) * (1.0 + moda_ref[0, 1:2, :]) + moda_ref[0, 0:1, :]
    hb, hl = _hi_lo(h)

    def project(c0):
        def piece():
            pbuf[:, c0:c0 + PROJ_PIECE] = _dot(hb, w_ref[:, c0:c0 + PROJ_PIECE]).astype(BF16)
        return piece

    def project_dt():
        dtbuf[...] = _dot(hb, wdh_ref[...]) + _dot(hb, wdl_ref[...]) + _dot(hl, wdh_ref[...])

    fill = _Filler([project(c0) for c0 in range(0, MAIN_COLS, PROJ_PIECE)] + [project_dt])
    _mix_main(fill, j, zg, dts, xb_ref, modb_ref, *consts,
              xnew_ref, h2_ref, lt_ref, xp_ext, xbc_ext, act, state, yssd)


class _Filler:
    def __init__(self, thunks):
        self.pending = list(thunks)

    def __call__(self, n):
        n = len(self.pending) if n is None else n
        for _ in range(min(n, len(self.pending))):
            self.pending.pop(0)()


def _mix_main(fill, j, zg_ref, dt_ref, x_ref, mod_ref, poolw_ref, pools_ref, convw_ref, convb_ref,
              dtb_ref, alog_ref, dskip_ref, ng_ref, wbp_ref, wbs_ref, wout_ref, n2g_ref,
              wrh_ref, wrl_ref, br_ref, band_ref, expand_ref,
              xnew_ref, h2_ref, lt_ref,
              xp_ext, xbc_ext, act, state, yssd):
    tm = TM_MIX
    q = SSD_CHUNK
    halo = SEQ_HALO


    gw = POOL_GROUP_WIDTH
    pos = j * tm + lax.broadcasted_iota(I32, (tm, gw), 0)
    yp = []
    for gi, w in enumerate(POOL_WINDOWS):
        cols = slice(gi * gw, (gi + 1) * gw)
        acc = jnp.concatenate([_dot(band_ref[gi], xp_ext[r0:r0 + halo + q, cols])
                               for r0 in range(0, tm, q)], axis=0)
        cur = xp_ext[halo:halo + tm, cols].astype(F32)
        d = acc / jnp.minimum(pos + 1, w).astype(F32) - cur
        yp.append(_dot(d.astype(BF16), poolw_ref[gi]) * pools_ref[:, cols])
        fill(1)
    bp = _dot(jnp.concatenate(yp, axis=1).astype(BF16), wbp_ref[...])

    first = CONV_HALO - (SSD_CONV - 1)
    for sl in range(CONV_SLABS):
        cols = slice(sl * LANES, (sl + 1) * LANES)
        for par in range(2):
            acc = convb_ref[:, cols] + convw_ref[0:1, cols] * xbc_ext[sl, pl.ds(first + par, tm // 2, stride=2), :]
            for k in range(1, SSD_CONV):
                acc = acc + convw_ref[k:k + 1, cols] * xbc_ext[sl, pl.ds(first + par + k, tm // 2, stride=2), :]
            act[sl, pl.ds(par, tm // 2, stride=2), :] = _silu(acc)
        if sl % 4 == 3:
            fill(1)

    xp_ext[0:halo, :] = xp_ext[tm:tm + halo, :]
    xbc_ext[:, 0:CONV_HALO, :] = xbc_ext[:, tm:tm + CONV_HALO, :]

    lane = lax.broadcasted_iota(I32, (q, q), 1)
    row = lax.broadcasted_iota(I32, (q, q), 0)
    causal = row >= lane
    triu = jnp.where(row <= lane, 1.0, 0.0).astype(BF16)
    lo_half = lane < SSD_HEAD_DIM
    a_rows = -jnp.exp(alog_ref[...])
    col_b = SSD_INNER
    col_c = SSD_INNER + SSD_GROUPS * SSD_STATE
    gcw = SSD_INNER // SSD_GROUPS
    for c in range(tm // q):
        r0 = c * q
        dtv = _softplus(dt_ref[r0:r0 + q, :] + dtb_ref[...])
        dt_t = dtv.T
        p1, p2, p3 = _split3(dt_t * a_rows)
        acs_t = _dot(p1, triu) + _dot(p2, triu) + _dot(p3, triu)
        acs = acs_t.T
        acs2 = acs * LOG2E
        acs2_t = acs_t * LOG2E
        expand = expand_ref[...]
        dt_x = _dot(jnp.concatenate(_hi_lo(dtv), axis=1), jnp.concatenate([expand, expand], axis=0))
        fx = _dot(jnp.concatenate([jnp.exp(acs).astype(BF16),
                                   jnp.exp(acs[q - 1:q, :] - acs).astype(BF16)], axis=0), expand)
        ea_x = fx[0:q]
        dec_x = fx[q:2 * q]
        xdt = jnp.concatenate([act[sl, r0:r0 + q, :] for sl in range(SSD_INNER // LANES)], axis=1) * dt_x
        xb = xdt.astype(BF16)
        xdb = (xdt * dec_x).astype(BF16)
        s_all = state[...]
        sb = s_all.astype(BF16)
        for g in range(SSD_GROUPS):
            gcols = slice(g * gcw, (g + 1) * gcw)
            bg = act[col_b // LANES + g, r0:r0 + q, :]
            cg = act[col_c // LANES + g, r0:r0 + q, :].astype(BF16)
            cbm = lax.dot_general(cg, bg.astype(BF16),
                                  (((1,), (1,)), ((), ())), preferred_element_type=F32)
            y_off = _dot(cg, sb[:, gcols]) * ea_x[:, gcols]
            ds = _dot(bg.T.astype(BF16), xdb[:, gcols])
            state[:, gcols] = s_all[:, gcols] * ea_x[q - 1:q, gcols] + ds
            y_diag = []
            for pq in range(gcw // LANES):
                pair = g * (gcw // LANES) + pq
                ms = []
                for hh in range(2):
                    h = 2 * pair + hh
                    a_l = jnp.broadcast_to(acs2[:, h:h + 1], (q, q))
                    a_s = jnp.broadcast_to(acs2_t[h:h + 1, :], (q, q))
                    ms.append((jnp.where(causal, jnp.exp2(a_l - a_s), 0.0) * cbm).astype(BF16))
                x_pair = xb[:, pair * LANES:(pair + 1) * LANES]
                zero = jnp.zeros_like(x_pair)
                y_diag.append(_dot(jnp.concatenate(ms, axis=1),
                                   jnp.concatenate([jnp.where(lo_half, x_pair, zero),
                                                    jnp.where(lo_half, zero, x_pair)], axis=0)))
                fill(1 if pq % 2 == 0 else 0)
            yssd[r0:r0 + q, gcols] = jnp.concatenate(y_diag, axis=1) + y_off

    y = yssd[...] + dskip_ref[...] * jnp.concatenate([act[sl] for sl in range(SSD_INNER // LANES)], axis=1)
    y = y * _silu(zg_ref[:, 0:SSD_INNER].astype(F32))
    fill(1)
    y = _rmsnorm(y, ng_ref[...])
    fill(1)
    bs = _dot(y.astype(BF16), wbs_ref[...])
    g_pool = _sigmoid(zg_ref[:, SSD_INNER:SSD_INNER + D_MODEL].astype(F32))
    g_ssd = _sigmoid(zg_ref[:, SSD_INNER + D_MODEL:SSD_INNER + 2 * D_MODEL].astype(F32))
    fill(1)
    u = g_pool * bp + g_ssd * bs
    x_new = x_ref[...] + mod_ref[0, 2:3, :] * _dot(u.astype(BF16), wout_ref[...])
    xnew_ref[...] = x_new
    fill(None)

    h2 = _rmsnorm(x_new, n2g_ref[...]) * (1.0 + mod_ref[0, 4:5, :]) + mod_ref[0, 3:4, :]
    _store_token_tiles(h2_ref, h2)
    hh, hl = _hi_lo(h2)
    nt = (((1,), (1,)), ((), ()))
    logits_t = (lax.dot_general(wrh_ref[...], hh, nt, preferred_element_type=F32)
                + lax.dot_general(wrl_ref[...], hh, nt, preferred_element_type=F32)
                + lax.dot_general(wrh_ref[...], hl, nt, preferred_element_type=F32))
    lt_ref[...] = logits_t + jnp.concatenate([br_ref[...]] * (tm // LANES), axis=1)


def _promix(x2, mod3, g1, w_main, wd_hi, wd_lo, consts, seq):
    assert len(consts) == N_MIX_CONSTS
    t = x2.shape[0]
    n = t // TM_MIX
    tiles = seq // TM_MIX
    nxt = lambda s: jnp.minimum(s, n - 1)
    cur = lambda s: jnp.maximum(s - 1, 0)

    def cspec(a):
        zeros = (0,) * a.ndim
        return pl.BlockSpec(a.shape, lambda s: zeros, pipeline_mode=pl.Buffered(1))

    return pl.pallas_call(
        functools.partial(_promix_kernel, tiles),
        out_shape=(jax.ShapeDtypeStruct((t, D_MODEL), F32),
                   jax.ShapeDtypeStruct((t * TOK_SUB, LANES), U32),
                   jax.ShapeDtypeStruct((2 * LANES, t), F32)),
        grid=(n + 1,),
        in_specs=[
            pl.BlockSpec((TM_MIX, D_MODEL), lambda s: (nxt(s), 0)),
            pl.BlockSpec((1, 6, D_MODEL), lambda s: (nxt(s) // tiles, 0, 0)),
            cspec(g1), cspec(w_main), cspec(wd_hi), cspec(wd_lo),
            pl.BlockSpec((TM_MIX, D_MODEL), lambda s: (cur(s), 0)),
            pl.BlockSpec((1, 6, D_MODEL), lambda s: (cur(s) // tiles, 0, 0)),
        ] + [cspec(a) for a in consts],
        out_specs=(pl.BlockSpec((TM_MIX, D_MODEL), lambda s: (cur(s), 0)),
                   pl.BlockSpec((TM_MIX * TOK_SUB, LANES), lambda s: (cur(s), 0)),
                   pl.BlockSpec((2 * LANES, TM_MIX), lambda s: (0, cur(s)))),
        scratch_shapes=[
            pltpu.VMEM((TM_MIX, MAIN_COLS), BF16),
            pltpu.VMEM((TM_MIX, LANES), F32),
            pltpu.VMEM((TM_MIX, SSD_INNER + 2 * D_MODEL), BF16),
            pltpu.VMEM((TM_MIX, LANES), F32),
            pltpu.VMEM((SEQ_HALO + TM_MIX, POOL_WIDTH), BF16),
            pltpu.VMEM((CONV_SLABS, CONV_HALO + TM_MIX, LANES), F32),
            pltpu.VMEM((CONV_SLABS, TM_MIX, LANES), F32),
            pltpu.VMEM((SSD_STATE, SSD_INNER), F32),
            pltpu.VMEM((TM_MIX, SSD_INNER), F32),
        ],
        compiler_params=pltpu.CompilerParams(dimension_semantics=("arbitrary",),
                                             vmem_limit_bytes=VMEM_LIMIT_PROMIX),
        name="promix",
    )(x2, mod3, g1, w_main, wd_hi, wd_lo, x2, mod3, *consts)


def _route_kernel(lt_ref, idx_ref, rf_ref, cnt_ref, carry):
    i = pl.program_id(0)
    tr = TR_ROUTE
    ck = ROUTE_CHUNK
    neg = -jnp.inf

    @pl.when(i == 0)
    def _():
        carry[...] = jnp.zeros_like(carry)

    row8 = lax.broadcasted_iota(I32, (8, tr), 0)
    lg = jnp.where(row8 < MOE_GROUPS, lt_ref[0:8, :], neg)
    mg = jnp.max(lg, axis=0, keepdims=True)
    gsel = jnp.min(jnp.where(lg == mg, row8, 8), axis=0, keepdims=True)
    gwt = 1.0 / jnp.sum(jnp.exp(lg - mg), axis=0, keepdims=True)
    rowe = lax.broadcasted_iota(I32, (N_EXPERTS, tr), 0)
    le = jnp.where(lax.shift_right_logical(rowe, 3) == gsel, lt_ref[LANES:LANES + N_EXPERTS, :], neg)
    m1 = jnp.max(le, axis=0, keepdims=True)
    e0 = jnp.min(jnp.where(le == m1, rowe, N_EXPERTS), axis=0, keepdims=True)
    le2 = jnp.where(rowe == e0, neg, le)
    m2 = jnp.max(le2, axis=0, keepdims=True)
    e1 = jnp.min(jnp.where(le2 == m2, rowe, N_EXPERTS), axis=0, keepdims=True)
    t21 = jnp.exp(m2 - m1)
    w0 = gwt / (1.0 + t21)
    w1 = gwt * t21 / (1.0 + t21)
    oh0 = jnp.where(rowe == e0, 1.0, 0.0)
    oh1 = jnp.where(rowe == e1, 1.0, 0.0)
    oh = oh0 + oh1

    kr = lax.broadcasted_iota(I32, (ck, ck), 0)
    kc = lax.broadcasted_iota(I32, (ck, ck), 1)
    before = jnp.where(kr < kc, 1.0, 0.0).astype(BF16)
    wpad = jnp.zeros((LANES - 2, ck), F32)
    for c0 in range(0, tr, ck):
        cs = slice(c0, c0 + ck)
        pre = _dot(oh[:, cs].astype(BF16), before) + carry[:, 0:1]
        idx_ref[2:3, cs] = jnp.sum(pre * oh0[:, cs], axis=0, keepdims=True).astype(I32)
        idx_ref[3:4, cs] = jnp.sum(pre * oh1[:, cs], axis=0, keepdims=True).astype(I32)
        carry[...] = carry[...] + jnp.sum(oh[:, cs], axis=1, keepdims=True)
        rf_ref[cs, :] = jnp.concatenate([w0[:, cs], w1[:, cs], wpad], axis=0).T
    idx_ref[0:1, :] = e0
    idx_ref[1:2, :] = e1
    idx_ref[4:8, :] = jnp.zeros((4, tr), I32)
    cnt_ref[...] = carry[...].astype(I32)


def _route(logits_t):
    t = logits_t.shape[1]
    return pl.pallas_call(
        _route_kernel,
        out_shape=(jax.ShapeDtypeStruct((8, t), I32),
                   jax.ShapeDtypeStruct((t, LANES), F32),
                   jax.ShapeDtypeStruct((N_EXPERTS, LANES), I32)),
        grid=(t // TR_ROUTE,),
        in_specs=[pl.BlockSpec((2 * LANES, TR_ROUTE), lambda i: (0, i))],
        out_specs=(pl.BlockSpec((8, TR_ROUTE), lambda i: (0, i)),
                   pl.BlockSpec((TR_ROUTE, LANES), lambda i: (i, 0)),
                   pl.BlockSpec((N_EXPERTS, LANES), lambda i: (0, 0))),
        scratch_shapes=[pltpu.VMEM((N_EXPERTS, LANES), F32)],
        compiler_params=pltpu.CompilerParams(dimension_semantics=("arbitrary",)),
        name="moe_route",
    )(logits_t)


def _dest_kernel(ps_ref, idx_ref, dest_ref):
    dest = [idx_ref[2:3, :], idx_ref[3:4, :]]
    for e in range(N_EXPERTS):
        for k in range(2):
            dest[k] = dest[k] + jnp.where(idx_ref[k:k + 1, :] == e, ps_ref[e], 0)
    dest_ref[0:1, :] = dest[0] * TOK_SUB
    dest_ref[1:2, :] = dest[1] * TOK_SUB
    dest_ref[2:8, :] = jnp.zeros((6, dest_ref.shape[1]), I32)


def _dest(pad_starts, idx):
    t = idx.shape[1]
    return pl.pallas_call(
        _dest_kernel,
        out_shape=jax.ShapeDtypeStruct((8, t), I32),
        grid_spec=pltpu.PrefetchScalarGridSpec(
            num_scalar_prefetch=1,
            grid=(t // TR_ROUTE,),
            in_specs=[pl.BlockSpec((8, TR_ROUTE), lambda i, ps: (0, i))],
            out_specs=pl.BlockSpec((8, TR_ROUTE), lambda i, ps: (0, i)),
        ),
        compiler_params=pltpu.CompilerParams(dimension_semantics=("arbitrary",)),
        name="moe_dest",
    )(pad_starts, idx)


def _scatter_kernel(pv_ref, pe_ref, nu_ref, d0_ref, d1_ref, h_ref, xb_ref, zblk, sem, zsem):
    i = pl.program_id(0)
    ts = TS_SCATTER
    bm = BM_EXPERT
    nb = xb_ref.shape[0] // (bm * TOK_SUB)

    @pl.when(i == 0)
    def _():
        zblk[...] = jnp.zeros_like(zblk)

        def blk_copy(blk):
            start = pl.multiple_of(blk * (bm * TOK_SUB), bm * TOK_SUB)
            return pltpu.make_async_copy(zblk, xb_ref.at[pl.ds(start, bm * TOK_SUB)], zsem)

        def for_pad_blocks(fn):
            def per_expert(e, carry):
                @pl.when(pv_ref[e] < pe_ref[e])
                def _():
                    fn(blk_copy(lax.div(pe_ref[e], bm) - 1))
                return carry
            lax.fori_loop(0, N_EXPERTS, per_expert, 0)

            def per_block(blk, carry):
                fn(blk_copy(blk))
                return carry
            lax.fori_loop(nu_ref[0], nb, per_block, 0)

        for_pad_blocks(lambda cp: cp.start())
        for_pad_blocks(lambda cp: cp.wait())

    def issue(j, carry):
        for u in range(ISSUE_UNROLL):
            r = j * ISSUE_UNROLL + u
            for k in range(2):
                pltpu.make_async_copy(_token_tile(h_ref, r), _token_tile_at(xb_ref, (d0_ref, d1_ref)[k][i * ts + r]),
                                      sem.at[k]).start(priority=k)
        return carry

    lax.fori_loop(0, ts // ISSUE_UNROLL, issue, 0)
    for k in range(2):
        pltpu.make_async_copy(h_ref, xb_ref.at[pl.ds(0, ts * TOK_SUB)], sem.at[k]).wait()


def _scatter(pad_valid_end, pad_ends, n_used, dest0, dest1, h2t, n_rows):
    t = h2t.shape[0] // TOK_SUB
    return pl.pallas_call(
        _scatter_kernel,
        out_shape=jax.ShapeDtypeStruct((n_rows * TOK_SUB, LANES), U32),
        grid_spec=pltpu.PrefetchScalarGridSpec(
            num_scalar_prefetch=5,
            grid=(t // TS_SCATTER,),
            in_specs=[pl.BlockSpec((TS_SCATTER * TOK_SUB, LANES), lambda i, *_: (i, 0))],
            out_specs=pl.BlockSpec(memory_space=pl.ANY),
            scratch_shapes=[pltpu.VMEM((BM_EXPERT * TOK_SUB, LANES), U32),
                            pltpu.SemaphoreType.DMA((2,)),
                            pltpu.SemaphoreType.DMA(())],
        ),
        compiler_params=pltpu.CompilerParams(dimension_semantics=("arbitrary",),
                                             disable_bounds_checks=True),
        name="moe_scatter",
    )(pad_valid_end, pad_ends, n_used, dest0, dest1, h2t)


def _expert_kernel(be_ref, nu_ref, x_ref, w13_ref, w2_ref, y_ref, w13b, w2b):
    i = pl.program_id(0)

    @pl.when((i == 0) | (be_ref[i] != be_ref[jnp.maximum(i - 1, 0)]))
    def _():
        w13b[...] = w13_ref[0].astype(BF16)
        w2b[...] = w2_ref[0].astype(BF16)

    @pl.when(i < nu_ref[0])
    def _():
        x = _load_token_tiles(x_ref).astype(BF16)
        a = _dot(x, w13b[:, 0:EXPERT_FF])
        g = _dot(x, w13b[:, EXPERT_FF:2 * EXPERT_FF])
        _store_token_tiles(y_ref, _dot((_silu(a) * g).astype(BF16), w2b[...]))

    @pl.when(i >= nu_ref[0])
    def _():
        y_ref[...] = jnp.zeros_like(y_ref)


def _experts(block_expert, n_used, xbuf, w13, w2):
    nb = xbuf.shape[0] // (BM_EXPERT * TOK_SUB)
    used = lambda i, nu: jnp.minimum(i, nu[0] - 1)
    return pl.pallas_call(
        _expert_kernel,
        out_shape=jax.ShapeDtypeStruct(xbuf.shape, U32),
        grid_spec=pltpu.PrefetchScalarGridSpec(
            num_scalar_prefetch=2,
            grid=(nb,),
            in_specs=[
                pl.BlockSpec((BM_EXPERT * TOK_SUB, LANES), lambda i, be, nu: (used(i, nu), 0)),
                pl.BlockSpec((1, D_MODEL, 2 * EXPERT_FF), lambda i, be, nu: (be[i], 0, 0)),
                pl.BlockSpec((1, EXPERT_FF, D_MODEL), lambda i, be, nu: (be[i], 0, 0)),
            ],
            out_specs=pl.BlockSpec((BM_EXPERT * TOK_SUB, LANES), lambda i, be, nu: (i, 0)),
            scratch_shapes=[pltpu.VMEM((D_MODEL, 2 * EXPERT_FF), BF16),
                            pltpu.VMEM((EXPERT_FF, D_MODEL), BF16)],
        ),
        compiler_params=pltpu.CompilerParams(dimension_semantics=("arbitrary",),
                                             vmem_limit_bytes=VMEM_LIMIT),
        name="moe_experts",
    )(block_expert, n_used, xbuf, w13, w2)


def _combine_kernel(d0_ref, d1_ref, x_ref, rf_ref, mod_ref, fg_ref, yb_ref, out_ref, gbuf, sem):
    i = pl.program_id(0)
    n = pl.num_programs(0)
    tc = TC_COMBINE
    slot = lax.rem(i, COMBINE_SLOTS)

    def start_gather(tile, s):
        def issue(j, carry):
            for u in range(ISSUE_UNROLL):
                r = j * ISSUE_UNROLL + u
                for k in range(2):
                    src = (d0_ref, d1_ref)[k][tile * tc + r]
                    pltpu.make_async_copy(_token_tile_at(yb_ref, src), _token_tile(gbuf.at[s, k], r),
                                          sem.at[s, k]).start(priority=k)
            return carry
        lax.fori_loop(0, tc // ISSUE_UNROLL, issue, 0)

    @pl.when(i == 0)
    def _():
        for ahead in range(COMBINE_SLOTS - 1):
            start_gather(ahead, ahead)

    @pl.when(i + COMBINE_SLOTS - 1 < n)
    def _():
        start_gather(i + COMBINE_SLOTS - 1, lax.rem(i + COMBINE_SLOTS - 1, COMBINE_SLOTS))

    for k in range(2):
        pltpu.make_async_copy(yb_ref.at[pl.ds(0, tc * TOK_SUB)], gbuf.at[slot, k], sem.at[slot, k]).wait()
    moe = (rf_ref[:, 0:1] * _load_token_tiles(gbuf.at[slot, 0])
           + rf_ref[:, 1:2] * _load_token_tiles(gbuf.at[slot, 1]))
    out_ref[...] = _rmsnorm(x_ref[...] + mod_ref[0, 5:6, :] * moe, fg_ref[...])


def _combine(dest0, dest1, x_new, rf, mod3, fg, ybuf, seq):
    t = x_new.shape[0]
    n = t // TC_COMBINE
    tiles_per_seq = seq // TC_COMBINE
    return pl.pallas_call(
        _combine_kernel,
        out_shape=jax.ShapeDtypeStruct((t, D_MODEL), F32),
        grid_spec=pltpu.PrefetchScalarGridSpec(
            num_scalar_prefetch=2,
            grid=(n,),
            in_specs=[
                pl.BlockSpec((TC_COMBINE, D_MODEL), lambda i, *_: (i, 0)),
                pl.BlockSpec((TC_COMBINE, LANES), lambda i, *_: (i, 0)),
                pl.BlockSpec((1, 6, D_MODEL), lambda i, *_: (i // tiles_per_seq, 0, 0)),
                pl.BlockSpec((1, D_MODEL), lambda i, *_: (0, 0)),
                pl.BlockSpec(memory_space=pl.ANY),
            ],
            out_specs=pl.BlockSpec((TC_COMBINE, D_MODEL), lambda i, *_: (i, 0)),
            scratch_shapes=[pltpu.VMEM((COMBINE_SLOTS, 2, TC_COMBINE * TOK_SUB, LANES), U32),
                            pltpu.SemaphoreType.DMA((COMBINE_SLOTS, 2))],
        ),
        compiler_params=pltpu.CompilerParams(dimension_semantics=("arbitrary",),
                                             disable_bounds_checks=True),
        name="moe_combine",
    )(dest0, dest1, x_new, rf, mod3, fg, ybuf)


def _pad_lanes(a, n=LANES):
    return jnp.pad(a, [(0, 0)] * (a.ndim - 1) + [(0, n - a.shape[-1])])


def _selection_matrices():
    q = SSD_CHUNK
    rel = jnp.arange(SEQ_HALO + q)[None, :] - jnp.arange(q)[:, None] - SEQ_HALO
    band = jnp.stack([(rel <= 0) & (rel > -w) for w in POOL_WINDOWS]).astype(BF16)
    expand = (jnp.arange(LANES)[:, None] == jnp.arange(SSD_INNER)[None, :] // SSD_HEAD_DIM).astype(BF16)
    return band, expand


def _layer(x2, c, bsz, seq, ada_w, ada_b, norm1_g, w_in, pool_w, pool_scale, conv_w, conv_b, dt_bias,
           a_log, d_skip, ssd_norm_g, w_branch_pool, w_branch_ssd, w_out, norm2_g, router_group_w,
           router_group_b, router_expert_w, router_expert_b, w13, w2):
    t = bsz * seq
    mod3 = _ada(c, ada_w, ada_b).reshape(bsz, 6, D_MODEL)

    i2 = POOL_WIDTH + SSD_INNER + SSD_CONV_DIM
    i3 = i2 + SSD_HEADS
    w_main = jnp.concatenate([w_in[:, :i2], w_in[:, i3:]], axis=1).astype(BF16)
    wd = _pad_lanes(w_in[:, i2:i3])
    wd_hi = wd.astype(BF16)
    wd_lo = (wd - wd_hi.astype(F32)).astype(BF16)

    wr = jnp.concatenate([_pad_lanes(router_group_w), _pad_lanes(router_expert_w)], axis=1).T
    wr_hi = wr.astype(BF16)
    wr_lo = (wr - wr_hi.astype(F32)).astype(BF16)
    br = jnp.concatenate([_pad_lanes(router_group_b.reshape(1, -1)),
                          _pad_lanes(router_expert_b.reshape(1, -1))], axis=1)
    br = jnp.broadcast_to(br.reshape(2 * LANES, 1), (2 * LANES, LANES))
    consts = [
        pool_w.astype(BF16),
        pool_scale.reshape(1, POOL_WIDTH),
        conv_w,
        conv_b.reshape(1, SSD_CONV_DIM),
        _pad_lanes(dt_bias.reshape(1, SSD_HEADS)),
        jnp.broadcast_to(jnp.pad(a_log, (0, LANES - SSD_HEADS))[:, None], (LANES, LANES)),
        jnp.repeat(d_skip, SSD_HEAD_DIM).reshape(1, SSD_INNER),
        ssd_norm_g.reshape(1, SSD_INNER),
        w_branch_pool.astype(BF16),
        w_branch_ssd.astype(BF16),
        w_out.astype(BF16),
        norm2_g.reshape(1, D_MODEL),
        wr_hi, wr_lo, br,
        *_selection_matrices(),
    ]
    x_new, h2, logits_t = _promix(x2, mod3, norm1_g.reshape(1, D_MODEL), w_main, wd_hi, wd_lo, consts, seq)
    idx, rf, cnt = _route(logits_t)

    counts = cnt[:, 0]
    padded = ((counts + BM_EXPERT - 1) // BM_EXPERT) * BM_EXPERT
    pad_ends = jnp.cumsum(padded)
    pad_starts = (pad_ends - padded).astype(I32)
    n_rows = 2 * t + N_EXPERTS * BM_EXPERT
    nb = n_rows // BM_EXPERT
    block_row0 = jnp.arange(nb, dtype=I32) * BM_EXPERT
    block_expert = jnp.minimum(jnp.sum((pad_ends[None, :] <= block_row0[:, None]).astype(I32), axis=1),
                               N_EXPERTS - 1)
    n_used = (pad_ends[-1:] // BM_EXPERT).astype(I32)

    dest = _dest(pad_starts, idx)
    dest0, dest1 = dest[0], dest[1]
    xbuf = _scatter(pad_starts + counts, pad_ends.astype(I32), n_used, dest0, dest1, h2, n_rows)
    ybuf = _experts(block_expert, n_used, xbuf, w13, w2)
    return dest0, dest1, x_new, rf, mod3, ybuf


def kernel(x, c, ada_w, ada_b, norm1_g, w_in, pool_w, pool_scale, conv_w, conv_b, dt_bias, a_log, d_skip, ssd_norm_g, w_branch_pool, w_branch_ssd, w_out, norm2_g, router_group_w, router_group_b, router_expert_w, router_expert_b, w13, w2, final_norm_g):
    bsz, seq, d = x.shape
    depth = ada_w.shape[0]
    assert depth == 1 and d == D_MODEL and seq % TM_MIX == 0
    x2 = x.reshape(bsz * seq, d)
    l = 0
    dest0, dest1, x_new, rf, mod3, ybuf = _layer(
        x2, c, bsz, seq, ada_w[l], ada_b[l], norm1_g[l], w_in[l], pool_w[l], pool_scale[l], conv_w[l],
        conv_b[l], dt_bias[l], a_log[l], d_skip[l], ssd_norm_g[l], w_branch_pool[l], w_branch_ssd[l],
        w_out[l], norm2_g[l], router_group_w[l], router_group_b[l], router_expert_w[l],
        router_expert_b[l], w13[l], w2[l])
    out = _combine(dest0, dest1, x_new, rf, mod3, final_norm_g.reshape(1, d), ybuf, seq)
    return out.reshape(bsz, seq, d)
```
